```python
import math
import jax, jax.numpy as jnp
from jax import lax
import numpy as np

D_MODEL = 1024
BATCH = 16
SEQ = 2048
DEPTH = 1

CHUNK = 64
Q_BLOCK = 128
N_MEM = 256
EPS = 1e-6

DA_HEADS = 4
DA_QK_DIM = 64
DA_V_DIM = 2 * DA_QK_DIM
DA_WIDTH = DA_HEADS * DA_V_DIM
FX_HEADS = 8
FX_DIM = 64
FX_WIDTH = FX_HEADS * FX_DIM
MEM_HEADS = 4
MEM_DIM = 128
MEM_WIDTH = MEM_HEADS * MEM_DIM
N_BRANCH = 3
D_FF = 2816
CONV_W = 3

IN_SIZES = (DA_HEADS * 2 * DA_QK_DIM, DA_HEADS * 2 * DA_QK_DIM, DA_WIDTH,
            FX_WIDTH, FX_WIDTH, FX_WIDTH, FX_HEADS,
            MEM_WIDTH,
            N_BRANCH * D_MODEL)
IN_COLS = sum(IN_SIZES)

kernel_name = "hybrid_diffattn_fox_memxattn_convffn"


def rms_norm(x, g):
    xf = x.astype(jnp.float32)
    y = xf * lax.rsqrt(jnp.mean(xf * xf, axis=-1, keepdims=True) + EPS)
    return (y * g.astype(jnp.float32)).astype(x.dtype)


def lambda_init(layer_idx):
    return 0.8 - 0.6 * math.exp(-0.3 * layer_idx)


def alibi_slopes(n):
    return jnp.array([2.0 ** (-8.0 * (i + 1) / n) for i in range(n)], dtype=jnp.float32)


def split_heads(t, n_heads, d):
    b, s, _ = t.shape
    return t.reshape(b, s, n_heads, d).transpose(0, 2, 1, 3)


def merge_heads(t):
    b, h, s, d = t.shape
    return t.transpose(0, 2, 1, 3).reshape(b, s, h * d)


def diff_attention(q, k, v, lam, slopes):
    s_len = q.shape[3]
    scale = DA_QK_DIM ** -0.5
    pos = jnp.arange(s_len)
    outs = []
    for i in range(s_len // Q_BLOCK):
        q0, q1 = i * Q_BLOCK, (i + 1) * Q_BLOCK
        tq, tk = pos[q0:q1], pos[:q1]
        s = jnp.einsum('bhcqd,bhckd->bhcqk', q[:, :, :, q0:q1], k[:, :, :, :q1]).astype(jnp.float32) * scale
        dist = jnp.abs(tq[:, None] - tk[None, :]).astype(jnp.float32)
        bias = -slopes[:, None, None] * dist
        mask = (tk[None, :] // CHUNK) <= (tq[:, None] // CHUNK)
        s = jnp.where(mask, s + bias[None, :, None], -jnp.inf)
        p = jax.nn.softmax(s, axis=-1)
        w = p[:, :, 0] - lam * p[:, :, 1]
        outs.append(jnp.einsum('bhqk,bhkd->bhqd', w.astype(v.dtype), v[:, :, :q1]))
    return jnp.concatenate(outs, axis=2)


def forgetting_attention(q, k, v, log_f):
    s_len = q.shape[2]
    scale = FX_DIM ** -0.5
    c = jnp.cumsum(log_f, axis=-1)
    pos = jnp.arange(s_len)
    outs = []
    for i in range(s_len // Q_BLOCK):
        q0, q1 = i * Q_BLOCK, (i + 1) * Q_BLOCK
        tq, tk = pos[q0:q1], pos[:q1]
        s = jnp.einsum('bhqd,bhkd->bhqk', q[:, :, q0:q1], k[:, :, :q1]).astype(jnp.float32) * scale
        s = s + c[:, :, q0:q1, None] - c[:, :, None, :q1]
        s = jnp.where(tk[None, :] <= tq[:, None], s, -jnp.inf)
        p = jax.nn.softmax(s, axis=-1)
        outs.append(jnp.einsum('bhqk,bhkd->bhqd', p.astype(v.dtype), v[:, :, :q1]))
    return jnp.concatenate(outs, axis=2)


def memory_attention(q, k, v):
    s = jnp.einsum('bhqd,bhkd->bhqk', q, k).astype(jnp.float32) * (MEM_DIM ** -0.5)
    p = jax.nn.softmax(s, axis=-1)
    return jnp.einsum('bhqk,bhkd->bhqd', p.astype(v.dtype), v)


def setup_inputs(seed: int = 0) -> dict:
    key = jax.random.key(seed)
    ks = jax.random.split(key, 32)
    f32 = jnp.float32

    def nrm(k, shape, scale):
        return jax.random.normal(k, shape, f32) * scale

    def gain(k, shape):
        return 1.0 + 0.05 * jax.random.normal(k, shape, f32)

    L = DEPTH
    return {
        "x": nrm(ks[0], (BATCH, SEQ, D_MODEL), 1.0),
        "mem": nrm(ks[1], (BATCH, N_MEM, D_MODEL), 1.0),
        "norm_mix": gain(ks[2], (L, D_MODEL)),
        "w_in": nrm(ks[3], (L, D_MODEL, IN_COLS), D_MODEL ** -0.5),
        "b_gate": nrm(ks[4], (L, N_BRANCH, D_MODEL), 0.1),
        "da_q_norm": gain(ks[5], (L, DA_QK_DIM)),
        "da_k_norm": gain(ks[6], (L, DA_QK_DIM)),
        "da_lambda_q1": nrm(ks[7], (L, DA_QK_DIM), 0.1),
        "da_lambda_k1": nrm(ks[8], (L, DA_QK_DIM), 0.1),
        "da_lambda_q2": nrm(ks[9], (L, DA_QK_DIM), 0.1),
        "da_lambda_k2": nrm(ks[10], (L, DA_QK_DIM), 0.1),
        "da_subln": gain(ks[11], (L, DA_V_DIM)),
        "fx_q_norm": gain(ks[12], (L, FX_DIM)),
        "fx_k_norm": gain(ks[13], (L, FX_DIM)),
        "fx_f_bias": 3.0 + 0.5 * jax.random.normal(ks[14], (L, FX_HEADS), f32),
        "mem_norm": gain(ks[15], (L, D_MODEL)),
        "w_mem_kv": nrm(ks[16], (L, D_MODEL, 2 * MEM_WIDTH), D_MODEL ** -0.5),
        "mem_q_norm": gain(ks[17], (L, MEM_DIM)),
        "mem_k_norm": gain(ks[18], (L, MEM_DIM)),
        "w_branch_da": nrm(ks[19], (L, DA_WIDTH, D_MODEL), DA_WIDTH ** -0.5),
        "w_branch_fx": nrm(ks[20], (L, FX_WIDTH, D_MODEL), FX_WIDTH ** -0.5),
        "w_branch_mem": nrm(ks[21], (L, MEM_WIDTH, D_MODEL), MEM_WIDTH ** -0.5),
        "w_out": nrm(ks[22], (L, D_MODEL, D_MODEL), D_MODEL ** -0.5),
        "norm_ffn": gain(ks[23], (L, D_MODEL)),
        "w_up": nrm(ks[24], (L, D_MODEL, 2 * D_FF), D_MODEL ** -0.5),
        "conv_w": nrm(ks[25], (L, CONV_W, 2 * D_FF), CONV_W ** -0.5),
        "conv_b": nrm(ks[26], (L, 2 * D_FF), 0.02),
        "w_down": nrm(ks[27], (L, D_FF, D_MODEL), D_FF ** -0.5),
    }


def reference(x, mem, norm_mix, w_in, b_gate, da_q_norm, da_k_norm,
              da_lambda_q1, da_lambda_k1, da_lambda_q2, da_lambda_k2, da_subln,
              fx_q_norm, fx_k_norm, fx_f_bias, mem_norm, w_mem_kv, mem_q_norm, mem_k_norm,
              w_branch_da, w_branch_fx, w_branch_mem, w_out,
              norm_ffn, w_up, conv_w, conv_b, w_down):
    b, s_len, _ = x.shape
    offsets = np.cumsum(np.array(IN_SIZES))[:-1].tolist()
    slopes = alibi_slopes(DA_HEADS)
    for l in range(DEPTH):
        h = rms_norm(x, norm_mix[l])
        proj = h @ w_in[l]
        (a_q, a_k, a_v, f_q, f_k, f_v, f_gate, m_q, g_logit) = jnp.split(proj, offsets, axis=-1)

        lam_init = lambda_init(l)
        qa = a_q.reshape(b, s_len, DA_HEADS, 2, DA_QK_DIM).transpose(0, 2, 3, 1, 4)
        ka = a_k.reshape(b, s_len, DA_HEADS, 2, DA_QK_DIM).transpose(0, 2, 3, 1, 4)
        qa = rms_norm(qa, da_q_norm[l])
        ka = rms_norm(ka, da_k_norm[l])
        va = split_heads(a_v, DA_HEADS, DA_V_DIM)
        lam = (jnp.exp(jnp.sum(da_lambda_q1[l].astype(jnp.float32) * da_lambda_k1[l].astype(jnp.float32)))
               - jnp.exp(jnp.sum(da_lambda_q2[l].astype(jnp.float32) * da_lambda_k2[l].astype(jnp.float32)))
               + lam_init)
        oa = diff_attention(qa, ka, va, lam, slopes)
        oa = rms_norm(oa, da_subln[l]) * (1.0 - lam_init)
        ya = merge_heads(oa) @ w_branch_da[l]

        qf = rms_norm(split_heads(f_q, FX_HEADS, FX_DIM), fx_q_norm[l])
        kf = rms_norm(split_heads(f_k, FX_HEADS, FX_DIM), fx_k_norm[l])
        vf = split_heads(f_v, FX_HEADS, FX_DIM)
        log_f = jax.nn.log_sigmoid((f_gate + fx_f_bias[l]).astype(jnp.float32)).transpose(0, 2, 1)
        of = forgetting_attention(qf, kf, vf, log_f)
        yf = merge_heads(of) @ w_branch_fx[l]

        mh = rms_norm(mem, mem_norm[l])
        m_k, m_v = jnp.split(mh @ w_mem_kv[l], 2, axis=-1)
        qm = rms_norm(split_heads(m_q, MEM_HEADS, MEM_DIM), mem_q_norm[l])
        km = rms_norm(split_heads(m_k, MEM_HEADS, MEM_DIM), mem_k_norm[l])
        vm = split_heads(m_v, MEM_HEADS, MEM_DIM)
        om = memory_attention(qm, km, vm)
        ym = merge_heads(om) @ w_branch_mem[l]

        gates = jax.nn.sigmoid(g_logit.reshape(b, s_len, N_BRANCH, D_MODEL) + b_gate[l])
        merged = gates[:, :, 0] * ya + gates[:, :, 1] * yf + gates[:, :, 2] * ym
        x = x + merged @ w_out[l]

        h2 = rms_norm(x, norm_ffn[l])
        u = h2 @ w_up[l]
        u_pad = jnp.pad(u, ((0, 0), (CONV_W - 1, 0), (0, 0)))
        cw = conv_w[l]
        uc = sum(u_pad[:, j:j + s_len] * cw[j] for j in range(CONV_W)) + conv_b[l]
        a, g = jnp.split(uc, 2, axis=-1)
        x = x + (jax.nn.silu(a) * g) @ w_down[l]
    return x
```

```python
import functools
import math

import jax
import jax.numpy as jnp
import numpy as np
from jax import lax
from jax.experimental import pallas as pl
from jax.experimental.pallas import tpu as pltpu

D_MODEL = 1024
SEQ = 2048
CHUNK = 64
N_MEM = 256
EPS = 1e-6

DA_HEADS = 4
DA_QK_DIM = 64
DA_V_DIM = 128
DA_WIDTH = 512
FX_HEADS = 8
FX_DIM = 64
FX_WIDTH = 512
MEM_HEADS = 4
MEM_DIM = 128
MEM_WIDTH = 512
N_BRANCH = 3
D_FF = 2816
CONV_W = 3

IN_SIZES = (512, 512, DA_WIDTH, FX_WIDTH, FX_WIDTH, FX_WIDTH, FX_HEADS, MEM_WIDTH, N_BRANCH * D_MODEL)

LANES = 128
SUBLANES = 8
MXU_DIM = 256
VMEM_LIMIT = 56 * 1024 * 1024

LOG2E = 1.4426950408889634
NEG = -1e30

TM_PROJ = 512
TM_FFN = 512
TQ = 256
TK = 256
FC = 256

F32 = jnp.float32
BF16 = jnp.bfloat16

_NT = (((1,), (1,)), ((), ()))


def _dot(a, b):
    return jnp.dot(a, b, preferred_element_type=F32)


def _dot_nt(a, b):
    return lax.dot_general(a, b, _NT, preferred_element_type=F32)


def _rms_rows(x, g):
    ms = jnp.mean(x * x, axis=-1, keepdims=True)
    return x * lax.rsqrt(ms + EPS) * g


def _const_spec(shape):
    nd = len(shape)
    return pl.BlockSpec(shape, lambda *_: (0,) * nd, pipeline_mode=pl.Buffered(1))


def _params(*sem):
    return pltpu.CompilerParams(dimension_semantics=sem, vmem_limit_bytes=VMEM_LIMIT)


def _inproj_kernel(x_ref, nm_ref, wqk_ref, wmq_ref, wv_ref, wfg_ref, gqk_ref, gmq_ref, fb_ref,
                   g64_ref, g128_ref, tri_ref,
                   qa_ref, ka_ref, qf_ref, kf_ref, va_ref, vf_ref, qm_ref, c_ref, carry_ref):
    tm = x_ref.shape[0]
    h = _rms_rows(x_ref[...], nm_ref[...]).astype(BF16)

    def normed(w_ref, gain_ref, group_ref, inv_dim, col):
        y = _dot(h, w_ref[:, col:col + MXU_DIM])
        ss = _dot((y * y).astype(BF16), group_ref[...])
        return (y * lax.rsqrt(ss * inv_dim + EPS) * gain_ref[:, col:col + MXU_DIM]).astype(BF16)

    for t, out_ref in enumerate((qa_ref, ka_ref, qf_ref, kf_ref)):
        for s in range(2):
            yn = normed(wqk_ref, gqk_ref, g64_ref, 1.0 / DA_QK_DIM, t * 512 + s * MXU_DIM)
            out_ref[2 * s] = yn[:, :LANES]
            out_ref[2 * s + 1] = yn[:, LANES:]

    for s in range(2):
        qm_ref[:, s * MXU_DIM:(s + 1) * MXU_DIM] = normed(wmq_ref, gmq_ref, g128_ref, 1.0 / MEM_DIM, s * MXU_DIM)

    for t, out_ref in enumerate((va_ref, vf_ref)):
        for s in range(2):
            col = t * 512 + s * MXU_DIM
            v = _dot(h, wv_ref[:, col:col + MXU_DIM]).astype(BF16)
            out_ref[2 * s] = v[:, :LANES]
            out_ref[2 * s + 1] = v[:, LANES:]

    fg = _dot(h, wfg_ref[...]) + fb_ref[...]
    logf = jnp.minimum(fg, 0.0) - jnp.log(1.0 + jnp.exp(-jnp.abs(fg)))
    hi = logf.astype(BF16)
    r1 = logf - hi.astype(F32)
    mid = r1.astype(BF16)
    lo = (r1 - mid.astype(F32)).astype(BF16)
    tri = tri_ref[...]
    local = _dot(tri, hi) + _dot(tri, mid) + _dot(tri, lo)

    @pl.when(pl.program_id(1) == 0)
    def _():
        carry_ref[...] = jnp.zeros_like(carry_ref)

    c = local + carry_ref[0:1, :]
    carry_ref[...] = jnp.broadcast_to(c[tm - 1:tm, :], carry_ref.shape)
    c_ref[...] = (c * LOG2E).T[0:FX_HEADS, :]


def _in_projection(x, nm, wqk, wmq, wv, wfg, gqk, gmq, fb, g64, g128, tri):
    b, s, d = x.shape
    tm = TM_PROJ
    head_major = jax.ShapeDtypeStruct((b, 4, s, LANES), BF16)
    head_spec = pl.BlockSpec((None, 4, tm, LANES), lambda bi, i: (bi, 0, i, 0))
    return pl.pallas_call(
        _inproj_kernel,
        grid=(b, s // tm),
        in_specs=[
            pl.BlockSpec((None, tm, d), lambda bi, i: (bi, i, 0)),
            _const_spec(nm.shape), _const_spec(wqk.shape), _const_spec(wmq.shape), _const_spec(wv.shape),
            _const_spec(wfg.shape), _const_spec(gqk.shape), _const_spec(gmq.shape), _const_spec(fb.shape),
            _const_spec(g64.shape), _const_spec(g128.shape), _const_spec(tri.shape),
        ],
        out_specs=[head_spec] * 6 + [
            pl.BlockSpec((None, tm, MEM_WIDTH), lambda bi, i: (bi, i, 0)),
            pl.BlockSpec((None, FX_HEADS, tm), lambda bi, i: (bi, 0, i)),
        ],
        out_shape=[head_major] * 6 + [
            jax.ShapeDtypeStruct((b, s, MEM_WIDTH), BF16),
            jax.ShapeDtypeStruct((b, FX_HEADS, s), F32),
        ],
        scratch_shapes=[pltpu.VMEM((SUBLANES, LANES), F32)],
        compiler_params=_params("parallel", "arbitrary"),
        name="in_projection",
    )(x, nm, wqk, wmq, wv, wfg, gqk, gmq, fb, g64, g128, tri)


def _memkv_kernel(mem_ref, nm_ref, w_ref, gk_ref, g128_ref, km_ref, vm_ref):
    mh = _rms_rows(mem_ref[...], nm_ref[...]).astype(BF16)
    for s in range(2):
        sl = slice(s * MXU_DIM, (s + 1) * MXU_DIM)
        y = _dot(mh, w_ref[:, sl])
        ss = _dot((y * y).astype(BF16), g128_ref[...])
        km_ref[:, sl] = (y * lax.rsqrt(ss * (1.0 / MEM_DIM) + EPS) * gk_ref[:, sl]).astype(BF16)
        vm_ref[:, sl] = _dot(mh, w_ref[:, MEM_WIDTH + s * MXU_DIM:MEM_WIDTH + (s + 1) * MXU_DIM]).astype(BF16)


def _memory_kv(mem, nm, w, gk, g128):
    b, n, d = mem.shape
    out = jax.ShapeDtypeStruct((b, n, MEM_WIDTH), BF16)
    spec = pl.BlockSpec((None, n, MEM_WIDTH), lambda bi: (bi, 0, 0))
    return pl.pallas_call(
        _memkv_kernel,
        grid=(b,),
        in_specs=[pl.BlockSpec((None, n, d), lambda bi: (bi, 0, 0)),
                  _const_spec(nm.shape), _const_spec(w.shape), _const_spec(gk.shape), _const_spec(g128.shape)],
        out_specs=[spec, spec],
        out_shape=[out, out],
        compiler_params=_params("parallel"),
        name="memory_kv",
    )(mem, nm, w, gk, g128)


def _online_step(z, shift, m_ref, l_ref, idx):
    m_prev = m_ref[idx]
    m_curr = jnp.max(z, axis=1)[:, None] + shift
    m_next = jnp.maximum(m_prev, m_curr)
    alpha = jnp.exp2(m_prev - m_next)
    p = jnp.exp2(z - pltpu.repeat(m_next - shift, z.shape[1] // LANES, 1))
    l_ref[idx] = alpha * l_ref[idx] + jnp.sum(p, axis=1)[:, None]
    m_ref[idx] = m_next
    return p.astype(BF16), alpha


def _diff_kernel(slope_ref, q_ref, k_ref, v_ref, lq1_ref, lk1_ref, lq2_ref, lk2_ref, sub_ref, o_ref,
                 acc_ref, m_ref, l_ref, rel_ref, diag_ref, *, lam_init):
    hd = pl.program_id(1)
    i = pl.program_id(2)
    slope = slope_ref[hd] * LOG2E

    @pl.when(i == 0)
    def _():
        r = lax.broadcasted_iota(jnp.int32, (TQ, TK), 0)
        c = lax.broadcasted_iota(jnp.int32, (TQ, TK), 1)
        d = (r - c).astype(F32)
        rel_ref[...] = -slope * d
        visible = (c // CHUNK) <= (r // CHUNK)
        diag_ref[...] = jnp.where(visible, -slope * jnp.abs(d), NEG)

    m_ref[...] = jnp.full_like(m_ref, NEG)
    l_ref[...] = jnp.zeros_like(l_ref)
    acc_ref[...] = jnp.zeros_like(acc_ref)

    lane = lax.broadcasted_iota(jnp.int32, (TQ, LANES), 1)
    q = q_ref[...]
    zero = jnp.zeros_like(q)
    qs = (jnp.where(lane < DA_QK_DIM, q, zero), jnp.where(lane >= DA_QK_DIM, q, zero))

    def block(j, bias_ref, shift):
        start = pl.multiple_of(j * TK, TK)
        kb = k_ref[pl.ds(start, TK), :]
        vb = v_ref[pl.ds(start, TK), :]
        for c in range(2):
            z = _dot_nt(qs[c], kb) + bias_ref[...]
            p, alpha = _online_step(z, shift, m_ref, l_ref, c)
            acc_ref[c] = alpha * acc_ref[c] + _dot(p, vb)

    block(i, diag_ref, 0.0)

    def body(j, carry):
        block(j, rel_ref, -slope * ((i - j) * TK).astype(F32))
        return carry

    lax.fori_loop(0, i, body, 0)

    lam = (jnp.exp(jnp.sum(lq1_ref[...] * lk1_ref[...], axis=1, keepdims=True))
           - jnp.exp(jnp.sum(lq2_ref[...] * lk2_ref[...], axis=1, keepdims=True)) + lam_init)
    o = acc_ref[0] / l_ref[0] - lam * (acc_ref[1] / l_ref[1])
    o = _rms_rows(o, sub_ref[...]) * (1.0 - lam_init)
    o_ref[...] = o.astype(BF16)


def _diff_attention(q, k, v, slopes, lq1, lk1, lq2, lk2, sub, lam_init):
    b, nh, s, _ = q.shape
    kv_spec = pl.BlockSpec((None, None, s, LANES), lambda bi, h, i: (bi, h, 0, 0))
    vec_spec = _const_spec(lq1.shape)
    return pl.pallas_call(
        functools.partial(_diff_kernel, lam_init=lam_init),
        grid=(b, nh, s // TQ),
        in_specs=[
            pl.BlockSpec(memory_space=pltpu.SMEM),
            pl.BlockSpec((None, None, TQ, LANES), lambda bi, h, i: (bi, h, i, 0)),
            kv_spec, kv_spec, vec_spec, vec_spec, vec_spec, vec_spec, _const_spec(sub.shape),
        ],
        out_specs=pl.BlockSpec((None, TQ, LANES), lambda bi, h, i: (bi, i, h)),
        out_shape=jax.ShapeDtypeStruct((b, s, nh * LANES), BF16),
        scratch_shapes=[
            pltpu.VMEM((2, TQ, LANES), F32), pltpu.VMEM((2, TQ, LANES), F32), pltpu.VMEM((2, TQ, LANES), F32),
            pltpu.VMEM((TQ, TK), F32), pltpu.VMEM((TQ, TK), F32),
        ],
        compiler_params=_params("parallel", "parallel", "arbitrary"),
        name="diff_attention",
    )(slopes, q, k, v, lq1, lk1, lq2, lk2, sub)


def _fox_kernel(q_ref, k_ref, v_ref, c_ref, o_ref, acc_ref, m_ref, l_ref):
    i = pl.program_id(2)
    m_ref[...] = jnp.full_like(m_ref, NEG)
    l_ref[...] = jnp.zeros_like(l_ref)
    acc_ref[...] = jnp.zeros_like(acc_ref)

    lane = lax.broadcasted_iota(jnp.int32, (TQ, LANES), 1)
    low = lane < FX_DIM
    q = q_ref[...]
    zero = jnp.zeros_like(q)
    qs = (jnp.where(low, q, zero), jnp.where(low, zero, q))

    def block(j, causal):
        start = pl.multiple_of(j * TK, TK)
        kb = k_ref[pl.ds(start, TK), :]
        vb = v_ref[pl.ds(start, TK), :]
        vzero = jnp.zeros_like(vb)
        vs = (jnp.where(low, vb, vzero), jnp.where(low, vzero, vb))
        pv = None
        alphas = []
        for hh in range(2):
            z = _dot_nt(qs[hh], kb) - c_ref[hh, pl.ds(j, 1), :]
            if causal:
                r = lax.broadcasted_iota(jnp.int32, (TQ, TK), 0)
                c = lax.broadcasted_iota(jnp.int32, (TQ, TK), 1)
                z = jnp.where(c <= r, z, NEG)
            p, alpha = _online_step(z, 0.0, m_ref, l_ref, hh)
            alphas.append(alpha)
            d = _dot(p, vs[hh])
            pv = d if pv is None else pv + d
        acc_ref[...] = jnp.where(low, alphas[0], alphas[1]) * acc_ref[...] + pv

    block(i, True)

    def body(j, carry):
        block(j, False)
        return carry

    lax.fori_loop(0, i, body, 0)
    o_ref[...] = (acc_ref[...] / jnp.where(low, l_ref[0], l_ref[1])).astype(BF16)


def _fox_attention(q, k, v, c):
    b, npair, s, _ = q.shape
    kv_spec = pl.BlockSpec((None, None, s, LANES), lambda bi, h, i: (bi, h, 0, 0))
    return pl.pallas_call(
        _fox_kernel,
        grid=(b, npair, s // TQ),
        in_specs=[
            pl.BlockSpec((None, None, TQ, LANES), lambda bi, h, i: (bi, h, i, 0)),
            kv_spec, kv_spec,
            pl.BlockSpec((None, None, 2, s // TK, TK), lambda bi, h, i: (bi, h, 0, 0, 0)),
        ],
        out_specs=pl.BlockSpec((None, TQ, LANES), lambda bi, h, i: (bi, i, h)),
        out_shape=jax.ShapeDtypeStruct((b, s, npair * LANES), BF16),
        scratch_shapes=[
            pltpu.VMEM((TQ, LANES), F32), pltpu.VMEM((2, TQ, LANES), F32), pltpu.VMEM((2, TQ, LANES), F32),
        ],
        compiler_params=_params("parallel", "parallel", "arbitrary"),
        name="fox_attention",
    )(q, k, v, c)


def _merge_kernel(x_ref, nm_ref, oa_ref, of_ref, qm_ref, km_ref, vm_ref, wg_ref, bg_ref,
                  wda_ref, wfx_ref, wmem_ref, wout_ref, o_ref, om_ref, merged_ref):
    x = x_ref[...]
    h = _rms_rows(x, nm_ref[...]).astype(BF16)

    for hd in range(MEM_HEADS):
        sl = slice(hd * MEM_DIM, (hd + 1) * MEM_DIM)
        s = _dot_nt(qm_ref[:, sl], km_ref[:, sl])
        p = jnp.exp2(s - jnp.max(s, axis=1, keepdims=True))
        inv = 1.0 / jnp.sum(p, axis=1, keepdims=True)
        om_ref[:, sl] = (_dot(p.astype(BF16), vm_ref[:, sl]) * inv).astype(BF16)

    branches = ((oa_ref, wda_ref), (of_ref, wfx_ref), (om_ref, wmem_ref))
    for n in range(D_MODEL // MXU_DIM):
        sl = slice(n * MXU_DIM, (n + 1) * MXU_DIM)
        merged = None
        for br, (o_br, w_br) in enumerate(branches):
            gsl = slice(br * D_MODEL + n * MXU_DIM, br * D_MODEL + (n + 1) * MXU_DIM)
            gate = jax.nn.sigmoid(_dot(h, wg_ref[:, gsl]) + bg_ref[:, gsl])
            term = gate * _dot(o_br[...], w_br[:, sl])
            merged = term if merged is None else merged + term
        merged_ref[:, sl] = merged.astype(BF16)

    o_ref[...] = x + _dot(merged_ref[...], wout_ref[...])


def _merge(x, nm, oa, of, qm, km, vm, wg, bg, wda, wfx, wmem, wout):
    b, s, d = x.shape
    tm = TM_PROJ
    row = lambda width: pl.BlockSpec((None, tm, width), lambda bi, i: (bi, i, 0))
    mem_spec = pl.BlockSpec((None, N_MEM, MEM_WIDTH), lambda bi, i: (bi, 0, 0))
    return pl.pallas_call(
        _merge_kernel,
        grid=(b, s // tm),
        in_specs=[row(d), _const_spec(nm.shape), row(DA_WIDTH), row(FX_WIDTH), row(MEM_WIDTH), mem_spec, mem_spec,
                  _const_spec(wg.shape), _const_spec(bg.shape), _const_spec(wda.shape), _const_spec(wfx.shape),
                  _const_spec(wmem.shape), _const_spec(wout.shape)],
        out_specs=row(d),
        out_shape=jax.ShapeDtypeStruct((b, s, d), F32),
        scratch_shapes=[pltpu.VMEM((tm, MEM_WIDTH), BF16), pltpu.VMEM((tm, d), BF16)],
        compiler_params=_params("parallel", "parallel"),
        name="merge",
    )(x, nm, oa, of, qm, km, vm, wg, bg, wda, wfx, wmem, wout)


def _ffn_kernel(x_ref, nf_ref, wup_ref, cw_ref, cb_ref, wdn_ref, o_ref, carry_ref, act_ref):
    tm = x_ref.shape[0]

    @pl.when(pl.program_id(1) == 0)
    def _():
        carry_ref[...] = jnp.zeros_like(carry_ref)

    x = x_ref[...]
    h2 = _rms_rows(x, nf_ref[...]).astype(BF16)
    row = lax.broadcasted_iota(jnp.int32, (SUBLANES, FC), 0)

    def conv(col):
        sl = slice(col, col + FC)
        u = _dot(h2, wup_ref[:, sl])
        prev = carry_ref[:, sl]
        carry_ref[:, sl] = u[tm - SUBLANES:tm, :]
        out = u * cw_ref[CONV_W - 1:CONV_W, sl] + cb_ref[:, sl]
        for shift in range(1, CONV_W):
            us = pltpu.roll(u, shift, 0)
            head = jnp.where(row < shift, pltpu.roll(prev, shift, 0), us[0:SUBLANES, :])
            us = jnp.concatenate([head, us[SUBLANES:, :]], axis=0)
            out = out + us * cw_ref[CONV_W - 1 - shift:CONV_W - shift, sl]
        return out

    for ch in range(D_FF // FC):
        a = conv(ch * FC)
        g = conv(D_FF + ch * FC)
        act_ref[:, ch * FC:(ch + 1) * FC] = (a * jax.nn.sigmoid(a) * g).astype(BF16)

    o_ref[...] = x + _dot(act_ref[...], wdn_ref[...])


def _ffn(x, nf, wup, cw, cb, wdn):
    b, s, d = x.shape
    tm = TM_FFN
    row = pl.BlockSpec((None, tm, d), lambda bi, i: (bi, i, 0))
    return pl.pallas_call(
        _ffn_kernel,
        grid=(b, s // tm),
        in_specs=[row, _const_spec(nf.shape), _const_spec(wup.shape), _const_spec(cw.shape), _const_spec(cb.shape),
                  _const_spec(wdn.shape)],
        out_specs=row,
        out_shape=jax.ShapeDtypeStruct((b, s, d), F32),
        scratch_shapes=[pltpu.VMEM((SUBLANES, 2 * D_FF), F32), pltpu.VMEM((tm, D_FF), BF16)],
        compiler_params=_params("parallel", "arbitrary"),
        name="conv_mlp",
    )(x, nf, wup, cw, cb, wdn)


def _block_diag_ones(group):
    idx = np.arange(MXU_DIM) // group
    return jnp.asarray(idx[:, None] == idx[None, :], dtype=BF16)


def kernel(x, mem, norm_mix, w_in, b_gate, da_q_norm, da_k_norm, da_lambda_q1, da_lambda_k1, da_lambda_q2,
           da_lambda_k2, da_subln, fx_q_norm, fx_k_norm, fx_f_bias, mem_norm, w_mem_kv, mem_q_norm, mem_k_norm,
           w_branch_da, w_branch_fx, w_branch_mem, w_out, norm_ffn, w_up, conv_w, conv_b, w_down):
    b, s, d = x.shape
    depth = w_in.shape[0]
    off = np.cumsum(np.array(IN_SIZES))[:-1].tolist()
    slopes = jnp.asarray([2.0 ** (-8.0 * (i + 1) / DA_HEADS) for i in range(DA_HEADS)], dtype=F32)
    g64 = _block_diag_ones(DA_QK_DIM)
    g128 = _block_diag_ones(MEM_DIM)
    tri = jnp.asarray(np.tril(np.ones((TM_PROJ, TM_PROJ))), dtype=BF16)
    row = lambda v: v.reshape(1, -1).astype(F32)

    for l in range(depth):
        lam_init = 0.8 - 0.6 * math.exp(-0.3 * l)
        w = w_in[l]
        a_q, a_k, a_v, f_q, f_k, f_v, f_g, m_q, w_g = jnp.split(w, off, axis=-1)
        wqk = jnp.concatenate([a_q, a_k, f_q, f_k], axis=1).astype(BF16)
        wv = jnp.concatenate([a_v, f_v], axis=1).astype(BF16)
        wfg = jnp.pad(f_g, ((0, 0), (0, LANES - FX_HEADS))).astype(BF16)
        fb = jnp.pad(fx_f_bias[l], (0, LANES - FX_HEADS)).reshape(1, LANES).astype(F32)
        qk_scale = DA_QK_DIM ** -0.5 * LOG2E
        gqk = jnp.concatenate([
            jnp.tile(da_q_norm[l], 2 * DA_HEADS) * qk_scale, jnp.tile(da_k_norm[l], 2 * DA_HEADS),
            jnp.tile(fx_q_norm[l], FX_HEADS) * (FX_DIM ** -0.5 * LOG2E), jnp.tile(fx_k_norm[l], FX_HEADS),
        ]).reshape(1, -1).astype(F32)
        gmq = row(jnp.tile(mem_q_norm[l], MEM_HEADS) * (MEM_DIM ** -0.5 * LOG2E))
        gmk = row(jnp.tile(mem_k_norm[l], MEM_HEADS))

        qa, ka, qf, kf, va, vf, qm, c = _in_projection(
            x, row(norm_mix[l]), wqk, m_q.astype(BF16), wv, wfg, gqk, gmq, fb, g64, g128, tri)
        km, vm = _memory_kv(mem, row(mem_norm[l]), w_mem_kv[l].astype(BF16), gmk, g128)

        oa = _diff_attention(qa, ka, va, slopes, row(da_lambda_q1[l]), row(da_lambda_k1[l]),
                             row(da_lambda_q2[l]), row(da_lambda_k2[l]), row(da_subln[l]), lam_init)
        of = _fox_attention(qf, kf, vf, c.reshape(b, FX_HEADS // 2, 2, s // TK, TK))

        x = _merge(x, row(norm_mix[l]), oa, of, qm, km, vm, w_g.astype(BF16), b_gate[l].reshape(1, -1).astype(F32),
                   w_branch_da[l].astype(BF16), w_branch_fx[l].astype(BF16), w_branch_mem[l].astype(BF16),
                   w_out[l].astype(BF16))
        x = _ffn(x, row(norm_ffn[l]), w_up[l].astype(BF16), conv_w[l].astype(F32), row(conv_b[l]),
                 w_down[l].astype(BF16))
    return x
```

```python
import functools
import math

import jax
import jax.numpy as jnp
import numpy as np
from jax import lax
from jax.experimental import pallas as pl
from jax.experimental.pallas import tpu as pltpu

D_MODEL = 1024
CHUNK = 64
N_MEM = 256
EPS = 1e-6

DA_HEADS = 4
DA_QK_DIM = 64
DA_V_DIM = 128
DA_WIDTH = 512
FX_HEADS = 8
FX_DIM = 64
FX_WIDTH = 512
MEM_HEADS = 4
MEM_DIM = 128
MEM_WIDTH = 512
N_BRANCH = 3
D_FF = 2816
CONV_W = 3

IN_SIZES = (512, 512, DA_WIDTH, FX_WIDTH, FX_WIDTH, FX_WIDTH, FX_HEADS, MEM_WIDTH, N_BRANCH * D_MODEL)

LANES = 128
SUBLANES = 8
MXU_DIM = 256
VMEM_LIMIT = 56 * 1024 * 1024

LOG2E = 1.4426950408889634
NEG = -1e30

TM_PROJ = 512
TM_FFN = 512
TB = 256
FC = 256
N_SPLIT = 3
HALF = LANES // 2

F32 = jnp.float32
BF16 = jnp.bfloat16

_NT = (((1,), (1,)), ((), ()))


def _dot(a, b):
    return jnp.dot(a, b, preferred_element_type=F32)


def _dot_nt(a, b):
    return lax.dot_general(a, b, _NT, preferred_element_type=F32)


def _rms_rows(x, g):
    ms = jnp.mean(x * x, axis=-1, keepdims=True)
    return x * lax.rsqrt(ms + EPS) * g


def _three_way(a):
    hi = a.astype(BF16)
    r = a - hi.astype(F32)
    mid = r.astype(BF16)
    lo = (r - mid.astype(F32)).astype(BF16)
    return hi, mid, lo


def _const_spec(shape):
    nd = len(shape)
    return pl.BlockSpec(shape, lambda *_: (0,) * nd, pipeline_mode=pl.Buffered(1))


def _params(*sem):
    return pltpu.CompilerParams(dimension_semantics=sem, vmem_limit_bytes=VMEM_LIMIT)


def _inproj_kernel(x_ref, nm_ref, wqt_ref, wk_ref, wvt_ref, wmq_ref, wfg_ref, gqt_ref, gk_ref, gmq_ref, fb_ref,
                   part_ref, pos_ref, pat_ref, g64_ref, g128_ref, tri_ref,
                   qat_ref, qft_ref, ka_ref, kf_ref, vat_ref, vft_ref, qm_ref, carry_ref, *, slopes):
    tm = x_ref.shape[0]
    nb = tm // TB
    h = _rms_rows(x_ref[...], nm_ref[...]).astype(BF16)
    lane = lax.broadcasted_iota(jnp.int32, (tm, LANES), 1)

    def store_blocks(out_ref, idx, yt):
        for jb in range(nb):
            out_ref[idx, jb] = yt[:, jb * TB:(jb + 1) * TB]

    for s in range(4):
        rows = slice(s * MXU_DIM, (s + 1) * MXU_DIM)
        yt = _dot_nt(wqt_ref[rows, :], h)
        y3 = yt.reshape(MXU_DIM // DA_QK_DIM, DA_QK_DIM, tm)
        inv = lax.rsqrt(jnp.mean(y3 * y3, axis=1, keepdims=True) + EPS)
        yn = ((y3 * inv).reshape(MXU_DIM, tm) * pltpu.repeat(gqt_ref[rows, :], tm // LANES, 1)).astype(BF16)
        out_ref = (qat_ref, qft_ref)[s // 2]
        for t in range(2):
            hd = 2 * (s % 2) + t
            ones_rows = pat_ref[hd * (s // 2)]
            store_blocks(out_ref, 2 * hd, jnp.concatenate([yn[t * LANES:t * LANES + HALF], ones_rows], axis=0))
            store_blocks(out_ref, 2 * hd + 1, jnp.concatenate([ones_rows, yn[t * LANES + HALF:(t + 1) * LANES]], axis=0))

    for s in range(4):
        rows = slice(s * MXU_DIM, (s + 1) * MXU_DIM)
        vt = _dot_nt(wvt_ref[rows, :], h).astype(BF16)
        out_ref = (vat_ref, vft_ref)[s // 2]
        store_blocks(out_ref, 2 * (s % 2), vt[:LANES])
        store_blocks(out_ref, 2 * (s % 2) + 1, vt[LANES:])

    def normed(w_ref, gain_ref, group_ref, inv_dim, col):
        y = _dot(h, w_ref[:, col:col + MXU_DIM])
        ss = _dot((y * y).astype(BF16), group_ref[...])
        return (y * lax.rsqrt(ss * inv_dim + EPS) * gain_ref[:, col:col + MXU_DIM]).astype(BF16)

    for s in range(2):
        qm_ref[:, s * MXU_DIM:(s + 1) * MXU_DIM] = normed(wmq_ref, gmq_ref, g128_ref, 1.0 / MEM_DIM, s * MXU_DIM)

    def split_tile(a, base):
        hi, mid, lo = _three_way(a)
        zero = jnp.zeros_like(hi)
        return jnp.where(lane == base, hi, jnp.where(lane == base + 1, mid, jnp.where(lane == base + 2, lo, zero)))

    pos = pos_ref[...] + (pl.program_id(1) * tm).astype(F32)
    for s in range(2):
        yn = normed(wk_ref, gk_ref, g64_ref, 1.0 / DA_QK_DIM, s * MXU_DIM)
        for t in range(2):
            hd = 2 * s + t
            kt = yn[:, t * LANES:(t + 1) * LANES]
            a = pos * (slopes[hd] * LOG2E)
            ka_ref[2 * hd] = jnp.where(lane < HALF, kt, split_tile(a, HALF))
            ka_ref[2 * hd + 1] = jnp.where(lane >= HALF, kt, split_tile(a, 0))

    fg = _dot(h, wfg_ref[...]) + fb_ref[...]
    logf = jnp.minimum(fg, 0.0) - jnp.log(1.0 + jnp.exp(-jnp.abs(fg)))
    tri = tri_ref[...]
    local = sum(_dot(tri, piece) for piece in _three_way(logf))

    @pl.when(pl.program_id(1) == 0)
    def _():
        carry_ref[...] = jnp.zeros_like(carry_ref)

    c = local + carry_ref[0:1, :]
    carry_ref[...] = jnp.broadcast_to(c[tm - 1:tm, :], carry_ref.shape)
    hi, mid, lo = _three_way(c * (-LOG2E))
    part = jnp.broadcast_to(part_ref[...], (tm, LANES))
    zero = jnp.zeros_like(hi)
    feat = jnp.where(part == 0, hi, jnp.where(part == 1, mid, jnp.where(part == 2, lo, zero)))

    for s in range(2):
        yn = normed(wk_ref, gk_ref, g64_ref, 1.0 / FX_DIM, DA_WIDTH + s * MXU_DIM)
        for t in range(2):
            a = 2 * s + t
            kt = yn[:, t * LANES:(t + 1) * LANES]
            fa = jnp.where(lane < HALF + N_SPLIT * a, zero, jnp.where(lane < HALF + N_SPLIT * (a + 1), feat, zero))
            fb = jnp.where(lane < N_SPLIT * a, zero, jnp.where(lane < N_SPLIT * (a + 1), feat, zero))
            kf_ref[2 * a] = jnp.where(lane < HALF, kt, fa)
            kf_ref[2 * a + 1] = jnp.where(lane >= HALF, kt, fb)


def _in_projection(x, nm, wqt, wk, wvt, wmq, wfg, gqt, gk, gmq, fb, part, pos, pat, g64, g128, tri, slopes):
    b, s, d = x.shape
    tm = TM_PROJ
    t_major = jax.ShapeDtypeStruct((b, 4, s // TB, LANES, TB), BF16)
    t_spec = pl.BlockSpec((None, 4, tm // TB, LANES, TB), lambda bi, i: (bi, 0, i, 0, 0))
    q_major = jax.ShapeDtypeStruct((b, 8, s // TB, LANES, TB), BF16)
    q_spec = pl.BlockSpec((None, 8, tm // TB, LANES, TB), lambda bi, i: (bi, 0, i, 0, 0))
    k_major = jax.ShapeDtypeStruct((b, 8, s, LANES), BF16)
    k_spec = pl.BlockSpec((None, 8, tm, LANES), lambda bi, i: (bi, 0, i, 0))
    consts = (nm, wqt, wk, wvt, wmq, wfg, gqt, gk, gmq, fb, part, pos, pat, g64, g128, tri)
    return pl.pallas_call(
        functools.partial(_inproj_kernel, slopes=slopes),
        grid=(b, s // tm),
        in_specs=[pl.BlockSpec((None, tm, d), lambda bi, i: (bi, i, 0))] + [_const_spec(c.shape) for c in consts],
        out_specs=[q_spec, q_spec, k_spec, k_spec, t_spec, t_spec,
                   pl.BlockSpec((None, tm, MEM_WIDTH), lambda bi, i: (bi, i, 0))],
        out_shape=[q_major, q_major, k_major, k_major, t_major, t_major,
                   jax.ShapeDtypeStruct((b, s, MEM_WIDTH), BF16)],
        scratch_shapes=[pltpu.VMEM((SUBLANES, LANES), F32)],
        compiler_params=_params("parallel", "arbitrary"),
        name="in_projection",
    )(x, *consts)


def _memkv_kernel(mem_ref, nm_ref, w_ref, gk_ref, g128_ref, km_ref, vm_ref):
    mh = _rms_rows(mem_ref[...], nm_ref[...]).astype(BF16)
    for s in range(2):
        sl = slice(s * MXU_DIM, (s + 1) * MXU_DIM)
        y = _dot(mh, w_ref[:, sl])
        ss = _dot((y * y).astype(BF16), g128_ref[...])
        km_ref[:, sl] = (y * lax.rsqrt(ss * (1.0 / MEM_DIM) + EPS) * gk_ref[:, sl]).astype(BF16)
        vm_ref[:, sl] = _dot(mh, w_ref[:, MEM_WIDTH + s * MXU_DIM:MEM_WIDTH + (s + 1) * MXU_DIM]).astype(BF16)


def _memory_kv(mem, nm, w, gk, g128):
    b, n, d = mem.shape
    out = jax.ShapeDtypeStruct((b, n, MEM_WIDTH), BF16)
    spec = pl.BlockSpec((None, n, MEM_WIDTH), lambda bi: (bi, 0, 0))
    return pl.pallas_call(
        _memkv_kernel,
        grid=(b,),
        in_specs=[pl.BlockSpec((None, n, d), lambda bi: (bi, 0, 0)),
                  _const_spec(nm.shape), _const_spec(w.shape), _const_spec(gk.shape), _const_spec(g128.shape)],
        out_specs=[spec, spec],
        out_shape=[out, out],
        compiler_params=_params("parallel"),
        name="memory_kv",
    )(mem, nm, w, gk, g128)


def _softmax_next(z, m_prev, l_prev):
    m = jnp.maximum(m_prev, jnp.max(z, axis=0, keepdims=True))
    alpha = jnp.exp2(m_prev - m)
    p = jnp.exp2(z - m)
    return p.astype(BF16), m, alpha * l_prev + jnp.sum(p, axis=0, keepdims=True), alpha


def _attend(i, qt_ref, k_ref, v_block, corr_ref, acc_ref, za_ref, zb_ref, finish):
    q_aug = (qt_ref[0], qt_ref[1])

    def scores(z_ref, j):
        st = pl.multiple_of(j * TB, TB)
        for c in range(2):
            z_ref[c] = _dot(k_ref[c, pl.ds(st, TB), :], q_aug[c])

    def consume(z_ref, j, state, diag):
        out = []
        for c in range(2):
            z = z_ref[c] + corr_ref[...] if diag else z_ref[c]
            p, m, l, alpha = _softmax_next(z, state[2 * c], state[2 * c + 1])
            acc_ref[c] = alpha * acc_ref[c] + _dot(v_block(j, c), p)
            out += [m, l]
        return tuple(out)

    acc_ref[...] = jnp.zeros_like(acc_ref)
    m0 = jnp.full((1, TB), NEG, F32)
    l0 = jnp.zeros((1, TB), F32)
    scores(za_ref, 0)

    def pair(u, state):
        j = 2 * u
        scores(zb_ref, j + 1)
        state = consume(za_ref, j, state, False)
        scores(za_ref, j + 2)
        return consume(zb_ref, j + 1, state, False)

    state = lax.fori_loop(0, i // 2, pair, (m0, l0, m0, l0))

    @pl.when(i % 2 == 0)
    def _():
        finish(consume(za_ref, i, state, True))

    @pl.when(i % 2 == 1)
    def _():
        scores(zb_ref, i)
        finish(consume(zb_ref, i, consume(za_ref, i - 1, state, False), True))


def _attention_specs(b, s):
    nb = s // TB
    q_spec = pl.BlockSpec((None, None, 2, None, LANES, TB), lambda bi, h, i: (bi, h, 0, i, 0, 0))
    k_spec = pl.BlockSpec((None, None, 2, s, LANES), lambda bi, h, i: (bi, h, 0, 0, 0))
    v_spec = pl.BlockSpec((None, None, nb, LANES, TB), lambda bi, h, i: (bi, h, 0, 0, 0))
    o_spec = pl.BlockSpec((None, TB, LANES), lambda bi, h, i: (bi, i, h))
    return q_spec, k_spec, v_spec, o_spec


def _diff_kernel(slope_ref, qt_ref, k_ref, vt_ref, lq1_ref, lk1_ref, lq2_ref, lk2_ref, sub_ref, o_ref,
                 acc_ref, corr_ref, za_ref, zb_ref, *, lam_init):
    hd = pl.program_id(1)
    i = pl.program_id(2)

    @pl.when(i == 0)
    def _():
        slope = slope_ref[hd] * LOG2E
        kk = lax.broadcasted_iota(jnp.int32, (TB, TB), 0)
        qq = lax.broadcasted_iota(jnp.int32, (TB, TB), 1)
        after = jnp.where(kk > qq, (2.0 * slope) * (qq - kk).astype(F32), 0.0)
        corr_ref[...] = jnp.where((kk // CHUNK) <= (qq // CHUNK), after, NEG)

    def finish(state):
        _, l0, _, l1 = state
        lam = (jnp.exp(jnp.sum(lq1_ref[...] * lk1_ref[...], axis=1, keepdims=True))
               - jnp.exp(jnp.sum(lq2_ref[...] * lk2_ref[...], axis=1, keepdims=True)) + lam_init)
        o = acc_ref[0] * (1.0 / l0) - lam * (acc_ref[1] * (1.0 / l1))
        ms = jnp.mean(o * o, axis=0, keepdims=True)
        o = o * lax.rsqrt(ms + EPS) * pltpu.repeat(sub_ref[...], TB // LANES, 1) * (1.0 - lam_init)
        o_ref[...] = o.T.astype(BF16)

    _attend(i, qt_ref, k_ref, lambda j, c: vt_ref[j], corr_ref, acc_ref, za_ref, zb_ref, finish)


def _diff_attention(qt, k, vt, slopes, lq1, lk1, lq2, lk2, sub, lam_init):
    b, nh, _, nb, _, _ = qt.shape
    s = nb * TB
    q_spec, k_spec, v_spec, o_spec = _attention_specs(b, s)
    vec_spec = _const_spec(lq1.shape)
    return pl.pallas_call(
        functools.partial(_diff_kernel, lam_init=lam_init),
        grid=(b, nh, nb),
        in_specs=[pl.BlockSpec(memory_space=pltpu.SMEM), q_spec, k_spec, v_spec,
                  vec_spec, vec_spec, vec_spec, vec_spec, _const_spec(sub.shape)],
        out_specs=o_spec,
        out_shape=jax.ShapeDtypeStruct((b, s, nh * LANES), BF16),
        scratch_shapes=[pltpu.VMEM((2, DA_V_DIM, TB), F32), pltpu.VMEM((TB, TB), F32),
                        pltpu.VMEM((2, TB, TB), F32), pltpu.VMEM((2, TB, TB), F32)],
        compiler_params=_params("parallel", "parallel", "arbitrary"),
        name="diff_attention",
    )(slopes, qt, k, vt, lq1, lk1, lq2, lk2, sub)


def _fox_kernel(qt_ref, k_ref, vt_ref, o_ref, acc_ref, corr_ref, za_ref, zb_ref):
    i = pl.program_id(2)

    @pl.when(i == 0)
    def _():
        kk = lax.broadcasted_iota(jnp.int32, (TB, TB), 0)
        qq = lax.broadcasted_iota(jnp.int32, (TB, TB), 1)
        corr_ref[...] = jnp.where(kk <= qq, 0.0, NEG)

    def finish(state):
        _, l0, _, l1 = state
        o = jnp.concatenate([acc_ref[0] * (1.0 / l0), acc_ref[1] * (1.0 / l1)], axis=0)
        o_ref[...] = o.T.astype(BF16)

    _attend(i, qt_ref, k_ref, lambda j, hh: vt_ref[j, hh * HALF:(hh + 1) * HALF, :], corr_ref, acc_ref,
            za_ref, zb_ref, finish)


def _fox_attention(qt, k, vt):
    b, npair, _, nb, _, _ = qt.shape
    s = nb * TB
    q_spec, k_spec, v_spec, o_spec = _attention_specs(b, s)
    return pl.pallas_call(
        _fox_kernel,
        grid=(b, npair, nb),
        in_specs=[q_spec, k_spec, v_spec],
        out_specs=o_spec,
        out_shape=jax.ShapeDtypeStruct((b, s, npair * LANES), BF16),
        scratch_shapes=[pltpu.VMEM((2, FX_DIM, TB), F32), pltpu.VMEM((TB, TB), F32),
                        pltpu.VMEM((2, TB, TB), F32), pltpu.VMEM((2, TB, TB), F32)],
        compiler_params=_params("parallel", "parallel", "arbitrary"),
        name="fox_attention",
    )(qt, k, vt)


def _merge_kernel(x_ref, nm_ref, oa_ref, of_ref, qm_ref, km_ref, vm_ref, wg_ref, bg_ref,
                  wda_ref, wfx_ref, wmem_ref, wout_ref, o_ref, om_ref, merged_ref):
    x = x_ref[...]
    h = _rms_rows(x, nm_ref[...]).astype(BF16)

    for hd in range(MEM_HEADS):
        sl = slice(hd * MEM_DIM, (hd + 1) * MEM_DIM)
        s = _dot_nt(qm_ref[:, sl], km_ref[:, sl])
        p = jnp.exp2(s - jnp.max(s, axis=1, keepdims=True))
        inv = 1.0 / jnp.sum(p, axis=1, keepdims=True)
        om_ref[:, sl] = (_dot(p.astype(BF16), vm_ref[:, sl]) * inv).astype(BF16)

    branches = ((oa_ref, wda_ref), (of_ref, wfx_ref), (om_ref, wmem_ref))
    for n in range(D_MODEL // MXU_DIM):
        sl = slice(n * MXU_DIM, (n + 1) * MXU_DIM)
        merged = None
        for br, (o_br, w_br) in enumerate(branches):
            gsl = slice(br * D_MODEL + n * MXU_DIM, br * D_MODEL + (n + 1) * MXU_DIM)
            gate = jax.nn.sigmoid(_dot(h, wg_ref[:, gsl]) + bg_ref[:, gsl])
            term = gate * _dot(o_br[...], w_br[:, sl])
            merged = term if merged is None else merged + term
        merged_ref[:, sl] = merged.astype(BF16)

    o_ref[...] = x + _dot(merged_ref[...], wout_ref[...])


def _merge(x, nm, oa, of, qm, km, vm, wg, bg, wda, wfx, wmem, wout):
    b, s, d = x.shape
    tm = TM_PROJ
    row = lambda width: pl.BlockSpec((None, tm, width), lambda bi, i: (bi, i, 0))
    mem_spec = pl.BlockSpec((None, N_MEM, MEM_WIDTH), lambda bi, i: (bi, 0, 0))
    return pl.pallas_call(
        _merge_kernel,
        grid=(b, s // tm),
        in_specs=[row(d), _const_spec(nm.shape), row(DA_WIDTH), row(FX_WIDTH), row(MEM_WIDTH), mem_spec, mem_spec,
                  _const_spec(wg.shape), _const_spec(bg.shape), _const_spec(wda.shape), _const_spec(wfx.shape),
                  _const_spec(wmem.shape), _const_spec(wout.shape)],
        out_specs=row(d),
        out_shape=jax.ShapeDtypeStruct((b, s, d), F32),
        scratch_shapes=[pltpu.VMEM((tm, MEM_WIDTH), BF16), pltpu.VMEM((tm, d), BF16)],
        compiler_params=_params("parallel", "parallel"),
        name="merge",
    )(x, nm, oa, of, qm, km, vm, wg, bg, wda, wfx, wmem, wout)


def _ffn_kernel(x_ref, nf_ref, wup_ref, cw_ref, cb_ref, wdn_ref, o_ref, carry_ref, act_ref):
    tm = x_ref.shape[0]

    @pl.when(pl.program_id(1) == 0)
    def _():
        carry_ref[...] = jnp.zeros_like(carry_ref)

    x = x_ref[...]
    h2 = _rms_rows(x, nf_ref[...]).astype(BF16)
    row = lax.broadcasted_iota(jnp.int32, (SUBLANES, FC), 0)

    def conv(col):
        sl = slice(col, col + FC)
        u = _dot(h2, wup_ref[:, sl])
        prev = carry_ref[:, sl]
        carry_ref[:, sl] = u[tm - SUBLANES:tm, :]
        out = u * cw_ref[CONV_W - 1:CONV_W, sl] + cb_ref[:, sl]
        for shift in range(1, CONV_W):
            us = pltpu.roll(u, shift, 0)
            head = jnp.where(row < shift, pltpu.roll(prev, shift, 0), us[0:SUBLANES, :])
            us = jnp.concatenate([head, us[SUBLANES:, :]], axis=0)
            out = out + us * cw_ref[CONV_W - 1 - shift:CONV_W - shift, sl]
        return out

    for ch in range(D_FF // FC):
        a = conv(ch * FC)
        g = conv(D_FF + ch * FC)
        act_ref[:, ch * FC:(ch + 1) * FC] = (a * jax.nn.sigmoid(a) * g).astype(BF16)

    o_ref[...] = x + _dot(act_ref[...], wdn_ref[...])


def _ffn(x, nf, wup, cw, cb, wdn):
    b, s, d = x.shape
    tm = TM_FFN
    row = pl.BlockSpec((None, tm, d), lambda bi, i: (bi, i, 0))
    return pl.pallas_call(
        _ffn_kernel,
        grid=(b, s // tm),
        in_specs=[row, _const_spec(nf.shape), _const_spec(wup.shape), _const_spec(cw.shape), _const_spec(cb.shape),
                  _const_spec(wdn.shape)],
        out_specs=row,
        out_shape=jax.ShapeDtypeStruct((b, s, d), F32),
        scratch_shapes=[pltpu.VMEM((SUBLANES, 2 * D_FF), F32), pltpu.VMEM((tm, D_FF), BF16)],
        compiler_params=_params("parallel", "arbitrary"),
        name="conv_mlp",
    )(x, nf, wup, cw, cb, wdn)


def _block_diag_ones(group):
    idx = np.arange(MXU_DIM) // group
    return jnp.asarray(idx[:, None] == idx[None, :], dtype=BF16)


def _gate_lane_layout():
    head = np.full((LANES,), -1, np.int32)
    part = np.full((LANES,), N_SPLIT, np.int32)
    for a in range(FX_HEADS // 2):
        for p in range(N_SPLIT):
            head[HALF + N_SPLIT * a + p] = 2 * a
            head[N_SPLIT * a + p] = 2 * a + 1
            part[HALF + N_SPLIT * a + p] = p
            part[N_SPLIT * a + p] = p
    return head, part


def kernel(x, mem, norm_mix, w_in, b_gate, da_q_norm, da_k_norm, da_lambda_q1, da_lambda_k1, da_lambda_q2,
           da_lambda_k2, da_subln, fx_q_norm, fx_k_norm, fx_f_bias, mem_norm, w_mem_kv, mem_q_norm, mem_k_norm,
           w_branch_da, w_branch_fx, w_branch_mem, w_out, norm_ffn, w_up, conv_w, conv_b, w_down):
    b, s, d = x.shape
    depth = w_in.shape[0]
    off = np.cumsum(np.array(IN_SIZES))[:-1].tolist()
    slope_vals = tuple(2.0 ** (-8.0 * (i + 1) / DA_HEADS) for i in range(DA_HEADS))
    slopes = jnp.asarray(slope_vals, dtype=F32)
    g64 = _block_diag_ones(DA_QK_DIM)
    g128 = _block_diag_ones(MEM_DIM)
    tri = jnp.asarray(np.tril(np.ones((TM_PROJ, TM_PROJ))), dtype=BF16)
    lane_head, lane_part = _gate_lane_layout()
    lane_used = jnp.asarray(lane_head >= 0)
    lane_src = jnp.asarray(np.maximum(lane_head, 0))
    part = jnp.asarray(lane_part).reshape(1, LANES)
    pos = jnp.asarray(np.broadcast_to(np.arange(TM_PROJ, dtype=np.float32)[:, None], (TM_PROJ, LANES)))
    pat_np = np.zeros((FX_HEADS // 2, HALF, TM_PROJ), np.float32)
    for a in range(FX_HEADS // 2):
        pat_np[a, N_SPLIT * a:N_SPLIT * (a + 1), :] = 1.0
    pat = jnp.asarray(pat_np, dtype=BF16)
    row = lambda v: v.reshape(1, -1).astype(F32)
    col = lambda v: jnp.broadcast_to(v.reshape(-1, 1).astype(F32), (v.size, LANES))

    for l in range(depth):
        lam_init = 0.8 - 0.6 * math.exp(-0.3 * l)
        a_q, a_k, a_v, f_q, f_k, f_v, f_g, m_q, w_g = jnp.split(w_in[l], off, axis=-1)
        wqt = jnp.concatenate([a_q, f_q], axis=1).T.astype(BF16)
        wvt = jnp.concatenate([a_v, f_v], axis=1).T.astype(BF16)
        wk = jnp.concatenate([a_k, f_k], axis=1).astype(BF16)
        wfg = jnp.where(lane_used[None, :], f_g[:, lane_src], 0.0).astype(BF16)
        fb = jnp.where(lane_used, fx_f_bias[l][lane_src], 0.0).reshape(1, LANES).astype(F32)
        gqt = col(jnp.concatenate([jnp.tile(da_q_norm[l], 2 * DA_HEADS) * (DA_QK_DIM ** -0.5 * LOG2E),
                                   jnp.tile(fx_q_norm[l], FX_HEADS) * (FX_DIM ** -0.5 * LOG2E)]))
        gk = row(jnp.concatenate([jnp.tile(da_k_norm[l], 2 * DA_HEADS), jnp.tile(fx_k_norm[l], FX_HEADS)]))
        gmq = row(jnp.tile(mem_q_norm[l], MEM_HEADS) * (MEM_DIM ** -0.5 * LOG2E))
        gmk = row(jnp.tile(mem_k_norm[l], MEM_HEADS))

        qat, qft, ka, kf, vat, vft, qm = _in_projection(
            x, row(norm_mix[l]), wqt, wk, wvt, m_q.astype(BF16), wfg, gqt, gk, gmq, fb, part, pos, pat, g64, g128, tri,
            slope_vals)
        km, vm = _memory_kv(mem, row(mem_norm[l]), w_mem_kv[l].astype(BF16), gmk, g128)

        oa = _diff_attention(qat.reshape(b, DA_HEADS, 2, s // TB, LANES, TB), ka.reshape(b, DA_HEADS, 2, s, LANES), vat, slopes,
                             row(da_lambda_q1[l]), row(da_lambda_k1[l]), row(da_lambda_q2[l]), row(da_lambda_k2[l]),
                             col(da_subln[l]), lam_init)
        of = _fox_attention(qft.reshape(b, FX_HEADS // 2, 2, s // TB, LANES, TB), kf.reshape(b, FX_HEADS // 2, 2, s, LANES), vft)

        x = _merge(x, row(norm_mix[l]), oa, of, qm, km, vm, w_g.astype(BF16), b_gate[l].reshape(1, -1).astype(F32),
                   w_branch_da[l].astype(BF16), w_branch_fx[l].astype(BF16), w_branch_mem[l].astype(BF16),
                   w_out[l].astype(BF16))
        x = _ffn(x, row(norm_ffn[l]), w_up[l].astype(BF16), conv_w[l].astype(F32), row(conv_b[l]),
                 w_down[l].astype(BF16))
    return x
```

```python
import functools
import math

import jax
import jax.numpy as jnp
import numpy as np
from jax import lax
from jax.experimental import pallas as pl
from jax.experimental.pallas import tpu as pltpu

D_MODEL = 1024
CHUNK = 64
N_MEM = 256
EPS = 1e-6

DA_HEADS = 4
DA_QK_DIM = 64
DA_V_DIM = 128
DA_WIDTH = 512
FX_HEADS = 8
FX_DIM = 64
FX_WIDTH = 512
MEM_HEADS = 4
MEM_DIM = 128
MEM_WIDTH = 512
N_BRANCH = 3
D_FF = 2816
CONV_W = 3

IN_SIZES = (512, 512, DA_WIDTH, FX_WIDTH, FX_WIDTH, FX_WIDTH, FX_HEADS, MEM_WIDTH, N_BRANCH * D_MODEL)

LANES = 128
SUBLANES = 8
MXU_DIM = 256
VMEM_LIMIT = 56 * 1024 * 1024

LOG2E = 1.4426950408889634
NEG = -1e30

TM_PROJ = 512
TM_FFN = 512
TB = 256
FC = 256
N_SPLIT = 3
HALF = LANES // 2
ONES_ROWS = 16

F32 = jnp.float32
BF16 = jnp.bfloat16

_NT = (((1,), (1,)), ((), ()))


def _dot(a, b):
    return jnp.dot(a, b, preferred_element_type=F32)


def _dot_nt(a, b):
    return lax.dot_general(a, b, _NT, preferred_element_type=F32)


def _rms_rows(x, g):
    ms = jnp.mean(x * x, axis=-1, keepdims=True)
    return x * lax.rsqrt(ms + EPS) * g


def _three_way(a):
    hi = a.astype(BF16)
    r = a - hi.astype(F32)
    mid = r.astype(BF16)
    lo = (r - mid.astype(F32)).astype(BF16)
    return hi, mid, lo


def _lane_tile(x, reps):
    return jnp.concatenate([x] * reps, axis=1) if reps > 1 else x


def _const_spec(shape):
    nd = len(shape)
    return pl.BlockSpec(shape, lambda *_: (0,) * nd, pipeline_mode=pl.Buffered(1))


def _params(*sem):
    return pltpu.CompilerParams(dimension_semantics=sem, vmem_limit_bytes=VMEM_LIMIT)


def _inproj_kernel(x_ref, nm_ref, wqt_ref, wk_ref, wvt_ref, wmq_ref, wfg_ref, gqt_ref, gk_ref, gmq_ref, fb_ref,
                   part_ref, pos_ref, pat_ref, g64_ref, g128_ref, tri_ref,
                   qat_ref, qft_ref, ka_ref, kf_ref, vat_ref, vft_ref, qm_ref, carry_ref, *, slopes):
    tm = x_ref.shape[0]
    nb = tm // TB
    h = _rms_rows(x_ref[...], nm_ref[...]).astype(BF16)
    lane = lax.broadcasted_iota(jnp.int32, (tm, LANES), 1)

    def store_blocks(out_ref, idx, yt):
        for jb in range(nb):
            out_ref[idx, jb] = yt[:, jb * TB:(jb + 1) * TB]

    for s in range(4):
        rows = slice(s * MXU_DIM, (s + 1) * MXU_DIM)
        yt = _dot_nt(wqt_ref[rows, :], h)
        y3 = yt.reshape(MXU_DIM // DA_QK_DIM, DA_QK_DIM, tm)
        inv = lax.rsqrt(jnp.mean(y3 * y3, axis=1, keepdims=True) + EPS)
        yn = ((y3 * inv).reshape(MXU_DIM, tm) * _lane_tile(gqt_ref[rows, :], tm // LANES)).astype(BF16)
        out_ref = (qat_ref, qft_ref)[s // 2]
        for t in range(2):
            hd = 2 * (s % 2) + t
            ones_rows = pat_ref[hd * (s // 2)]
            store_blocks(out_ref, 2 * hd, jnp.concatenate([yn[t * LANES:t * LANES + HALF], ones_rows], axis=0))
            store_blocks(out_ref, 2 * hd + 1, jnp.concatenate([ones_rows, yn[t * LANES + HALF:(t + 1) * LANES]], axis=0))

    ones = jnp.ones((ONES_ROWS, tm), BF16)
    for s in range(4):
        rows = slice(s * MXU_DIM, (s + 1) * MXU_DIM)
        vt = _dot_nt(wvt_ref[rows, :], h).astype(BF16)
        for t in range(2):
            hd = 2 * (s % 2) + t
            if s < 2:
                store_blocks(vat_ref, hd, jnp.concatenate([vt[t * LANES:(t + 1) * LANES], ones], axis=0))
            else:
                store_blocks(vft_ref, hd, jnp.concatenate(
                    [vt[t * LANES:t * LANES + HALF], ones, vt[t * LANES + HALF:(t + 1) * LANES], ones], axis=0))

    def normed(w_ref, gain_ref, group_ref, inv_dim, col):
        y = _dot(h, w_ref[:, col:col + MXU_DIM])
        ss = _dot((y * y).astype(BF16), group_ref[...])
        return (y * lax.rsqrt(ss * inv_dim + EPS) * gain_ref[:, col:col + MXU_DIM]).astype(BF16)

    for s in range(2):
        qm_ref[:, s * MXU_DIM:(s + 1) * MXU_DIM] = normed(wmq_ref, gmq_ref, g128_ref, 1.0 / MEM_DIM, s * MXU_DIM)

    def split_tile(a, base):
        hi, mid, lo = _three_way(a)
        zero = jnp.zeros_like(hi)
        return jnp.where(lane == base, hi, jnp.where(lane == base + 1, mid, jnp.where(lane == base + 2, lo, zero)))

    pos = pos_ref[...] + (pl.program_id(1) * tm).astype(F32)
    for s in range(2):
        yn = normed(wk_ref, gk_ref, g64_ref, 1.0 / DA_QK_DIM, s * MXU_DIM)
        for t in range(2):
            hd = 2 * s + t
            kt = yn[:, t * LANES:(t + 1) * LANES]
            a = pos * (slopes[hd] * LOG2E)
            ka_ref[2 * hd] = jnp.where(lane < HALF, kt, split_tile(a, HALF))
            ka_ref[2 * hd + 1] = jnp.where(lane >= HALF, kt, split_tile(a, 0))

    fg = _dot(h, wfg_ref[...]) + fb_ref[...]
    logf = jnp.minimum(fg, 0.0) - jnp.log(1.0 + jnp.exp(-jnp.abs(fg)))
    tri = tri_ref[...]
    local = sum(_dot(tri, piece) for piece in _three_way(logf))

    @pl.when(pl.program_id(1) == 0)
    def _():
        carry_ref[...] = jnp.zeros_like(carry_ref)

    c = local + carry_ref[0:1, :]
    carry_ref[...] = jnp.broadcast_to(c[tm - 1:tm, :], carry_ref.shape)
    hi, mid, lo = _three_way(c * (-LOG2E))
    part = jnp.broadcast_to(part_ref[...], (tm, LANES))
    zero = jnp.zeros_like(hi)
    feat = jnp.where(part == 0, hi, jnp.where(part == 1, mid, jnp.where(part == 2, lo, zero)))

    for s in range(2):
        yn = normed(wk_ref, gk_ref, g64_ref, 1.0 / FX_DIM, DA_WIDTH + s * MXU_DIM)
        for t in range(2):
            a = 2 * s + t
            kt = yn[:, t * LANES:(t + 1) * LANES]
            fa = jnp.where(lane < HALF + N_SPLIT * a, zero, jnp.where(lane < HALF + N_SPLIT * (a + 1), feat, zero))
            fb = jnp.where(lane < N_SPLIT * a, zero, jnp.where(lane < N_SPLIT * (a + 1), feat, zero))
            kf_ref[2 * a] = jnp.where(lane < HALF, kt, fa)
            kf_ref[2 * a + 1] = jnp.where(lane >= HALF, kt, fb)


def _in_projection(x, nm, wqt, wk, wvt, wmq, wfg, gqt, gk, gmq, fb, part, pos, pat, g64, g128, tri, slopes):
    b, s, d = x.shape
    tm = TM_PROJ

    def v_major(rows):
        return (jax.ShapeDtypeStruct((b, 4, s // TB, rows, TB), BF16),
                pl.BlockSpec((None, 4, tm // TB, rows, TB), lambda bi, i: (bi, 0, i, 0, 0)))

    va_major, va_spec = v_major(DA_V_DIM + ONES_ROWS)
    vf_major, vf_spec = v_major(2 * (FX_DIM + ONES_ROWS))
    q_major = jax.ShapeDtypeStruct((b, 8, s // TB, LANES, TB), BF16)
    q_spec = pl.BlockSpec((None, 8, tm // TB, LANES, TB), lambda bi, i: (bi, 0, i, 0, 0))
    k_major = jax.ShapeDtypeStruct((b, 8, s, LANES), BF16)
    k_spec = pl.BlockSpec((None, 8, tm, LANES), lambda bi, i: (bi, 0, i, 0))
    consts = (nm, wqt, wk, wvt, wmq, wfg, gqt, gk, gmq, fb, part, pos, pat, g64, g128, tri)
    return pl.pallas_call(
        functools.partial(_inproj_kernel, slopes=slopes),
        grid=(b, s // tm),
        in_specs=[pl.BlockSpec((None, tm, d), lambda bi, i: (bi, i, 0))] + [_const_spec(c.shape) for c in consts],
        out_specs=[q_spec, q_spec, k_spec, k_spec, va_spec, vf_spec,
                   pl.BlockSpec((None, tm, MEM_WIDTH), lambda bi, i: (bi, i, 0))],
        out_shape=[q_major, q_major, k_major, k_major, va_major, vf_major,
                   jax.ShapeDtypeStruct((b, s, MEM_WIDTH), BF16)],
        scratch_shapes=[pltpu.VMEM((SUBLANES, LANES), F32)],
        compiler_params=_params("parallel", "arbitrary"),
        name="in_projection",
    )(x, *consts)


def _memkv_kernel(mem_ref, nm_ref, w_ref, gk_ref, g128_ref, km_ref, vm_ref):
    mh = _rms_rows(mem_ref[...], nm_ref[...]).astype(BF16)
    for s in range(2):
        sl = slice(s * MXU_DIM, (s + 1) * MXU_DIM)
        y = _dot(mh, w_ref[:, sl])
        ss = _dot((y * y).astype(BF16), g128_ref[...])
        km_ref[:, sl] = (y * lax.rsqrt(ss * (1.0 / MEM_DIM) + EPS) * gk_ref[:, sl]).astype(BF16)
        vm_ref[:, sl] = _dot(mh, w_ref[:, MEM_WIDTH + s * MXU_DIM:MEM_WIDTH + (s + 1) * MXU_DIM]).astype(BF16)


def _memory_kv(mem, nm, w, gk, g128):
    b, n, d = mem.shape
    out = jax.ShapeDtypeStruct((b, n, MEM_WIDTH), BF16)
    spec = pl.BlockSpec((None, n, MEM_WIDTH), lambda bi: (bi, 0, 0))
    return pl.pallas_call(
        _memkv_kernel,
        grid=(b,),
        in_specs=[pl.BlockSpec((None, n, d), lambda bi: (bi, 0, 0)),
                  _const_spec(nm.shape), _const_spec(w.shape), _const_spec(gk.shape), _const_spec(g128.shape)],
        out_specs=[spec, spec],
        out_shape=[out, out],
        compiler_params=_params("parallel"),
        name="memory_kv",
    )(mem, nm, w, gk, g128)


def _attend(q_ref, k_ref, v_block, corr_ref, acc_ref, z_ref, finish):
    nb = q_ref.shape[1]
    stages = [(i, j) for i in range(nb) for j in range(i + 1)]

    def scores(slot, i, j):
        zmax = []
        for c in range(2):
            z = _dot(k_ref[c, j * TB:(j + 1) * TB, :], q_ref[c, i])
            if i == j:
                z = z + corr_ref[...]
            z_ref[slot, c] = z
            zmax.append(jnp.max(z, axis=0, keepdims=True))
        return zmax

    zmax = scores(0, *stages[0])
    m = [None, None]
    for t, (i, j) in enumerate(stages):
        nxt = scores((t + 1) % 2, *stages[t + 1]) if t + 1 < len(stages) else None
        for c in range(2):
            if j == 0:
                m[c] = zmax[c]
                acc_ref[c] = _dot(v_block(j, c), jnp.exp2(z_ref[t % 2, c] - m[c]).astype(BF16))
            else:
                m_new = jnp.maximum(m[c], zmax[c])
                alpha = jnp.exp2(m[c] - m_new)
                m[c] = m_new
                acc_ref[c] = alpha * acc_ref[c] + _dot(v_block(j, c), jnp.exp2(z_ref[t % 2, c] - m_new).astype(BF16))
        if j == i:
            finish(i)
        zmax = nxt


def _attention_specs(s, v_rows):
    nb = s // TB
    q_spec = pl.BlockSpec((None, None, 2, nb, LANES, TB), lambda bi, h: (bi, h, 0, 0, 0, 0))
    k_spec = pl.BlockSpec((None, None, 2, s, LANES), lambda bi, h: (bi, h, 0, 0, 0))
    v_spec = pl.BlockSpec((None, None, nb, v_rows, TB), lambda bi, h: (bi, h, 0, 0, 0))
    o_spec = pl.BlockSpec((None, s, LANES), lambda bi, h: (bi, 0, h))
    return q_spec, k_spec, v_spec, o_spec


def _diff_kernel(slope_ref, qt_ref, k_ref, vt_ref, lq1_ref, lk1_ref, lq2_ref, lk2_ref, sub_ref, o_ref,
                 acc_ref, corr_ref, z_ref, *, lam_init):
    slope = slope_ref[pl.program_id(1)] * LOG2E
    kk = lax.broadcasted_iota(jnp.int32, (TB, TB), 0)
    qq = lax.broadcasted_iota(jnp.int32, (TB, TB), 1)
    after = jnp.where(kk > qq, (2.0 * slope) * (qq - kk).astype(F32), 0.0)
    corr_ref[...] = jnp.where((kk // CHUNK) <= (qq // CHUNK), after, NEG)

    lam = (jnp.exp(jnp.sum(lq1_ref[...] * lk1_ref[...], axis=1, keepdims=True))
           - jnp.exp(jnp.sum(lq2_ref[...] * lk2_ref[...], axis=1, keepdims=True)) + lam_init)
    gain = _lane_tile(sub_ref[...], TB // LANES) * (1.0 - lam_init)

    def finish(i):
        o0 = acc_ref[0, :DA_V_DIM, :] * (1.0 / acc_ref[0, DA_V_DIM:DA_V_DIM + 1, :])
        o1 = acc_ref[1, :DA_V_DIM, :] * (1.0 / acc_ref[1, DA_V_DIM:DA_V_DIM + 1, :])
        o = o0 - lam * o1
        ms = jnp.mean(o * o, axis=0, keepdims=True)
        o_ref[i * TB:(i + 1) * TB, :] = (o * lax.rsqrt(ms + EPS) * gain).T.astype(BF16)

    _attend(qt_ref, k_ref, lambda j, c: vt_ref[j], corr_ref, acc_ref, z_ref, finish)


def _diff_attention(qt, k, vt, slopes, lq1, lk1, lq2, lk2, sub, lam_init):
    b, nh, _, nb, _, _ = qt.shape
    s = nb * TB
    rows = vt.shape[3]
    q_spec, k_spec, v_spec, o_spec = _attention_specs(s, rows)
    vec_spec = _const_spec(lq1.shape)
    return pl.pallas_call(
        functools.partial(_diff_kernel, lam_init=lam_init),
        grid=(b, nh),
        in_specs=[pl.BlockSpec(memory_space=pltpu.SMEM), q_spec, k_spec, v_spec,
                  vec_spec, vec_spec, vec_spec, vec_spec, _const_spec(sub.shape)],
        out_specs=o_spec,
        out_shape=jax.ShapeDtypeStruct((b, s, nh * LANES), BF16),
        scratch_shapes=[pltpu.VMEM((2, rows, TB), F32), pltpu.VMEM((TB, TB), F32), pltpu.VMEM((2, 2, TB, TB), F32)],
        compiler_params=_params("parallel", "parallel"),
        name="diff_attention",
    )(slopes, qt, k, vt, lq1, lk1, lq2, lk2, sub)


def _fox_kernel(qt_ref, k_ref, vt_ref, o_ref, acc_ref, corr_ref, z_ref):
    kk = lax.broadcasted_iota(jnp.int32, (TB, TB), 0)
    qq = lax.broadcasted_iota(jnp.int32, (TB, TB), 1)
    corr_ref[...] = jnp.where(kk <= qq, 0.0, NEG)
    rows = acc_ref.shape[1]

    def finish(i):
        heads = [acc_ref[hh, :FX_DIM, :] * (1.0 / acc_ref[hh, FX_DIM:FX_DIM + 1, :]) for hh in range(2)]
        o_ref[i * TB:(i + 1) * TB, :] = jnp.concatenate(heads, axis=0).T.astype(BF16)

    _attend(qt_ref, k_ref, lambda j, hh: vt_ref[j, hh * rows:(hh + 1) * rows, :], corr_ref, acc_ref, z_ref, finish)


def _fox_attention(qt, k, vt):
    b, npair, _, nb, _, _ = qt.shape
    s = nb * TB
    rows = vt.shape[3] // 2
    q_spec, k_spec, v_spec, o_spec = _attention_specs(s, 2 * rows)
    return pl.pallas_call(
        _fox_kernel,
        grid=(b, npair),
        in_specs=[q_spec, k_spec, v_spec],
        out_specs=o_spec,
        out_shape=jax.ShapeDtypeStruct((b, s, npair * LANES), BF16),
        scratch_shapes=[pltpu.VMEM((2, rows, TB), F32), pltpu.VMEM((TB, TB), F32), pltpu.VMEM((2, 2, TB, TB), F32)],
        compiler_params=_params("parallel", "parallel"),
        name="fox_attention",
    )(qt, k, vt)


def _merge_kernel(x_ref, nm_ref, oa_ref, of_ref, qm_ref, km_ref, vm_ref, wg_ref, bg_ref,
                  wda_ref, wfx_ref, wmem_ref, wout_ref, o_ref, om_ref, merged_ref):
    x = x_ref[...]
    h = _rms_rows(x, nm_ref[...]).astype(BF16)

    for hd in range(MEM_HEADS):
        sl = slice(hd * MEM_DIM, (hd + 1) * MEM_DIM)
        s = _dot_nt(qm_ref[:, sl], km_ref[:, sl])
        p = jnp.exp2(s - jnp.max(s, axis=1, keepdims=True))
        inv = 1.0 / jnp.sum(p, axis=1, keepdims=True)
        om_ref[:, sl] = (_dot(p.astype(BF16), vm_ref[:, sl]) * inv).astype(BF16)

    branches = ((oa_ref, wda_ref), (of_ref, wfx_ref), (om_ref, wmem_ref))
    for n in range(D_MODEL // MXU_DIM):
        sl = slice(n * MXU_DIM, (n + 1) * MXU_DIM)
        merged = None
        for br, (o_br, w_br) in enumerate(branches):
            gsl = slice(br * D_MODEL + n * MXU_DIM, br * D_MODEL + (n + 1) * MXU_DIM)
            gate = jax.nn.sigmoid(_dot(h, wg_ref[:, gsl]) + bg_ref[:, gsl])
            term = gate * _dot(o_br[...], w_br[:, sl])
            merged = term if merged is None else merged + term
        merged_ref[:, sl] = merged.astype(BF16)

    o_ref[...] = x + _dot(merged_ref[...], wout_ref[...])


def _merge(x, nm, oa, of, qm, km, vm, wg, bg, wda, wfx, wmem, wout):
    b, s, d = x.shape
    tm = TM_PROJ
    row = lambda width: pl.BlockSpec((None, tm, width), lambda bi, i: (bi, i, 0))
    mem_spec = pl.BlockSpec((None, N_MEM, MEM_WIDTH), lambda bi, i: (bi, 0, 0))
    return pl.pallas_call(
        _merge_kernel,
        grid=(b, s // tm),
        in_specs=[row(d), _const_spec(nm.shape), row(DA_WIDTH), row(FX_WIDTH), row(MEM_WIDTH), mem_spec, mem_spec,
                  _const_spec(wg.shape), _const_spec(bg.shape), _const_spec(wda.shape), _const_spec(wfx.shape),
                  _const_spec(wmem.shape), _const_spec(wout.shape)],
        out_specs=row(d),
        out_shape=jax.ShapeDtypeStruct((b, s, d), F32),
        scratch_shapes=[pltpu.VMEM((tm, MEM_WIDTH), BF16), pltpu.VMEM((tm, d), BF16)],
        compiler_params=_params("parallel", "parallel"),
        name="merge",
    )(x, nm, oa, of, qm, km, vm, wg, bg, wda, wfx, wmem, wout)


def _ffn_kernel(x_ref, nf_ref, wup_ref, cw_ref, cb_ref, wdn_ref, o_ref, carry_ref, act_ref):
    tm = x_ref.shape[0]

    @pl.when(pl.program_id(1) == 0)
    def _():
        carry_ref[...] = jnp.zeros_like(carry_ref)

    x = x_ref[...]
    h2 = _rms_rows(x, nf_ref[...]).astype(BF16)
    row = lax.broadcasted_iota(jnp.int32, (SUBLANES, FC), 0)

    def conv(col):
        sl = slice(col, col + FC)
        u = _dot(h2, wup_ref[:, sl])
        prev = carry_ref[:, sl]
        carry_ref[:, sl] = u[tm - SUBLANES:tm, :]
        out = u * cw_ref[CONV_W - 1:CONV_W, sl] + cb_ref[:, sl]
        for shift in range(1, CONV_W):
            us = pltpu.roll(u, shift, 0)
            head = jnp.where(row < shift, pltpu.roll(prev, shift, 0), us[0:SUBLANES, :])
            us = jnp.concatenate([head, us[SUBLANES:, :]], axis=0)
            out = out + us * cw_ref[CONV_W - 1 - shift:CONV_W - shift, sl]
        return out

    for ch in range(D_FF // FC):
        a = conv(ch * FC)
        g = conv(D_FF + ch * FC)
        act_ref[:, ch * FC:(ch + 1) * FC] = (a * jax.nn.sigmoid(a) * g).astype(BF16)

    o_ref[...] = x + _dot(act_ref[...], wdn_ref[...])


def _ffn(x, nf, wup, cw, cb, wdn):
    b, s, d = x.shape
    tm = TM_FFN
    row = pl.BlockSpec((None, tm, d), lambda bi, i: (bi, i, 0))
    return pl.pallas_call(
        _ffn_kernel,
        grid=(b, s // tm),
        in_specs=[row, _const_spec(nf.shape), _const_spec(wup.shape), _const_spec(cw.shape), _const_spec(cb.shape),
                  _const_spec(wdn.shape)],
        out_specs=row,
        out_shape=jax.ShapeDtypeStruct((b, s, d), F32),
        scratch_shapes=[pltpu.VMEM((SUBLANES, 2 * D_FF), F32), pltpu.VMEM((tm, D_FF), BF16)],
        compiler_params=_params("parallel", "arbitrary"),
        name="conv_mlp",
    )(x, nf, wup, cw, cb, wdn)


def _block_diag_ones(group):
    idx = np.arange(MXU_DIM) // group
    return jnp.asarray(idx[:, None] == idx[None, :], dtype=BF16)


def _gate_lane_layout():
    head = np.full((LANES,), -1, np.int32)
    part = np.full((LANES,), N_SPLIT, np.int32)
    for a in range(FX_HEADS // 2):
        for p in range(N_SPLIT):
            head[HALF + N_SPLIT * a + p] = 2 * a
            head[N_SPLIT * a + p] = 2 * a + 1
            part[HALF + N_SPLIT * a + p] = p
            part[N_SPLIT * a + p] = p
    return head, part


def kernel(x, mem, norm_mix, w_in, b_gate, da_q_norm, da_k_norm, da_lambda_q1, da_lambda_k1, da_lambda_q2,
           da_lambda_k2, da_subln, fx_q_norm, fx_k_norm, fx_f_bias, mem_norm, w_mem_kv, mem_q_norm, mem_k_norm,
           w_branch_da, w_branch_fx, w_branch_mem, w_out, norm_ffn, w_up, conv_w, conv_b, w_down):
    b, s, d = x.shape
    depth = w_in.shape[0]
    off = np.cumsum(np.array(IN_SIZES))[:-1].tolist()
    slope_vals = tuple(2.0 ** (-8.0 * (i + 1) / DA_HEADS) for i in range(DA_HEADS))
    slopes = jnp.asarray(slope_vals, dtype=F32)
    g64 = _block_diag_ones(DA_QK_DIM)
    g128 = _block_diag_ones(MEM_DIM)
    tri = jnp.asarray(np.tril(np.ones((TM_PROJ, TM_PROJ))), dtype=BF16)
    lane_head, lane_part = _gate_lane_layout()
    lane_used = jnp.asarray(lane_head >= 0)
    lane_src = jnp.asarray(np.maximum(lane_head, 0))
    part = jnp.asarray(lane_part).reshape(1, LANES)
    pos = jnp.asarray(np.broadcast_to(np.arange(TM_PROJ, dtype=np.float32)[:, None], (TM_PROJ, LANES)))
    pat_np = np.zeros((FX_HEADS // 2, HALF, TM_PROJ), np.float32)
    for a in range(FX_HEADS // 2):
        pat_np[a, N_SPLIT * a:N_SPLIT * (a + 1), :] = 1.0
    pat = jnp.asarray(pat_np, dtype=BF16)
    row = lambda v: v.reshape(1, -1).astype(F32)
    col = lambda v: jnp.broadcast_to(v.reshape(-1, 1).astype(F32), (v.size, LANES))

    for l in range(depth):
        lam_init = 0.8 - 0.6 * math.exp(-0.3 * l)
        a_q, a_k, a_v, f_q, f_k, f_v, f_g, m_q, w_g = jnp.split(w_in[l], off, axis=-1)
        wqt = jnp.concatenate([a_q, f_q], axis=1).T.astype(BF16)
        wvt = jnp.concatenate([a_v, f_v], axis=1).T.astype(BF16)
        wk = jnp.concatenate([a_k, f_k], axis=1).astype(BF16)
        wfg = jnp.where(lane_used[None, :], f_g[:, lane_src], 0.0).astype(BF16)
        fb = jnp.where(lane_used, fx_f_bias[l][lane_src], 0.0).reshape(1, LANES).astype(F32)
        gqt = col(jnp.concatenate([jnp.tile(da_q_norm[l], 2 * DA_HEADS) * (DA_QK_DIM ** -0.5 * LOG2E),
                                   jnp.tile(fx_q_norm[l], FX_HEADS) * (FX_DIM ** -0.5 * LOG2E)]))
        gk = row(jnp.concatenate([jnp.tile(da_k_norm[l], 2 * DA_HEADS), jnp.tile(fx_k_norm[l], FX_HEADS)]))
        gmq = row(jnp.tile(mem_q_norm[l], MEM_HEADS) * (MEM_DIM ** -0.5 * LOG2E))
        gmk = row(jnp.tile(mem_k_norm[l], MEM_HEADS))

        qat, qft, ka, kf, vat, vft, qm = _in_projection(
            x, row(norm_mix[l]), wqt, wk, wvt, m_q.astype(BF16), wfg, gqt, gk, gmq, fb, part, pos, pat, g64, g128, tri,
            slope_vals)
        km, vm = _memory_kv(mem, row(mem_norm[l]), w_mem_kv[l].astype(BF16), gmk, g128)

        oa = _diff_attention(qat.reshape(b, DA_HEADS, 2, s // TB, LANES, TB), ka.reshape(b, DA_HEADS, 2, s, LANES),
                             vat, slopes, row(da_lambda_q1[l]), row(da_lambda_k1[l]), row(da_lambda_q2[l]),
                             row(da_lambda_k2[l]), col(da_subln[l]), lam_init)
        of = _fox_attention(qft.reshape(b, FX_HEADS // 2, 2, s // TB, LANES, TB),
                            kf.reshape(b, FX_HEADS // 2, 2, s, LANES), vft)

        x = _merge(x, row(norm_mix[l]), oa, of, qm, km, vm, w_g.astype(BF16), b_gate[l].reshape(1, -1).astype(F32),
                   w_branch_da[l].astype(BF16), w_branch_fx[l].astype(BF16), w_branch_mem[l].astype(BF16),
                   w_out[l].astype(BF16))
        x = _ffn(x, row(norm_ffn[l]), w_up[l].astype(BF16), conv_w[l].astype(F32), row(conv_b[l]),
                 w_down[l].astype(BF16))
    return x
```

```python
import functools
import math

import jax
import jax.numpy as jnp
import numpy as np
from jax import lax
from jax.experimental import pallas as pl
from jax.experimental.pallas import tpu as pltpu

D_MODEL = 1024
CHUNK = 64
N_MEM = 256
EPS = 1e-6

DA_HEADS = 4
DA_QK_DIM = 64
DA_V_DIM = 128
DA_WIDTH = 512
FX_HEADS = 8
FX_DIM = 64
FX_WIDTH = 512
MEM_HEADS = 4
MEM_DIM = 128
MEM_WIDTH = 512
N_BRANCH = 3
D_FF = 2816
CONV_W = 3

IN_SIZES = (512, 512, DA_WIDTH, FX_WIDTH, FX_WIDTH, FX_WIDTH, FX_HEADS, MEM_WIDTH, N_BRANCH * D_MODEL)

LANES = 128
SUBLANES = 8
MXU_DIM = 256
VMEM_LIMIT = 56 * 1024 * 1024

LOG2E = 1.4426950408889634
NEG = -1e30

TM_PROJ = 512
TM_FFN = 512
TB = 256
FC = 256
N_SPLIT = 3
HALF = LANES // 2
ONES_ROWS = 16

F32 = jnp.float32
BF16 = jnp.bfloat16

_NT = (((1,), (1,)), ((), ()))


def _dot(a, b):
    return jnp.dot(a, b, preferred_element_type=F32)


def _dot_nt(a, b):
    return lax.dot_general(a, b, _NT, preferred_element_type=F32)


def _rms_rows(x, g):
    ms = jnp.mean(x * x, axis=-1, keepdims=True)
    return x * lax.rsqrt(ms + EPS) * g


def _three_way(a):
    hi = a.astype(BF16)
    r = a - hi.astype(F32)
    mid = r.astype(BF16)
    lo = (r - mid.astype(F32)).astype(BF16)
    return hi, mid, lo


def _lane_tile(x, reps):
    return jnp.concatenate([x] * reps, axis=1) if reps > 1 else x


def _const_spec(shape):
    nd = len(shape)
    return pl.BlockSpec(shape, lambda *_: (0,) * nd, pipeline_mode=pl.Buffered(1))


def _params(*sem):
    return pltpu.CompilerParams(dimension_semantics=sem, vmem_limit_bytes=VMEM_LIMIT)


def _inproj_kernel(x_ref, nm_ref, wqt_ref, wvt_ref, wtok_ref, gqt_ref, gtok_ref, fb_ref,
                   part_ref, pat_ref, g64_ref, g128_ref, tri_ref, kpos_ref,
                   qat_ref, qft_ref, ka_ref, kf_ref, vat_ref, vft_ref, qm_ref, carry_ref):
    tm = x_ref.shape[0]
    nb = tm // TB
    h = _rms_rows(x_ref[...], nm_ref[...]).astype(BF16)
    lane = lax.broadcasted_iota(jnp.int32, (tm, LANES), 1)

    def store_blocks(out_ref, idx, yt):
        for jb in range(nb):
            out_ref[idx, jb] = yt[:, jb * TB:(jb + 1) * TB]

    def q_slab(s):
        rows = slice(s * MXU_DIM, (s + 1) * MXU_DIM)
        yt = _dot_nt(wqt_ref[rows, :], h)
        y3 = yt.reshape(MXU_DIM // DA_QK_DIM, DA_QK_DIM, tm)
        inv = lax.rsqrt(jnp.mean(y3 * y3, axis=1, keepdims=True) + EPS)
        yn = ((y3 * inv).reshape(MXU_DIM, tm) * _lane_tile(gqt_ref[rows, :], tm // LANES)).astype(BF16)
        out_ref = (qat_ref, qft_ref)[s // 2]
        for t in range(2):
            hd = 2 * (s % 2) + t
            ones_rows = pat_ref[hd * (s // 2)]
            store_blocks(out_ref, 2 * hd, jnp.concatenate([yn[t * LANES:t * LANES + HALF], ones_rows], axis=0))
            store_blocks(out_ref, 2 * hd + 1, jnp.concatenate([ones_rows, yn[t * LANES + HALF:(t + 1) * LANES]], axis=0))

    ones = jnp.ones((ONES_ROWS, tm), BF16)

    def v_slab(s):
        rows = slice(s * MXU_DIM, (s + 1) * MXU_DIM)
        vt = _dot_nt(wvt_ref[rows, :], h).astype(BF16)
        for t in range(2):
            hd = 2 * (s % 2) + t
            if s < 2:
                store_blocks(vat_ref, hd, jnp.concatenate([vt[t * LANES:(t + 1) * LANES], ones], axis=0))
            else:
                store_blocks(vft_ref, hd, jnp.concatenate(
                    [vt[t * LANES:t * LANES + HALF], ones, vt[t * LANES + HALF:(t + 1) * LANES], ones], axis=0))


    y = _dot(h, wtok_ref[...])

    def normed_pair(col, group_ref, inv_dim):
        cols = [slice(col + s * MXU_DIM, col + (s + 1) * MXU_DIM) for s in range(2)]
        ss = [_dot((y[:, sl] * y[:, sl]).astype(BF16), group_ref[...]) for sl in cols]
        return [(y[:, sl] * lax.rsqrt(q * inv_dim + EPS) * gtok_ref[:, sl]).astype(BF16) for sl, q in zip(cols, ss)]

    fg = y[:, 2 * DA_WIDTH + MEM_WIDTH:] + fb_ref[...]
    logf = jnp.minimum(fg, 0.0) - jnp.log(1.0 + jnp.exp(-jnp.abs(fg)))
    tri = tri_ref[...]
    local = sum(_dot(tri, piece) for piece in _three_way(logf))

    q_slab(0)
    for s, yn in enumerate(normed_pair(2 * DA_WIDTH, g128_ref, 1.0 / MEM_DIM)):
        qm_ref[:, s * MXU_DIM:(s + 1) * MXU_DIM] = yn
    v_slab(0)

    for s, yn in enumerate(normed_pair(0, g64_ref, 1.0 / DA_QK_DIM)):
        for t in range(2):
            hd = 2 * s + t
            kt = yn[:, t * LANES:(t + 1) * LANES]
            ka_ref[2 * hd] = jnp.where(lane < HALF, kt, kpos_ref[hd, 0])
            ka_ref[2 * hd + 1] = jnp.where(lane >= HALF, kt, kpos_ref[hd, 1])
        (q_slab, v_slab)[s](1)

    @pl.when(pl.program_id(1) == 0)
    def _():
        carry_ref[...] = jnp.zeros_like(carry_ref)

    c = local + carry_ref[0:1, :]
    carry_ref[...] = jnp.broadcast_to(c[tm - 1:tm, :], carry_ref.shape)
    hi, mid, lo = _three_way(c * (-LOG2E))
    part = jnp.broadcast_to(part_ref[...], (tm, LANES))
    zero = jnp.zeros_like(hi)
    feat = jnp.where(part == 0, hi, jnp.where(part == 1, mid, jnp.where(part == 2, lo, zero)))
    q_slab(2)

    for s, yn in enumerate(normed_pair(DA_WIDTH, g64_ref, 1.0 / FX_DIM)):
        for t in range(2):
            a = 2 * s + t
            kt = yn[:, t * LANES:(t + 1) * LANES]
            fa = jnp.where(lane < HALF + N_SPLIT * a, zero, jnp.where(lane < HALF + N_SPLIT * (a + 1), feat, zero))
            fb = jnp.where(lane < N_SPLIT * a, zero, jnp.where(lane < N_SPLIT * (a + 1), feat, zero))
            kf_ref[2 * a] = jnp.where(lane < HALF, kt, fa)
            kf_ref[2 * a + 1] = jnp.where(lane >= HALF, kt, fb)
        (v_slab, q_slab)[s](2 + s)
    v_slab(3)


def _in_projection(x, nm, wqt, wvt, wtok, gqt, gtok, fb, part, pat, g64, g128, tri, kpos):
    b, s, d = x.shape
    tm = TM_PROJ

    def v_major(rows):
        return (jax.ShapeDtypeStruct((b, 4, s // TB, rows, TB), BF16),
                pl.BlockSpec((None, 4, tm // TB, rows, TB), lambda bi, i: (bi, 0, i, 0, 0)))

    va_major, va_spec = v_major(DA_V_DIM + ONES_ROWS)
    vf_major, vf_spec = v_major(2 * (FX_DIM + ONES_ROWS))
    q_major = jax.ShapeDtypeStruct((b, 8, s // TB, LANES, TB), BF16)
    q_spec = pl.BlockSpec((None, 8, tm // TB, LANES, TB), lambda bi, i: (bi, 0, i, 0, 0))
    k_major = jax.ShapeDtypeStruct((b, 8, s, LANES), BF16)
    k_spec = pl.BlockSpec((None, 8, tm, LANES), lambda bi, i: (bi, 0, i, 0))
    consts = (nm, wqt, wvt, wtok, gqt, gtok, fb, part, pat, g64, g128, tri)
    return pl.pallas_call(
        _inproj_kernel,
        grid=(b, s // tm),
        in_specs=[pl.BlockSpec((None, tm, d), lambda bi, i: (bi, i, 0))] + [_const_spec(c.shape) for c in consts]
        + [pl.BlockSpec((DA_HEADS, 2, tm, LANES), lambda bi, i: (0, 0, i, 0))],
        out_specs=[q_spec, q_spec, k_spec, k_spec, va_spec, vf_spec,
                   pl.BlockSpec((None, tm, MEM_WIDTH), lambda bi, i: (bi, i, 0))],
        out_shape=[q_major, q_major, k_major, k_major, va_major, vf_major,
                   jax.ShapeDtypeStruct((b, s, MEM_WIDTH), BF16)],
        scratch_shapes=[pltpu.VMEM((SUBLANES, LANES), F32)],
        compiler_params=_params("parallel", "arbitrary"),
        name="in_projection",
    )(x, *consts, kpos)


def _memkv_kernel(mem_ref, nm_ref, w_ref, gk_ref, g128_ref, km_ref, vm_ref):
    mh = _rms_rows(mem_ref[...], nm_ref[...]).astype(BF16)
    for s in range(2):
        sl = slice(s * MXU_DIM, (s + 1) * MXU_DIM)
        y = _dot(mh, w_ref[:, sl])
        ss = _dot((y * y).astype(BF16), g128_ref[...])
        km_ref[:, sl] = (y * lax.rsqrt(ss * (1.0 / MEM_DIM) + EPS) * gk_ref[:, sl]).astype(BF16)
        vm_ref[:, sl] = _dot(mh, w_ref[:, MEM_WIDTH + s * MXU_DIM:MEM_WIDTH + (s + 1) * MXU_DIM]).astype(BF16)


def _memory_kv(mem, nm, w, gk, g128):
    b, n, d = mem.shape
    out = jax.ShapeDtypeStruct((b, n, MEM_WIDTH), BF16)
    spec = pl.BlockSpec((None, n, MEM_WIDTH), lambda bi: (bi, 0, 0))
    return pl.pallas_call(
        _memkv_kernel,
        grid=(b,),
        in_specs=[pl.BlockSpec((None, n, d), lambda bi: (bi, 0, 0)),
                  _const_spec(nm.shape), _const_spec(w.shape), _const_spec(gk.shape), _const_spec(g128.shape)],
        out_specs=[spec, spec],
        out_shape=[out, out],
        compiler_params=_params("parallel"),
        name="memory_kv",
    )(mem, nm, w, gk, g128)


def _attend(q_ref, k_ref, v_block, corr_ref, acc_ref, z_ref, finish):
    nb = q_ref.shape[1]
    stages = [(i, j) for i in range(nb) for j in range(i + 1)]

    def scores(slot, i, j):
        zmax = []
        for c in range(2):
            z = _dot(k_ref[c, j * TB:(j + 1) * TB, :], q_ref[c, i])
            if i == j:
                z = z + corr_ref[...]
            z_ref[slot, c] = z
            zmax.append(jnp.max(z, axis=0, keepdims=True))
        return zmax

    zmax = scores(0, *stages[0])
    m = [None, None]
    for t, (i, j) in enumerate(stages):
        nxt = scores((t + 1) % 2, *stages[t + 1]) if t + 1 < len(stages) else None
        for c in range(2):
            if j == 0:
                m[c] = zmax[c]
                acc_ref[c] = _dot(v_block(j, c), jnp.exp2(z_ref[t % 2, c] - m[c]).astype(BF16))
            else:
                m_new = jnp.maximum(m[c], zmax[c])
                alpha = jnp.exp2(m[c] - m_new)
                m[c] = m_new
                acc_ref[c] = alpha * acc_ref[c] + _dot(v_block(j, c), jnp.exp2(z_ref[t % 2, c] - m_new).astype(BF16))
        if j == i:
            finish(i)
        zmax = nxt


def _attention_specs(s, v_rows):
    nb = s // TB
    q_spec = pl.BlockSpec((None, None, 2, nb, LANES, TB), lambda bi, h: (bi, h, 0, 0, 0, 0))
    k_spec = pl.BlockSpec((None, None, 2, s, LANES), lambda bi, h: (bi, h, 0, 0, 0))
    v_spec = pl.BlockSpec((None, None, nb, v_rows, TB), lambda bi, h: (bi, h, 0, 0, 0))
    o_spec = pl.BlockSpec((None, s, LANES), lambda bi, h: (bi, 0, h))
    return q_spec, k_spec, v_spec, o_spec


def _diff_kernel(slope_ref, qt_ref, k_ref, vt_ref, lq1_ref, lk1_ref, lq2_ref, lk2_ref, sub_ref, o_ref,
                 acc_ref, corr_ref, z_ref, *, lam_init):
    slope = slope_ref[pl.program_id(1)] * LOG2E
    kk = lax.broadcasted_iota(jnp.int32, (TB, TB), 0)
    qq = lax.broadcasted_iota(jnp.int32, (TB, TB), 1)
    after = jnp.where(kk > qq, (2.0 * slope) * (qq - kk).astype(F32), 0.0)
    corr_ref[...] = jnp.where((kk // CHUNK) <= (qq // CHUNK), after, NEG)

    lam = (jnp.exp(jnp.sum(lq1_ref[...] * lk1_ref[...], axis=1, keepdims=True))
           - jnp.exp(jnp.sum(lq2_ref[...] * lk2_ref[...], axis=1, keepdims=True)) + lam_init)
    gain = _lane_tile(sub_ref[...], TB // LANES) * (1.0 - lam_init)

    def finish(i):
        o0 = acc_ref[0, :DA_V_DIM, :] * (1.0 / acc_ref[0, DA_V_DIM:DA_V_DIM + 1, :])
        o1 = acc_ref[1, :DA_V_DIM, :] * (1.0 / acc_ref[1, DA_V_DIM:DA_V_DIM + 1, :])
        o = o0 - lam * o1
        ms = jnp.mean(o * o, axis=0, keepdims=True)
        o_ref[i * TB:(i + 1) * TB, :] = (o * lax.rsqrt(ms + EPS) * gain).T.astype(BF16)

    _attend(qt_ref, k_ref, lambda j, c: vt_ref[j], corr_ref, acc_ref, z_ref, finish)


def _diff_attention(qt, k, vt, slopes, lq1, lk1, lq2, lk2, sub, lam_init):
    b, nh, _, nb, _, _ = qt.shape
    s = nb * TB
    rows = vt.shape[3]
    q_spec, k_spec, v_spec, o_spec = _attention_specs(s, rows)
    vec_spec = _const_spec(lq1.shape)
    return pl.pallas_call(
        functools.partial(_diff_kernel, lam_init=lam_init),
        grid=(b, nh),
        in_specs=[pl.BlockSpec(memory_space=pltpu.SMEM), q_spec, k_spec, v_spec,
                  vec_spec, vec_spec, vec_spec, vec_spec, _const_spec(sub.shape)],
        out_specs=o_spec,
        out_shape=jax.ShapeDtypeStruct((b, s, nh * LANES), BF16),
        scratch_shapes=[pltpu.VMEM((2, rows, TB), F32), pltpu.VMEM((TB, TB), F32), pltpu.VMEM((2, 2, TB, TB), F32)],
        compiler_params=_params("parallel", "parallel"),
        name="diff_attention",
    )(slopes, qt, k, vt, lq1, lk1, lq2, lk2, sub)


def _fox_kernel(qt_ref, k_ref, vt_ref, o_ref, acc_ref, corr_ref, z_ref):
    kk = lax.broadcasted_iota(jnp.int32, (TB, TB), 0)
    qq = lax.broadcasted_iota(jnp.int32, (TB, TB), 1)
    corr_ref[...] = jnp.where(kk <= qq, 0.0, NEG)
    rows = acc_ref.shape[1]

    def finish(i):
        heads = [acc_ref[hh, :FX_DIM, :] * (1.0 / acc_ref[hh, FX_DIM:FX_DIM + 1, :]) for hh in range(2)]
        o_ref[i * TB:(i + 1) * TB, :] = jnp.concatenate(heads, axis=0).T.astype(BF16)

    _attend(qt_ref, k_ref, lambda j, hh: vt_ref[j, hh * rows:(hh + 1) * rows, :], corr_ref, acc_ref, z_ref, finish)


def _fox_attention(qt, k, vt):
    b, npair, _, nb, _, _ = qt.shape
    s = nb * TB
    rows = vt.shape[3] // 2
    q_spec, k_spec, v_spec, o_spec = _attention_specs(s, 2 * rows)
    return pl.pallas_call(
        _fox_kernel,
        grid=(b, npair),
        in_specs=[q_spec, k_spec, v_spec],
        out_specs=o_spec,
        out_shape=jax.ShapeDtypeStruct((b, s, npair * LANES), BF16),
        scratch_shapes=[pltpu.VMEM((2, rows, TB), F32), pltpu.VMEM((TB, TB), F32), pltpu.VMEM((2, 2, TB, TB), F32)],
        compiler_params=_params("parallel", "parallel"),
        name="fox_attention",
    )(qt, k, vt)


def _merge_kernel(x_ref, nm_ref, oa_ref, of_ref, qm_ref, km_ref, vm_ref, wg_ref, bg_ref,
                  wda_ref, wfx_ref, wmem_ref, wout_ref, o_ref, om_ref, merged_ref):
    x = x_ref[...]
    h = _rms_rows(x, nm_ref[...]).astype(BF16)

    for hd in range(MEM_HEADS):
        sl = slice(hd * MEM_DIM, (hd + 1) * MEM_DIM)
        s = _dot_nt(qm_ref[:, sl], km_ref[:, sl])
        p = jnp.exp2(s - jnp.max(s, axis=1, keepdims=True))
        inv = 1.0 / jnp.sum(p, axis=1, keepdims=True)
        om_ref[:, sl] = (_dot(p.astype(BF16), vm_ref[:, sl]) * inv).astype(BF16)

    branches = ((oa_ref, wda_ref), (of_ref, wfx_ref), (om_ref, wmem_ref))
    for n in range(D_MODEL // MXU_DIM):
        sl = slice(n * MXU_DIM, (n + 1) * MXU_DIM)
        merged = None
        for br, (o_br, w_br) in enumerate(branches):
            gsl = slice(br * D_MODEL + n * MXU_DIM, br * D_MODEL + (n + 1) * MXU_DIM)
            gate = jax.nn.sigmoid(_dot(h, wg_ref[:, gsl]) + bg_ref[:, gsl])
            term = gate * _dot(o_br[...], w_br[:, sl])
            merged = term if merged is None else merged + term
        merged_ref[:, sl] = merged.astype(BF16)

    o_ref[...] = x + _dot(merged_ref[...], wout_ref[...])


def _merge(x, nm, oa, of, qm, km, vm, wg, bg, wda, wfx, wmem, wout):
    b, s, d = x.shape
    tm = TM_PROJ
    row = lambda width: pl.BlockSpec((None, tm, width), lambda bi, i: (bi, i, 0))
    mem_spec = pl.BlockSpec((None, N_MEM, MEM_WIDTH), lambda bi, i: (bi, 0, 0))
    return pl.pallas_call(
        _merge_kernel,
        grid=(b, s // tm),
        in_specs=[row(d), _const_spec(nm.shape), row(DA_WIDTH), row(FX_WIDTH), row(MEM_WIDTH), mem_spec, mem_spec,
                  _const_spec(wg.shape), _const_spec(bg.shape), _const_spec(wda.shape), _const_spec(wfx.shape),
                  _const_spec(wmem.shape), _const_spec(wout.shape)],
        out_specs=row(d),
        out_shape=jax.ShapeDtypeStruct((b, s, d), F32),
        scratch_shapes=[pltpu.VMEM((tm, MEM_WIDTH), BF16), pltpu.VMEM((tm, d), BF16)],
        compiler_params=_params("parallel", "parallel"),
        name="merge",
    )(x, nm, oa, of, qm, km, vm, wg, bg, wda, wfx, wmem, wout)


def _ffn_kernel(x_ref, nf_ref, wup_ref, cw_ref, cb_ref, wdn_ref, o_ref, carry_ref, act_ref):
    tm = x_ref.shape[0]

    @pl.when(pl.program_id(1) == 0)
    def _():
        carry_ref[...] = jnp.zeros_like(carry_ref)

    x = x_ref[...]
    h2 = _rms_rows(x, nf_ref[...]).astype(BF16)
    row = lax.broadcasted_iota(jnp.int32, (SUBLANES, FC), 0)

    def conv(col):
        sl = slice(col, col + FC)
        u = _dot(h2, wup_ref[:, sl])
        prev = carry_ref[:, sl]
        carry_ref[:, sl] = u[tm - SUBLANES:tm, :]
        out = u * cw_ref[CONV_W - 1:CONV_W, sl] + cb_ref[:, sl]
        for shift in range(1, CONV_W):
            us = pltpu.roll(u, shift, 0)
            head = jnp.where(row < shift, pltpu.roll(prev, shift, 0), us[0:SUBLANES, :])
            us = jnp.concatenate([head, us[SUBLANES:, :]], axis=0)
            out = out + us * cw_ref[CONV_W - 1 - shift:CONV_W - shift, sl]
        return out

    for ch in range(D_FF // FC):
        a = conv(ch * FC)
        g = conv(D_FF + ch * FC)
        act_ref[:, ch * FC:(ch + 1) * FC] = (a * jax.nn.sigmoid(a) * g).astype(BF16)

    o_ref[...] = x + _dot(act_ref[...], wdn_ref[...])


def _ffn(x, nf, wup, cw, cb, wdn):
    b, s, d = x.shape
    tm = TM_FFN
    row = pl.BlockSpec((None, tm, d), lambda bi, i: (bi, i, 0))
    return pl.pallas_call(
        _ffn_kernel,
        grid=(b, s // tm),
        in_specs=[row, _const_spec(nf.shape), _const_spec(wup.shape), _const_spec(cw.shape), _const_spec(cb.shape),
                  _const_spec(wdn.shape)],
        out_specs=row,
        out_shape=jax.ShapeDtypeStruct((b, s, d), F32),
        scratch_shapes=[pltpu.VMEM((SUBLANES, 2 * D_FF), F32), pltpu.VMEM((tm, D_FF), BF16)],
        compiler_params=_params("parallel", "arbitrary"),
        name="conv_mlp",
    )(x, nf, wup, cw, cb, wdn)


def _block_diag_ones(group):
    idx = np.arange(MXU_DIM) // group
    return jnp.asarray(idx[:, None] == idx[None, :], dtype=BF16)


def _alibi_key_table(s):
    slopes = np.asarray([2.0 ** (-8.0 * (i + 1) / DA_HEADS) * LOG2E for i in range(DA_HEADS)], np.float32)
    rest = np.arange(s, dtype=np.float32)[None, :] * slopes[:, None]
    table = np.zeros((DA_HEADS, 2, s, LANES), BF16)
    for p in range(N_SPLIT):
        piece = rest.astype(BF16)
        rest = rest - piece.astype(np.float32)
        table[:, 0, :, HALF + p] = piece
        table[:, 1, :, p] = piece
    return table


def _gate_lane_layout():
    head = np.full((LANES,), -1, np.int32)
    part = np.full((LANES,), N_SPLIT, np.int32)
    for a in range(FX_HEADS // 2):
        for p in range(N_SPLIT):
            head[HALF + N_SPLIT * a + p] = 2 * a
            head[N_SPLIT * a + p] = 2 * a + 1
            part[HALF + N_SPLIT * a + p] = p
            part[N_SPLIT * a + p] = p
    return head, part


def kernel(x, mem, norm_mix, w_in, b_gate, da_q_norm, da_k_norm, da_lambda_q1, da_lambda_k1, da_lambda_q2,
           da_lambda_k2, da_subln, fx_q_norm, fx_k_norm, fx_f_bias, mem_norm, w_mem_kv, mem_q_norm, mem_k_norm,
           w_branch_da, w_branch_fx, w_branch_mem, w_out, norm_ffn, w_up, conv_w, conv_b, w_down):
    b, s, d = x.shape
    depth = w_in.shape[0]
    off = np.cumsum(np.array(IN_SIZES))[:-1].tolist()
    slopes = jnp.asarray([2.0 ** (-8.0 * (i + 1) / DA_HEADS) for i in range(DA_HEADS)], dtype=F32)
    g64 = _block_diag_ones(DA_QK_DIM)
    g128 = _block_diag_ones(MEM_DIM)
    tri = jnp.asarray(np.tril(np.ones((TM_PROJ, TM_PROJ))), dtype=BF16)
    lane_head, lane_part = _gate_lane_layout()
    lane_used = jnp.asarray(lane_head >= 0)
    lane_src = jnp.asarray(np.maximum(lane_head, 0))
    part = jnp.asarray(lane_part).reshape(1, LANES)
    kpos = jnp.asarray(_alibi_key_table(s))
    pat_np = np.zeros((FX_HEADS // 2, HALF, TM_PROJ), np.float32)
    for a in range(FX_HEADS // 2):
        pat_np[a, N_SPLIT * a:N_SPLIT * (a + 1), :] = 1.0
    pat = jnp.asarray(pat_np, dtype=BF16)
    row = lambda v: v.reshape(1, -1).astype(F32)
    col = lambda v: jnp.broadcast_to(v.reshape(-1, 1).astype(F32), (v.size, LANES))

    for l in range(depth):
        lam_init = 0.8 - 0.6 * math.exp(-0.3 * l)
        a_q, a_k, a_v, f_q, f_k, f_v, f_g, m_q, w_g = jnp.split(w_in[l], off, axis=-1)
        wqt = jnp.concatenate([a_q, f_q], axis=1).T.astype(BF16)
        wvt = jnp.concatenate([a_v, f_v], axis=1).T.astype(BF16)
        wfg = jnp.where(lane_used[None, :], f_g[:, lane_src], 0.0)
        wtok = jnp.concatenate([a_k, f_k, m_q, wfg], axis=1).astype(BF16)
        fb = jnp.where(lane_used, fx_f_bias[l][lane_src], 0.0).reshape(1, LANES).astype(F32)
        gqt = col(jnp.concatenate([jnp.tile(da_q_norm[l], 2 * DA_HEADS) * (DA_QK_DIM ** -0.5 * LOG2E),
                                   jnp.tile(fx_q_norm[l], FX_HEADS) * (FX_DIM ** -0.5 * LOG2E)]))
        gtok = row(jnp.concatenate([jnp.tile(da_k_norm[l], 2 * DA_HEADS), jnp.tile(fx_k_norm[l], FX_HEADS),
                                    jnp.tile(mem_q_norm[l], MEM_HEADS) * (MEM_DIM ** -0.5 * LOG2E)]))
        gmk = row(jnp.tile(mem_k_norm[l], MEM_HEADS))

        qat, qft, ka, kf, vat, vft, qm = _in_projection(
            x, row(norm_mix[l]), wqt, wvt, wtok, gqt, gtok, fb, part, pat, g64, g128, tri, kpos)
        km, vm = _memory_kv(mem, row(mem_norm[l]), w_mem_kv[l].astype(BF16), gmk, g128)

        oa = _diff_attention(qat.reshape(b, DA_HEADS, 2, s // TB, LANES, TB), ka.reshape(b, DA_HEADS, 2, s, LANES),
                             vat, slopes, row(da_lambda_q1[l]), row(da_lambda_k1[l]), row(da_lambda_q2[l]),
                             row(da_lambda_k2[l]), col(da_subln[l]), lam_init)
        of = _fox_attention(qft.reshape(b, FX_HEADS // 2, 2, s // TB, LANES, TB),
                            kf.reshape(b, FX_HEADS // 2, 2, s, LANES), vft)

        x = _merge(x, row(norm_mix[l]), oa, of, qm, km, vm, w_g.astype(BF16), b_gate[l].reshape(1, -1).astype(F32),
                   w_branch_da[l].astype(BF16), w_branch_fx[l].astype(BF16), w_branch_mem[l].astype(BF16),
                   w_out[l].astype(BF16))
        x = _ffn(x, row(norm_ffn[l]), w_up[l].astype(BF16), conv_w[l].astype(F32), row(conv_b[l]),
                 w_down[l].astype(BF16))
    return x
```

```python
import functools
import math

import jax
import jax.numpy as jnp
import numpy as np
from jax import lax
from jax.experimental import pallas as pl
from jax.experimental.pallas import tpu as pltpu

D_MODEL = 1024
CHUNK = 64
N_MEM = 256
EPS = 1e-6

DA_HEADS = 4
DA_QK_DIM = 64
DA_V_DIM = 128
DA_WIDTH = 512
FX_HEADS = 8
FX_DIM = 64
FX_WIDTH = 512
MEM_HEADS = 4
MEM_DIM = 128
MEM_WIDTH = 512
N_BRANCH = 3
D_FF = 2816
CONV_W = 3

IN_SIZES = (512, 512, DA_WIDTH, FX_WIDTH, FX_WIDTH, FX_WIDTH, FX_HEADS, MEM_WIDTH, N_BRANCH * D_MODEL)

LANES = 128
SUBLANES = 8
MXU_DIM = 256
VMEM_LIMIT = 56 * 1024 * 1024

LOG2E = 1.4426950408889634
NEG = -1e30

TM_PROJ = 512
TM_FFN = 512
TB = 256
FC = 256
N_SPLIT = 3
HALF = LANES // 2
ONES_ROWS = 16
Z_SLOTS = 3

F32 = jnp.float32
BF16 = jnp.bfloat16

_NT = (((1,), (1,)), ((), ()))


def _dot(a, b):
    return jnp.dot(a, b, preferred_element_type=F32)


def _dot_nt(a, b):
    return lax.dot_general(a, b, _NT, preferred_element_type=F32)


def _rms_rows(x, g):
    ms = jnp.mean(x * x, axis=-1, keepdims=True)
    return x * lax.rsqrt(ms + EPS) * g


def _three_way(a):
    hi = a.astype(BF16)
    r = a - hi.astype(F32)
    mid = r.astype(BF16)
    lo = (r - mid.astype(F32)).astype(BF16)
    return hi, mid, lo


def _lane_tile(x, reps):
    return jnp.concatenate([x] * reps, axis=1) if reps > 1 else x


def _const_spec(shape):
    nd = len(shape)
    return pl.BlockSpec(shape, lambda *_: (0,) * nd, pipeline_mode=pl.Buffered(1))


def _params(*sem):
    return pltpu.CompilerParams(dimension_semantics=sem, vmem_limit_bytes=VMEM_LIMIT)


def _inproj_kernel(x_ref, nm_ref, wqt_ref, wvt_ref, wtok_ref, gqt_ref, gtok_ref, fb_ref,
                   part_ref, pat_ref, g64_ref, g128_ref, tri_ref, kpos_ref,
                   qat_ref, qft_ref, ka_ref, kf_ref, vat_ref, vft_ref, qm_ref, carry_ref):
    tm = x_ref.shape[0]
    nb = tm // TB
    h = _rms_rows(x_ref[...], nm_ref[...]).astype(BF16)
    lane = lax.broadcasted_iota(jnp.int32, (tm, LANES), 1)

    def store_blocks(out_ref, idx, yt):
        for jb in range(nb):
            out_ref[idx, jb] = yt[:, jb * TB:(jb + 1) * TB]

    def q_slab(s):
        rows = slice(s * MXU_DIM, (s + 1) * MXU_DIM)
        yt = _dot_nt(wqt_ref[rows, :], h)
        y3 = yt.reshape(MXU_DIM // DA_QK_DIM, DA_QK_DIM, tm)
        inv = lax.rsqrt(jnp.mean(y3 * y3, axis=1, keepdims=True) + EPS)
        yn = ((y3 * inv).reshape(MXU_DIM, tm) * _lane_tile(gqt_ref[rows, :], tm // LANES)).astype(BF16)
        out_ref = (qat_ref, qft_ref)[s // 2]
        for t in range(2):
            hd = 2 * (s % 2) + t
            ones_rows = pat_ref[hd * (s // 2)]
            store_blocks(out_ref, 2 * hd, jnp.concatenate([yn[t * LANES:t * LANES + HALF], ones_rows], axis=0))
            store_blocks(out_ref, 2 * hd + 1, jnp.concatenate([ones_rows, yn[t * LANES + HALF:(t + 1) * LANES]], axis=0))

    ones = jnp.ones((ONES_ROWS, tm), BF16)

    def v_slab(s):
        rows = slice(s * MXU_DIM, (s + 1) * MXU_DIM)
        vt = _dot_nt(wvt_ref[rows, :], h).astype(BF16)
        for t in range(2):
            hd = 2 * (s % 2) + t
            if s < 2:
                store_blocks(vat_ref, hd, jnp.concatenate([vt[t * LANES:(t + 1) * LANES], ones], axis=0))
            else:
                store_blocks(vft_ref, hd, jnp.concatenate(
                    [vt[t * LANES:t * LANES + HALF], ones, vt[t * LANES + HALF:(t + 1) * LANES], ones], axis=0))


    y = _dot(h, wtok_ref[...])

    def normed_pair(col, group_ref, inv_dim):
        cols = [slice(col + s * MXU_DIM, col + (s + 1) * MXU_DIM) for s in range(2)]
        ss = [_dot((y[:, sl] * y[:, sl]).astype(BF16), group_ref[...]) for sl in cols]
        return [(y[:, sl] * lax.rsqrt(q * inv_dim + EPS) * gtok_ref[:, sl]).astype(BF16) for sl, q in zip(cols, ss)]

    fg = y[:, 2 * DA_WIDTH + MEM_WIDTH:] + fb_ref[...]
    logf = jnp.minimum(fg, 0.0) - jnp.log(1.0 + jnp.exp(-jnp.abs(fg)))
    tri = tri_ref[...]
    local = sum(_dot(tri, piece) for piece in _three_way(logf))

    q_slab(0)
    for s, yn in enumerate(normed_pair(2 * DA_WIDTH, g128_ref, 1.0 / MEM_DIM)):
        qm_ref[:, s * MXU_DIM:(s + 1) * MXU_DIM] = yn
    v_slab(0)

    for s, yn in enumerate(normed_pair(0, g64_ref, 1.0 / DA_QK_DIM)):
        for t in range(2):
            hd = 2 * s + t
            kt = yn[:, t * LANES:(t + 1) * LANES]
            ka_ref[2 * hd] = jnp.where(lane < HALF, kt, kpos_ref[hd, 0])
            ka_ref[2 * hd + 1] = jnp.where(lane >= HALF, kt, kpos_ref[hd, 1])
        (q_slab, v_slab)[s](1)

    @pl.when(pl.program_id(1) == 0)
    def _():
        carry_ref[...] = jnp.zeros_like(carry_ref)

    c = local + carry_ref[0:1, :]
    carry_ref[...] = jnp.broadcast_to(c[tm - 1:tm, :], carry_ref.shape)
    hi, mid, lo = _three_way(c * (-LOG2E))
    part = jnp.broadcast_to(part_ref[...], (tm, LANES))
    zero = jnp.zeros_like(hi)
    feat = jnp.where(part == 0, hi, jnp.where(part == 1, mid, jnp.where(part == 2, lo, zero)))
    q_slab(2)

    for s, yn in enumerate(normed_pair(DA_WIDTH, g64_ref, 1.0 / FX_DIM)):
        for t in range(2):
            a = 2 * s + t
            kt = yn[:, t * LANES:(t + 1) * LANES]
            fa = jnp.where(lane < HALF + N_SPLIT * a, zero, jnp.where(lane < HALF + N_SPLIT * (a + 1), feat, zero))
            fb = jnp.where(lane < N_SPLIT * a, zero, jnp.where(lane < N_SPLIT * (a + 1), feat, zero))
            kf_ref[2 * a] = jnp.where(lane < HALF, kt, fa)
            kf_ref[2 * a + 1] = jnp.where(lane >= HALF, kt, fb)
        (v_slab, q_slab)[s](2 + s)
    v_slab(3)


def _in_projection(x, nm, wqt, wvt, wtok, gqt, gtok, fb, part, pat, g64, g128, tri, kpos):
    b, s, d = x.shape
    tm = TM_PROJ

    def v_major(rows):
        return (jax.ShapeDtypeStruct((b, 4, s // TB, rows, TB), BF16),
                pl.BlockSpec((None, 4, tm // TB, rows, TB), lambda bi, i: (bi, 0, i, 0, 0)))

    va_major, va_spec = v_major(DA_V_DIM + ONES_ROWS)
    vf_major, vf_spec = v_major(2 * (FX_DIM + ONES_ROWS))
    q_major = jax.ShapeDtypeStruct((b, 8, s // TB, LANES, TB), BF16)
    q_spec = pl.BlockSpec((None, 8, tm // TB, LANES, TB), lambda bi, i: (bi, 0, i, 0, 0))
    k_major = jax.ShapeDtypeStruct((b, 8, s, LANES), BF16)
    k_spec = pl.BlockSpec((None, 8, tm, LANES), lambda bi, i: (bi, 0, i, 0))
    consts = (nm, wqt, wvt, wtok, gqt, gtok, fb, part, pat, g64, g128, tri)
    return pl.pallas_call(
        _inproj_kernel,
        grid=(b, s // tm),
        in_specs=[pl.BlockSpec((None, tm, d), lambda bi, i: (bi, i, 0))] + [_const_spec(c.shape) for c in consts]
        + [pl.BlockSpec((DA_HEADS, 2, tm, LANES), lambda bi, i: (0, 0, i, 0))],
        out_specs=[q_spec, q_spec, k_spec, k_spec, va_spec, vf_spec,
                   pl.BlockSpec((None, tm, MEM_WIDTH), lambda bi, i: (bi, i, 0))],
        out_shape=[q_major, q_major, k_major, k_major, va_major, vf_major,
                   jax.ShapeDtypeStruct((b, s, MEM_WIDTH), BF16)],
        scratch_shapes=[pltpu.VMEM((SUBLANES, LANES), F32)],
        compiler_params=_params("parallel", "arbitrary"),
        name="in_projection",
    )(x, *consts, kpos)


def _memkv_kernel(mem_ref, nm_ref, w_ref, gk_ref, g128_ref, km_ref, vm_ref):
    mh = _rms_rows(mem_ref[...], nm_ref[...]).astype(BF16)
    for s in range(2):
        sl = slice(s * MXU_DIM, (s + 1) * MXU_DIM)
        y = _dot(mh, w_ref[:, sl])
        ss = _dot((y * y).astype(BF16), g128_ref[...])
        km_ref[:, sl] = (y * lax.rsqrt(ss * (1.0 / MEM_DIM) + EPS) * gk_ref[:, sl]).astype(BF16)
        vm_ref[:, sl] = _dot(mh, w_ref[:, MEM_WIDTH + s * MXU_DIM:MEM_WIDTH + (s + 1) * MXU_DIM]).astype(BF16)


def _memory_kv(mem, nm, w, gk, g128):
    b, n, d = mem.shape
    out = jax.ShapeDtypeStruct((b, n, MEM_WIDTH), BF16)
    spec = pl.BlockSpec((None, n, MEM_WIDTH), lambda bi: (bi, 0, 0))
    return pl.pallas_call(
        _memkv_kernel,
        grid=(b,),
        in_specs=[pl.BlockSpec((None, n, d), lambda bi: (bi, 0, 0)),
                  _const_spec(nm.shape), _const_spec(w.shape), _const_spec(gk.shape), _const_spec(g128.shape)],
        out_specs=[spec, spec],
        out_shape=[out, out],
        compiler_params=_params("parallel"),
        name="memory_kv",
    )(mem, nm, w, gk, g128)


def _attend(q_ref, k_ref, v_block, corr_ref, acc_ref, z_ref, finish):
    nb = q_ref.shape[1]
    stages = [(i, j) for i in range(nb) for j in range(i + 1)]

    def scores(slot, i, j):
        zmax = []
        for c in range(2):
            z = _dot(k_ref[c, j * TB:(j + 1) * TB, :], q_ref[c, i])
            if i == j:
                z = z + corr_ref[...]
            z_ref[slot, c] = z
            zmax.append(jnp.max(z, axis=0, keepdims=True))
        return zmax

    ahead = z_ref.shape[0] - 1
    pending = [scores(t, *stages[t]) for t in range(ahead)]
    m = [None, None]
    for t, (i, j) in enumerate(stages):
        if t + ahead < len(stages):
            pending.append(scores((t + ahead) % (ahead + 1), *stages[t + ahead]))
        zmax = pending.pop(0)
        slot = t % (ahead + 1)
        for c in range(2):
            if j == 0:
                m[c] = zmax[c]
                acc_ref[c] = _dot(v_block(j, c), jnp.exp2(z_ref[slot, c] - m[c]).astype(BF16))
            else:
                m_new = jnp.maximum(m[c], zmax[c])
                alpha = jnp.exp2(m[c] - m_new)
                m[c] = m_new
                acc_ref[c] = alpha * acc_ref[c] + _dot(v_block(j, c), jnp.exp2(z_ref[slot, c] - m_new).astype(BF16))
        if j == i:
            finish(i)


def _attention_specs(s, v_rows):
    nb = s // TB
    q_spec = pl.BlockSpec((None, None, 2, nb, LANES, TB), lambda bi, h: (bi, h, 0, 0, 0, 0))
    k_spec = pl.BlockSpec((None, None, 2, s, LANES), lambda bi, h: (bi, h, 0, 0, 0))
    v_spec = pl.BlockSpec((None, None, nb, v_rows, TB), lambda bi, h: (bi, h, 0, 0, 0))
    o_spec = pl.BlockSpec((None, s, LANES), lambda bi, h: (bi, 0, h))
    return q_spec, k_spec, v_spec, o_spec


def _diff_kernel(slope_ref, qt_ref, k_ref, vt_ref, lq1_ref, lk1_ref, lq2_ref, lk2_ref, sub_ref, o_ref,
                 acc_ref, corr_ref, z_ref, *, lam_init):
    slope = slope_ref[pl.program_id(1)] * LOG2E
    kk = lax.broadcasted_iota(jnp.int32, (TB, TB), 0)
    qq = lax.broadcasted_iota(jnp.int32, (TB, TB), 1)
    after = jnp.where(kk > qq, (2.0 * slope) * (qq - kk).astype(F32), 0.0)
    corr_ref[...] = jnp.where((kk // CHUNK) <= (qq // CHUNK), after, NEG)

    lam = (jnp.exp(jnp.sum(lq1_ref[...] * lk1_ref[...], axis=1, keepdims=True))
           - jnp.exp(jnp.sum(lq2_ref[...] * lk2_ref[...], axis=1, keepdims=True)) + lam_init)
    gain = _lane_tile(sub_ref[...], TB // LANES) * (1.0 - lam_init)

    def finish(i):
        o0 = acc_ref[0, :DA_V_DIM, :] * (1.0 / acc_ref[0, DA_V_DIM:DA_V_DIM + 1, :])
        o1 = acc_ref[1, :DA_V_DIM, :] * (1.0 / acc_ref[1, DA_V_DIM:DA_V_DIM + 1, :])
        o = o0 - lam * o1
        ms = jnp.mean(o * o, axis=0, keepdims=True)
        o_ref[i * TB:(i + 1) * TB, :] = (o * lax.rsqrt(ms + EPS) * gain).T.astype(BF16)

    _attend(qt_ref, k_ref, lambda j, c: vt_ref[j], corr_ref, acc_ref, z_ref, finish)


def _diff_attention(qt, k, vt, slopes, lq1, lk1, lq2, lk2, sub, lam_init):
    b, nh, _, nb, _, _ = qt.shape
    s = nb * TB
    rows = vt.shape[3]
    q_spec, k_spec, v_spec, o_spec = _attention_specs(s, rows)
    vec_spec = _const_spec(lq1.shape)
    return pl.pallas_call(
        functools.partial(_diff_kernel, lam_init=lam_init),
        grid=(b, nh),
        in_specs=[pl.BlockSpec(memory_space=pltpu.SMEM), q_spec, k_spec, v_spec,
                  vec_spec, vec_spec, vec_spec, vec_spec, _const_spec(sub.shape)],
        out_specs=o_spec,
        out_shape=jax.ShapeDtypeStruct((b, s, nh * LANES), BF16),
        scratch_shapes=[pltpu.VMEM((2, rows, TB), F32), pltpu.VMEM((TB, TB), F32), pltpu.VMEM((Z_SLOTS, 2, TB, TB), F32)],
        compiler_params=_params("parallel", "parallel"),
        name="diff_attention",
    )(slopes, qt, k, vt, lq1, lk1, lq2, lk2, sub)


def _fox_kernel(qt_ref, k_ref, vt_ref, o_ref, acc_ref, corr_ref, z_ref):
    kk = lax.broadcasted_iota(jnp.int32, (TB, TB), 0)
    qq = lax.broadcasted_iota(jnp.int32, (TB, TB), 1)
    corr_ref[...] = jnp.where(kk <= qq, 0.0, NEG)
    rows = acc_ref.shape[1]

    def finish(i):
        heads = [acc_ref[hh, :FX_DIM, :] * (1.0 / acc_ref[hh, FX_DIM:FX_DIM + 1, :]) for hh in range(2)]
        o_ref[i * TB:(i + 1) * TB, :] = jnp.concatenate(heads, axis=0).T.astype(BF16)

    _attend(qt_ref, k_ref, lambda j, hh: vt_ref[j, hh * rows:(hh + 1) * rows, :], corr_ref, acc_ref, z_ref, finish)


def _fox_attention(qt, k, vt):
    b, npair, _, nb, _, _ = qt.shape
    s = nb * TB
    rows = vt.shape[3] // 2
    q_spec, k_spec, v_spec, o_spec = _attention_specs(s, 2 * rows)
    return pl.pallas_call(
        _fox_kernel,
        grid=(b, npair),
        in_specs=[q_spec, k_spec, v_spec],
        out_specs=o_spec,
        out_shape=jax.ShapeDtypeStruct((b, s, npair * LANES), BF16),
        scratch_shapes=[pltpu.VMEM((2, rows, TB), F32), pltpu.VMEM((TB, TB), F32), pltpu.VMEM((Z_SLOTS, 2, TB, TB), F32)],
        compiler_params=_params("parallel", "parallel"),
        name="fox_attention",
    )(qt, k, vt)


def _merge_kernel(x_ref, nm_ref, oa_ref, of_ref, qm_ref, km_ref, vm_ref, wg_ref, bg_ref,
                  wda_ref, wfx_ref, wmem_ref, wout_ref, o_ref, om_ref, merged_ref):
    x = x_ref[...]
    h = _rms_rows(x, nm_ref[...]).astype(BF16)

    for hd in range(MEM_HEADS):
        sl = slice(hd * MEM_DIM, (hd + 1) * MEM_DIM)
        s = _dot_nt(qm_ref[:, sl], km_ref[:, sl])
        p = jnp.exp2(s - jnp.max(s, axis=1, keepdims=True))
        inv = 1.0 / jnp.sum(p, axis=1, keepdims=True)
        om_ref[:, sl] = (_dot(p.astype(BF16), vm_ref[:, sl]) * inv).astype(BF16)

    branches = ((oa_ref, wda_ref), (of_ref, wfx_ref), (om_ref, wmem_ref))
    for n in range(D_MODEL // MXU_DIM):
        sl = slice(n * MXU_DIM, (n + 1) * MXU_DIM)
        merged = None
        for br, (o_br, w_br) in enumerate(branches):
            gsl = slice(br * D_MODEL + n * MXU_DIM, br * D_MODEL + (n + 1) * MXU_DIM)
            gate = jax.nn.sigmoid(_dot(h, wg_ref[:, gsl]) + bg_ref[:, gsl])
            term = gate * _dot(o_br[...], w_br[:, sl])
            merged = term if merged is None else merged + term
        merged_ref[:, sl] = merged.astype(BF16)

    o_ref[...] = x + _dot(merged_ref[...], wout_ref[...])


def _merge(x, nm, oa, of, qm, km, vm, wg, bg, wda, wfx, wmem, wout):
    b, s, d = x.shape
    tm = TM_PROJ
    row = lambda width: pl.BlockSpec((None, tm, width), lambda bi, i: (bi, i, 0))
    mem_spec = pl.BlockSpec((None, N_MEM, MEM_WIDTH), lambda bi, i: (bi, 0, 0))
    return pl.pallas_call(
        _merge_kernel,
        grid=(b, s // tm),
        in_specs=[row(d), _const_spec(nm.shape), row(DA_WIDTH), row(FX_WIDTH), row(MEM_WIDTH), mem_spec, mem_spec,
                  _const_spec(wg.shape), _const_spec(bg.shape), _const_spec(wda.shape), _const_spec(wfx.shape),
                  _const_spec(wmem.shape), _const_spec(wout.shape)],
        out_specs=row(d),
        out_shape=jax.ShapeDtypeStruct((b, s, d), F32),
        scratch_shapes=[pltpu.VMEM((tm, MEM_WIDTH), BF16), pltpu.VMEM((tm, d), BF16)],
        compiler_params=_params("parallel", "parallel"),
        name="merge",
    )(x, nm, oa, of, qm, km, vm, wg, bg, wda, wfx, wmem, wout)


def _ffn_kernel(x_ref, nf_ref, wup_ref, cw_ref, cb_ref, wdn_ref, o_ref, carry_ref, act_ref):
    tm = x_ref.shape[0]

    @pl.when(pl.program_id(1) == 0)
    def _():
        carry_ref[...] = jnp.zeros_like(carry_ref)

    x = x_ref[...]
    h2 = _rms_rows(x, nf_ref[...]).astype(BF16)
    row = lax.broadcasted_iota(jnp.int32, (SUBLANES, FC), 0)

    def conv(col):
        sl = slice(col, col + FC)
        u = _dot(h2, wup_ref[:, sl])
        prev = carry_ref[:, sl]
        carry_ref[:, sl] = u[tm - SUBLANES:tm, :]
        out = u * cw_ref[CONV_W - 1:CONV_W, sl] + cb_ref[:, sl]
        for shift in range(1, CONV_W):
            us = pltpu.roll(u, shift, 0)
            head = jnp.where(row < shift, pltpu.roll(prev, shift, 0), us[0:SUBLANES, :])
            us = jnp.concatenate([head, us[SUBLANES:, :]], axis=0)
            out = out + us * cw_ref[CONV_W - 1 - shift:CONV_W - shift, sl]
        return out

    for ch in range(D_FF // FC):
        a = conv(ch * FC)
        g = conv(D_FF + ch * FC)
        act_ref[:, ch * FC:(ch + 1) * FC] = (a * jax.nn.sigmoid(a) * g).astype(BF16)

    o_ref[...] = x + _dot(act_ref[...], wdn_ref[...])


def _ffn(x, nf, wup, cw, cb, wdn):
    b, s, d = x.shape
    tm = TM_FFN
    row = pl.BlockSpec((None, tm, d), lambda bi, i: (bi, i, 0))
    return pl.pallas_call(
        _ffn_kernel,
        grid=(b, s // tm),
        in_specs=[row, _const_spec(nf.shape), _const_spec(wup.shape), _const_spec(cw.shape), _const_spec(cb.shape),
                  _const_spec(wdn.shape)],
        out_specs=row,
        out_shape=jax.ShapeDtypeStruct((b, s, d), F32),
        scratch_shapes=[pltpu.VMEM((SUBLANES, 2 * D_FF), F32), pltpu.VMEM((tm, D_FF), BF16)],
        compiler_params=_params("parallel", "arbitrary"),
        name="conv_mlp",
    )(x, nf, wup, cw, cb, wdn)


def _block_diag_ones(group):
    idx = np.arange(MXU_DIM) // group
    return jnp.asarray(idx[:, None] == idx[None, :], dtype=BF16)


def _alibi_key_table(s):
    slopes = np.asarray([2.0 ** (-8.0 * (i + 1) / DA_HEADS) * LOG2E for i in range(DA_HEADS)], np.float32)
    rest = np.arange(s, dtype=np.float32)[None, :] * slopes[:, None]
    table = np.zeros((DA_HEADS, 2, s, LANES), BF16)
    for p in range(N_SPLIT):
        piece = rest.astype(BF16)
        rest = rest - piece.astype(np.float32)
        table[:, 0, :, HALF + p] = piece
        table[:, 1, :, p] = piece
    return table


def _gate_lane_layout():
    head = np.full((LANES,), -1, np.int32)
    part = np.full((LANES,), N_SPLIT, np.int32)
    for a in range(FX_HEADS // 2):
        for p in range(N_SPLIT):
            head[HALF + N_SPLIT * a + p] = 2 * a
            head[N_SPLIT * a + p] = 2 * a + 1
            part[HALF + N_SPLIT * a + p] = p
            part[N_SPLIT * a + p] = p
    return head, part


def kernel(x, mem, norm_mix, w_in, b_gate, da_q_norm, da_k_norm, da_lambda_q1, da_lambda_k1, da_lambda_q2,
           da_lambda_k2, da_subln, fx_q_norm, fx_k_norm, fx_f_bias, mem_norm, w_mem_kv, mem_q_norm, mem_k_norm,
           w_branch_da, w_branch_fx, w_branch_mem, w_out, norm_ffn, w_up, conv_w, conv_b, w_down):
    b, s, d = x.shape
    depth = w_in.shape[0]
    off = np.cumsum(np.array(IN_SIZES))[:-1].tolist()
    slopes = jnp.asarray([2.0 ** (-8.0 * (i + 1) / DA_HEADS) for i in range(DA_HEADS)], dtype=F32)
    g64 = _block_diag_ones(DA_QK_DIM)
    g128 = _block_diag_ones(MEM_DIM)
    tri = jnp.asarray(np.tril(np.ones((TM_PROJ, TM_PROJ))), dtype=BF16)
    lane_head, lane_part = _gate_lane_layout()
    lane_used = jnp.asarray(lane_head >= 0)
    lane_src = jnp.asarray(np.maximum(lane_head, 0))
    part = jnp.asarray(lane_part).reshape(1, LANES)
    kpos = jnp.asarray(_alibi_key_table(s))
    pat_np = np.zeros((FX_HEADS // 2, HALF, TM_PROJ), np.float32)
    for a in range(FX_HEADS // 2):
        pat_np[a, N_SPLIT * a:N_SPLIT * (a + 1), :] = 1.0
    pat = jnp.asarray(pat_np, dtype=BF16)
    row = lambda v: v.reshape(1, -1).astype(F32)
    col = lambda v: jnp.broadcast_to(v.reshape(-1, 1).astype(F32), (v.size, LANES))

    for l in range(depth):
        lam_init = 0.8 - 0.6 * math.exp(-0.3 * l)
        a_q, a_k, a_v, f_q, f_k, f_v, f_g, m_q, w_g = jnp.split(w_in[l], off, axis=-1)
        wqt = jnp.concatenate([a_q, f_q], axis=1).T.astype(BF16)
        wvt = jnp.concatenate([a_v, f_v], axis=1).T.astype(BF16)
        wfg = jnp.where(lane_used[None, :], f_g[:, lane_src], 0.0)
        wtok = jnp.concatenate([a_k, f_k, m_q, wfg], axis=1).astype(BF16)
        fb = jnp.where(lane_used, fx_f_bias[l][lane_src], 0.0).reshape(1, LANES).astype(F32)
        gqt = col(jnp.concatenate([jnp.tile(da_q_norm[l], 2 * DA_HEADS) * (DA_QK_DIM ** -0.5 * LOG2E),
                                   jnp.tile(fx_q_norm[l], FX_HEADS) * (FX_DIM ** -0.5 * LOG2E)]))
        gtok = row(jnp.concatenate([jnp.tile(da_k_norm[l], 2 * DA_HEADS), jnp.tile(fx_k_norm[l], FX_HEADS),
                                    jnp.tile(mem_q_norm[l], MEM_HEADS) * (MEM_DIM ** -0.5 * LOG2E)]))
        gmk = row(jnp.tile(mem_k_norm[l], MEM_HEADS))

        qat, qft, ka, kf, vat, vft, qm = _in_projection(
            x, row(norm_mix[l]), wqt, wvt, wtok, gqt, gtok, fb, part, pat, g64, g128, tri, kpos)
        km, vm = _memory_kv(mem, row(mem_norm[l]), w_mem_kv[l].astype(BF16), gmk, g128)

        oa = _diff_attention(qat.reshape(b, DA_HEADS, 2, s // TB, LANES, TB), ka.reshape(b, DA_HEADS, 2, s, LANES),
                             vat, slopes, row(da_lambda_q1[l]), row(da_lambda_k1[l]), row(da_lambda_q2[l]),
                             row(da_lambda_k2[l]), col(da_subln[l]), lam_init)
        of = _fox_attention(qft.reshape(b, FX_HEADS // 2, 2, s // TB, LANES, TB),
                            kf.reshape(b, FX_HEADS // 2, 2, s, LANES), vft)

        x = _merge(x, row(norm_mix[l]), oa, of, qm, km, vm, w_g.astype(BF16), b_gate[l].reshape(1, -1).astype(F32),
                   w_branch_da[l].astype(BF16), w_branch_fx[l].astype(BF16), w_branch_mem[l].astype(BF16),
                   w_out[l].astype(BF16))
        x = _ffn(x, row(norm_ffn[l]), w_up[l].astype(BF16), conv_w[l].astype(F32), row(conv_b[l]),
                 w_down[l].astype(BF16))
    return x
```

```python
import functools
import math

import jax
import jax.numpy as jnp
import numpy as np
from jax import lax
from jax.experimental import pallas as pl
from jax.experimental.pallas import tpu as pltpu

D_MODEL = 1024
CHUNK = 64
N_MEM = 256
EPS = 1e-6

DA_HEADS = 4
DA_QK_DIM = 64
DA_V_DIM = 128
DA_WIDTH = 512
FX_HEADS = 8
FX_DIM = 64
FX_WIDTH = 512
MEM_HEADS = 4
MEM_DIM = 128
MEM_WIDTH = 512
N_BRANCH = 3
D_FF = 2816
CONV_W = 3

IN_SIZES = (512, 512, DA_WIDTH, FX_WIDTH, FX_WIDTH, FX_WIDTH, FX_HEADS, MEM_WIDTH, N_BRANCH * D_MODEL)

LANES = 128
SUBLANES = 8
MXU_DIM = 256
VMEM_LIMIT = 56 * 1024 * 1024

LOG2E = 1.4426950408889634
NEG = -1e30

TM_PROJ = 512
TM_IN = 512
TM_FFN = 512
TB = 256
FC = 256
N_SPLIT = 3
HALF = LANES // 2
ONES_ROWS = 16
Z_SLOTS = 3

F32 = jnp.float32
BF16 = jnp.bfloat16

_NT = (((1,), (1,)), ((), ()))


def _dot(a, b):
    return jnp.dot(a, b, preferred_element_type=F32)


def _dot_nt(a, b):
    return lax.dot_general(a, b, _NT, preferred_element_type=F32)


def _rms_rows(x, g):
    ms = jnp.mean(x * x, axis=-1, keepdims=True)
    return x * lax.rsqrt(ms + EPS) * g


def _three_way(a):
    hi = a.astype(BF16)
    r = a - hi.astype(F32)
    mid = r.astype(BF16)
    lo = (r - mid.astype(F32)).astype(BF16)
    return hi, mid, lo


def _lane_tile(x, reps):
    return jnp.concatenate([x] * reps, axis=1) if reps > 1 else x


def _const_spec(shape):
    nd = len(shape)
    return pl.BlockSpec(shape, lambda *_: (0,) * nd, pipeline_mode=pl.Buffered(1))


def _params(*sem):
    return pltpu.CompilerParams(dimension_semantics=sem, vmem_limit_bytes=VMEM_LIMIT)


def _inproj_kernel(x_ref, nm_ref, wqt_ref, wvt_ref, wtok_ref, gqt_ref, gtok_ref, fb_ref,
                   part_ref, pat_ref, g64_ref, g128_ref, tri_ref, kpos_ref,
                   qat_ref, qft_ref, ka_ref, kf_ref, vat_ref, vft_ref, qm_ref, carry_ref):
    tm = x_ref.shape[0]
    nb = tm // TB
    h = _rms_rows(x_ref[...], nm_ref[...]).astype(BF16)
    lane = lax.broadcasted_iota(jnp.int32, (tm, LANES), 1)

    def store_blocks(out_ref, idx, yt):
        for jb in range(nb):
            out_ref[idx, jb] = yt[:, jb * TB:(jb + 1) * TB]

    def q_slab(s):
        rows = slice(s * MXU_DIM, (s + 1) * MXU_DIM)
        yt = _dot_nt(wqt_ref[rows, :], h)
        y3 = yt.reshape(MXU_DIM // DA_QK_DIM, DA_QK_DIM, tm)
        inv = lax.rsqrt(jnp.mean(y3 * y3, axis=1, keepdims=True) + EPS)
        yn = ((y3 * inv).reshape(MXU_DIM, tm) * _lane_tile(gqt_ref[rows, :], tm // LANES)).astype(BF16)
        out_ref = (qat_ref, qft_ref)[s // 2]
        for t in range(2):
            hd = 2 * (s % 2) + t
            ones_rows = pat_ref[hd * (s // 2)]
            store_blocks(out_ref, 2 * hd, jnp.concatenate([yn[t * LANES:t * LANES + HALF], ones_rows], axis=0))
            store_blocks(out_ref, 2 * hd + 1, jnp.concatenate([ones_rows, yn[t * LANES + HALF:(t + 1) * LANES]], axis=0))

    ones = jnp.ones((ONES_ROWS, tm), BF16)

    def v_slab(s):
        rows = slice(s * MXU_DIM, (s + 1) * MXU_DIM)
        vt = _dot_nt(wvt_ref[rows, :], h).astype(BF16)
        for t in range(2):
            hd = 2 * (s % 2) + t
            if s < 2:
                store_blocks(vat_ref, hd, jnp.concatenate([vt[t * LANES:(t + 1) * LANES], ones], axis=0))
            else:
                store_blocks(vft_ref, hd, jnp.concatenate(
                    [vt[t * LANES:t * LANES + HALF], ones, vt[t * LANES + HALF:(t + 1) * LANES], ones], axis=0))


    y = _dot(h, wtok_ref[...])

    def normed_pair(col, group_ref, inv_dim):
        cols = [slice(col + s * MXU_DIM, col + (s + 1) * MXU_DIM) for s in range(2)]
        ss = [_dot((y[:, sl] * y[:, sl]).astype(BF16), group_ref[...]) for sl in cols]
        return [(y[:, sl] * lax.rsqrt(q * inv_dim + EPS) * gtok_ref[:, sl]).astype(BF16) for sl, q in zip(cols, ss)]

    fg = y[:, 2 * DA_WIDTH + MEM_WIDTH:] + fb_ref[...]
    logf = jnp.minimum(fg, 0.0) - jnp.log(1.0 + jnp.exp(-jnp.abs(fg)))
    tri = tri_ref[...]
    pieces = _three_way(logf)
    local = [sum(_dot(tri, piece[r * MXU_DIM:(r + 1) * MXU_DIM]) for piece in pieces)
             for r in range(tm // MXU_DIM)]

    q_slab(0)
    for s, yn in enumerate(normed_pair(2 * DA_WIDTH, g128_ref, 1.0 / MEM_DIM)):
        qm_ref[:, s * MXU_DIM:(s + 1) * MXU_DIM] = yn
    v_slab(0)

    for s, yn in enumerate(normed_pair(0, g64_ref, 1.0 / DA_QK_DIM)):
        for t in range(2):
            hd = 2 * s + t
            kt = yn[:, t * LANES:(t + 1) * LANES]
            ka_ref[2 * hd] = jnp.where(lane < HALF, kt, kpos_ref[hd, 0])
            ka_ref[2 * hd + 1] = jnp.where(lane >= HALF, kt, kpos_ref[hd, 1])
        (q_slab, v_slab)[s](1)

    @pl.when(pl.program_id(1) == 0)
    def _():
        carry_ref[...] = jnp.zeros_like(carry_ref)

    run = carry_ref[0:1, :]
    blocks = []
    for blk in local:
        blocks.append(blk + run)
        run = blocks[-1][MXU_DIM - 1:MXU_DIM, :]
    c = jnp.concatenate(blocks, axis=0)
    carry_ref[...] = jnp.broadcast_to(run, carry_ref.shape)
    hi, mid, lo = _three_way(c * (-LOG2E))
    part = jnp.broadcast_to(part_ref[...], (tm, LANES))
    zero = jnp.zeros_like(hi)
    feat = jnp.where(part == 0, hi, jnp.where(part == 1, mid, jnp.where(part == 2, lo, zero)))
    q_slab(2)

    for s, yn in enumerate(normed_pair(DA_WIDTH, g64_ref, 1.0 / FX_DIM)):
        for t in range(2):
            a = 2 * s + t
            kt = yn[:, t * LANES:(t + 1) * LANES]
            fa = jnp.where(lane < HALF + N_SPLIT * a, zero, jnp.where(lane < HALF + N_SPLIT * (a + 1), feat, zero))
            fb = jnp.where(lane < N_SPLIT * a, zero, jnp.where(lane < N_SPLIT * (a + 1), feat, zero))
            kf_ref[2 * a] = jnp.where(lane < HALF, kt, fa)
            kf_ref[2 * a + 1] = jnp.where(lane >= HALF, kt, fb)
        (v_slab, q_slab)[s](2 + s)
    v_slab(3)


def _in_projection(x, nm, wqt, wvt, wtok, gqt, gtok, fb, part, pat, g64, g128, tri, kpos):
    b, s, d = x.shape
    tm = TM_IN

    def v_major(rows):
        return (jax.ShapeDtypeStruct((b, 4, s // TB, rows, TB), BF16),
                pl.BlockSpec((None, 4, tm // TB, rows, TB), lambda bi, i: (bi, 0, i, 0, 0)))

    va_major, va_spec = v_major(DA_V_DIM + ONES_ROWS)
    vf_major, vf_spec = v_major(2 * (FX_DIM + ONES_ROWS))
    q_major = jax.ShapeDtypeStruct((b, 8, s // TB, LANES, TB), BF16)
    q_spec = pl.BlockSpec((None, 8, tm // TB, LANES, TB), lambda bi, i: (bi, 0, i, 0, 0))
    k_major = jax.ShapeDtypeStruct((b, 8, s, LANES), BF16)
    k_spec = pl.BlockSpec((None, 8, tm, LANES), lambda bi, i: (bi, 0, i, 0))
    consts = (nm, wqt, wvt, wtok, gqt, gtok, fb, part, pat, g64, g128, tri)
    return pl.pallas_call(
        _inproj_kernel,
        grid=(b, s // tm),
        in_specs=[pl.BlockSpec((None, tm, d), lambda bi, i: (bi, i, 0))] + [_const_spec(c.shape) for c in consts]
        + [pl.BlockSpec((DA_HEADS, 2, tm, LANES), lambda bi, i: (0, 0, i, 0))],
        out_specs=[q_spec, q_spec, k_spec, k_spec, va_spec, vf_spec,
                   pl.BlockSpec((None, tm, MEM_WIDTH), lambda bi, i: (bi, i, 0))],
        out_shape=[q_major, q_major, k_major, k_major, va_major, vf_major,
                   jax.ShapeDtypeStruct((b, s, MEM_WIDTH), BF16)],
        scratch_shapes=[pltpu.VMEM((SUBLANES, LANES), F32)],
        compiler_params=_params("parallel", "arbitrary"),
        name="in_projection",
    )(x, *consts, kpos)


def _memkv_kernel(mem_ref, nm_ref, w_ref, gk_ref, g128_ref, km_ref, vm_ref):
    mh = _rms_rows(mem_ref[...], nm_ref[...]).astype(BF16)
    for s in range(2):
        sl = slice(s * MXU_DIM, (s + 1) * MXU_DIM)
        y = _dot(mh, w_ref[:, sl])
        ss = _dot((y * y).astype(BF16), g128_ref[...])
        km_ref[:, sl] = (y * lax.rsqrt(ss * (1.0 / MEM_DIM) + EPS) * gk_ref[:, sl]).astype(BF16)
        vm_ref[:, sl] = _dot(mh, w_ref[:, MEM_WIDTH + s * MXU_DIM:MEM_WIDTH + (s + 1) * MXU_DIM]).astype(BF16)


def _memory_kv(mem, nm, w, gk, g128):
    b, n, d = mem.shape
    out = jax.ShapeDtypeStruct((b, n, MEM_WIDTH), BF16)
    spec = pl.BlockSpec((None, n, MEM_WIDTH), lambda bi: (bi, 0, 0))
    return pl.pallas_call(
        _memkv_kernel,
        grid=(b,),
        in_specs=[pl.BlockSpec((None, n, d), lambda bi: (bi, 0, 0)),
                  _const_spec(nm.shape), _const_spec(w.shape), _const_spec(gk.shape), _const_spec(g128.shape)],
        out_specs=[spec, spec],
        out_shape=[out, out],
        compiler_params=_params("parallel"),
        name="memory_kv",
    )(mem, nm, w, gk, g128)


def _attend(q_ref, k_ref, v_block, corr_ref, acc_ref, z_ref, finish):
    nb = q_ref.shape[1]
    stages = [(i, j) for i in range(nb) for j in range(i + 1)]

    def scores(slot, i, j):
        zmax = []
        for c in range(2):
            z = _dot(k_ref[c, j * TB:(j + 1) * TB, :], q_ref[c, i])
            if i == j:
                z = z + corr_ref[...]
            z_ref[slot, c] = z
            zmax.append(jnp.max(z, axis=0, keepdims=True))
        return zmax

    ahead = z_ref.shape[0] - 1
    pending = [scores(t, *stages[t]) for t in range(ahead)]
    m = [None, None]
    for t, (i, j) in enumerate(stages):
        if t + ahead < len(stages):
            pending.append(scores((t + ahead) % (ahead + 1), *stages[t + ahead]))
        zmax = pending.pop(0)
        slot = t % (ahead + 1)
        for c in range(2):
            if j == 0:
                m[c] = zmax[c]
                acc_ref[c] = _dot(v_block(j, c), jnp.exp2(z_ref[slot, c] - m[c]).astype(BF16))
            else:
                m_new = jnp.maximum(m[c], zmax[c])
                alpha = jnp.exp2(m[c] - m_new)
                m[c] = m_new
                acc_ref[c] = alpha * acc_ref[c] + _dot(v_block(j, c), jnp.exp2(z_ref[slot, c] - m_new).astype(BF16))
        if j == i:
            finish(i)


def _attention_specs(s, v_rows):
    nb = s // TB
    q_spec = pl.BlockSpec((None, None, 2, nb, LANES, TB), lambda bi, h: (bi, h, 0, 0, 0, 0))
    k_spec = pl.BlockSpec((None, None, 2, s, LANES), lambda bi, h: (bi, h, 0, 0, 0))
    v_spec = pl.BlockSpec((None, None, nb, v_rows, TB), lambda bi, h: (bi, h, 0, 0, 0))
    o_spec = pl.BlockSpec((None, None, s, LANES), lambda bi, h: (bi, h, 0, 0))
    return q_spec, k_spec, v_spec, o_spec


def _diff_kernel(slope_ref, qt_ref, k_ref, vt_ref, lq1_ref, lk1_ref, lq2_ref, lk2_ref, sub_ref, o_ref,
                 acc_ref, corr_ref, z_ref, *, lam_init):
    slope = slope_ref[pl.program_id(1)] * LOG2E
    kk = lax.broadcasted_iota(jnp.int32, (TB, TB), 0)
    qq = lax.broadcasted_iota(jnp.int32, (TB, TB), 1)
    after = jnp.where(kk > qq, (2.0 * slope) * (qq - kk).astype(F32), 0.0)
    corr_ref[...] = jnp.where((kk // CHUNK) <= (qq // CHUNK), after, NEG)

    lam = (jnp.exp(jnp.sum(lq1_ref[...] * lk1_ref[...], axis=1, keepdims=True))
           - jnp.exp(jnp.sum(lq2_ref[...] * lk2_ref[...], axis=1, keepdims=True)) + lam_init)
    gain = _lane_tile(sub_ref[...], TB // LANES) * (1.0 - lam_init)

    def finish(i):
        o0 = acc_ref[0, :DA_V_DIM, :] * (1.0 / acc_ref[0, DA_V_DIM:DA_V_DIM + 1, :])
        o1 = acc_ref[1, :DA_V_DIM, :] * (1.0 / acc_ref[1, DA_V_DIM:DA_V_DIM + 1, :])
        o = o0 - lam * o1
        ms = jnp.mean(o * o, axis=0, keepdims=True)
        o_ref[i * TB:(i + 1) * TB, :] = (o * lax.rsqrt(ms + EPS) * gain).T.astype(BF16)

    _attend(qt_ref, k_ref, lambda j, c: vt_ref[j], corr_ref, acc_ref, z_ref, finish)


def _diff_attention(qt, k, vt, slopes, lq1, lk1, lq2, lk2, sub, lam_init):
    b, nh, _, nb, _, _ = qt.shape
    s = nb * TB
    rows = vt.shape[3]
    q_spec, k_spec, v_spec, o_spec = _attention_specs(s, rows)
    vec_spec = _const_spec(lq1.shape)
    return pl.pallas_call(
        functools.partial(_diff_kernel, lam_init=lam_init),
        grid=(b, nh),
        in_specs=[pl.BlockSpec(memory_space=pltpu.SMEM), q_spec, k_spec, v_spec,
                  vec_spec, vec_spec, vec_spec, vec_spec, _const_spec(sub.shape)],
        out_specs=o_spec,
        out_shape=jax.ShapeDtypeStruct((b, nh, s, LANES), BF16),
        scratch_shapes=[pltpu.VMEM((2, rows, TB), F32), pltpu.VMEM((TB, TB), F32), pltpu.VMEM((Z_SLOTS, 2, TB, TB), F32)],
        compiler_params=_params("parallel", "parallel"),
        name="diff_attention",
    )(slopes, qt, k, vt, lq1, lk1, lq2, lk2, sub)


def _fox_kernel(qt_ref, k_ref, vt_ref, o_ref, acc_ref, corr_ref, z_ref):
    kk = lax.broadcasted_iota(jnp.int32, (TB, TB), 0)
    qq = lax.broadcasted_iota(jnp.int32, (TB, TB), 1)
    corr_ref[...] = jnp.where(kk <= qq, 0.0, NEG)
    rows = acc_ref.shape[1]

    def finish(i):
        heads = [acc_ref[hh, :FX_DIM, :] * (1.0 / acc_ref[hh, FX_DIM:FX_DIM + 1, :]) for hh in range(2)]
        o_ref[i * TB:(i + 1) * TB, :] = jnp.concatenate(heads, axis=0).T.astype(BF16)

    _attend(qt_ref, k_ref, lambda j, hh: vt_ref[j, hh * rows:(hh + 1) * rows, :], corr_ref, acc_ref, z_ref, finish)


def _fox_attention(qt, k, vt):
    b, npair, _, nb, _, _ = qt.shape
    s = nb * TB
    rows = vt.shape[3] // 2
    q_spec, k_spec, v_spec, o_spec = _attention_specs(s, 2 * rows)
    return pl.pallas_call(
        _fox_kernel,
        grid=(b, npair),
        in_specs=[q_spec, k_spec, v_spec],
        out_specs=o_spec,
        out_shape=jax.ShapeDtypeStruct((b, npair, s, LANES), BF16),
        scratch_shapes=[pltpu.VMEM((2, rows, TB), F32), pltpu.VMEM((TB, TB), F32), pltpu.VMEM((Z_SLOTS, 2, TB, TB), F32)],
        compiler_params=_params("parallel", "parallel"),
        name="fox_attention",
    )(qt, k, vt)


def _merge_kernel(x_ref, nm_ref, oa_ref, of_ref, qm_ref, km_ref, vm_ref, wg_ref, bg_ref,
                  wda_ref, wfx_ref, wmem_ref, wout_ref, o_ref, om_ref, merged_ref):
    x = x_ref[...]
    h = _rms_rows(x, nm_ref[...]).astype(BF16)

    for hd in range(MEM_HEADS):
        sl = slice(hd * MEM_DIM, (hd + 1) * MEM_DIM)
        s = _dot_nt(qm_ref[:, sl], km_ref[:, sl])
        p = jnp.exp2(s - jnp.max(s, axis=1, keepdims=True))
        inv = 1.0 / jnp.sum(p, axis=1, keepdims=True)
        om_ref[:, sl] = (_dot(p.astype(BF16), vm_ref[:, sl]) * inv).astype(BF16)

    oa = jnp.concatenate([oa_ref[hd] for hd in range(oa_ref.shape[0])], axis=1)
    of = jnp.concatenate([of_ref[hd] for hd in range(of_ref.shape[0])], axis=1)
    branches = ((oa, wda_ref), (of, wfx_ref), (om_ref[...], wmem_ref))
    for n in range(D_MODEL // MXU_DIM):
        sl = slice(n * MXU_DIM, (n + 1) * MXU_DIM)
        merged = None
        for br, (o_br, w_br) in enumerate(branches):
            gsl = slice(br * D_MODEL + n * MXU_DIM, br * D_MODEL + (n + 1) * MXU_DIM)
            gate = jax.nn.sigmoid(_dot(h, wg_ref[:, gsl]) + bg_ref[:, gsl])
            term = gate * _dot(o_br, w_br[:, sl])
            merged = term if merged is None else merged + term
        merged_ref[:, sl] = merged.astype(BF16)

    o_ref[...] = x + _dot(merged_ref[...], wout_ref[...])


def _merge(x, nm, oa, of, qm, km, vm, wg, bg, wda, wfx, wmem, wout):
    b, s, d = x.shape
    tm = TM_PROJ
    row = lambda width: pl.BlockSpec((None, tm, width), lambda bi, i: (bi, i, 0))
    heads = pl.BlockSpec((None, oa.shape[1], tm, LANES), lambda bi, i: (bi, 0, i, 0))
    mem_spec = pl.BlockSpec((None, N_MEM, MEM_WIDTH), lambda bi, i: (bi, 0, 0))
    return pl.pallas_call(
        _merge_kernel,
        grid=(b, s // tm),
        in_specs=[row(d), _const_spec(nm.shape), heads, heads, row(MEM_WIDTH), mem_spec, mem_spec,
                  _const_spec(wg.shape), _const_spec(bg.shape), _const_spec(wda.shape), _const_spec(wfx.shape),
                  _const_spec(wmem.shape), _const_spec(wout.shape)],
        out_specs=row(d),
        out_shape=jax.ShapeDtypeStruct((b, s, d), F32),
        scratch_shapes=[pltpu.VMEM((tm, MEM_WIDTH), BF16), pltpu.VMEM((tm, d), BF16)],
        compiler_params=_params("parallel", "parallel"),
        name="merge",
    )(x, nm, oa, of, qm, km, vm, wg, bg, wda, wfx, wmem, wout)


def _ffn_kernel(x_ref, nf_ref, wup_ref, cw_ref, cb_ref, wdn_ref, o_ref, carry_ref, act_ref):
    tm = x_ref.shape[0]

    @pl.when(pl.program_id(1) == 0)
    def _():
        carry_ref[...] = jnp.zeros_like(carry_ref)

    x = x_ref[...]
    h2 = _rms_rows(x, nf_ref[...]).astype(BF16)
    row = lax.broadcasted_iota(jnp.int32, (SUBLANES, FC), 0)

    def conv(col):
        sl = slice(col, col + FC)
        u = _dot(h2, wup_ref[:, sl])
        prev = carry_ref[:, sl]
        carry_ref[:, sl] = u[tm - SUBLANES:tm, :]
        out = u * cw_ref[CONV_W - 1:CONV_W, sl] + cb_ref[:, sl]
        for shift in range(1, CONV_W):
            us = pltpu.roll(u, shift, 0)
            head = jnp.where(row < shift, pltpu.roll(prev, shift, 0), us[0:SUBLANES, :])
            us = jnp.concatenate([head, us[SUBLANES:, :]], axis=0)
            out = out + us * cw_ref[CONV_W - 1 - shift:CONV_W - shift, sl]
        return out

    for ch in range(D_FF // FC):
        a = conv(ch * FC)
        g = conv(D_FF + ch * FC)
        act_ref[:, ch * FC:(ch + 1) * FC] = (a * jax.nn.sigmoid(a) * g).astype(BF16)

    o_ref[...] = x + _dot(act_ref[...], wdn_ref[...])


def _ffn(x, nf, wup, cw, cb, wdn):
    b, s, d = x.shape
    tm = TM_FFN
    row = pl.BlockSpec((None, tm, d), lambda bi, i: (bi, i, 0))
    return pl.pallas_call(
        _ffn_kernel,
        grid=(b, s // tm),
        in_specs=[row, _const_spec(nf.shape), _const_spec(wup.shape), _const_spec(cw.shape), _const_spec(cb.shape),
                  _const_spec(wdn.shape)],
        out_specs=row,
        out_shape=jax.ShapeDtypeStruct((b, s, d), F32),
        scratch_shapes=[pltpu.VMEM((SUBLANES, 2 * D_FF), F32), pltpu.VMEM((tm, D_FF), BF16)],
        compiler_params=_params("parallel", "arbitrary"),
        name="conv_mlp",
    )(x, nf, wup, cw, cb, wdn)


def _block_diag_ones(group):
    idx = np.arange(MXU_DIM) // group
    return jnp.asarray(idx[:, None] == idx[None, :], dtype=BF16)


def _alibi_key_table(s):
    slopes = np.asarray([2.0 ** (-8.0 * (i + 1) / DA_HEADS) * LOG2E for i in range(DA_HEADS)], np.float32)
    rest = np.arange(s, dtype=np.float32)[None, :] * slopes[:, None]
    table = np.zeros((DA_HEADS, 2, s, LANES), BF16)
    for p in range(N_SPLIT):
        piece = rest.astype(BF16)
        rest = rest - piece.astype(np.float32)
        table[:, 0, :, HALF + p] = piece
        table[:, 1, :, p] = piece
    return table


def _gate_lane_layout():
    head = np.full((LANES,), -1, np.int32)
    part = np.full((LANES,), N_SPLIT, np.int32)
    for a in range(FX_HEADS // 2):
        for p in range(N_SPLIT):
            head[HALF + N_SPLIT * a + p] = 2 * a
            head[N_SPLIT * a + p] = 2 * a + 1
            part[HALF + N_SPLIT * a + p] = p
            part[N_SPLIT * a + p] = p
    return head, part


def kernel(x, mem, norm_mix, w_in, b_gate, da_q_norm, da_k_norm, da_lambda_q1, da_lambda_k1, da_lambda_q2,
           da_lambda_k2, da_subln, fx_q_norm, fx_k_norm, fx_f_bias, mem_norm, w_mem_kv, mem_q_norm, mem_k_norm,
           w_branch_da, w_branch_fx, w_branch_mem, w_out, norm_ffn, w_up, conv_w, conv_b, w_down):
    b, s, d = x.shape
    depth = w_in.shape[0]
    off = np.cumsum(np.array(IN_SIZES))[:-1].tolist()
    slopes = jnp.asarray([2.0 ** (-8.0 * (i + 1) / DA_HEADS) for i in range(DA_HEADS)], dtype=F32)
    g64 = _block_diag_ones(DA_QK_DIM)
    g128 = _block_diag_ones(MEM_DIM)
    tri = jnp.asarray(np.tril(np.ones((MXU_DIM, MXU_DIM))), dtype=BF16)
    lane_head, lane_part = _gate_lane_layout()
    lane_used = jnp.asarray(lane_head >= 0)
    lane_src = jnp.asarray(np.maximum(lane_head, 0))
    part = jnp.asarray(lane_part).reshape(1, LANES)
    kpos = jnp.asarray(_alibi_key_table(s))
    pat_np = np.zeros((FX_HEADS // 2, HALF, TM_IN), np.float32)
    for a in range(FX_HEADS // 2):
        pat_np[a, N_SPLIT * a:N_SPLIT * (a + 1), :] = 1.0
    pat = jnp.asarray(pat_np, dtype=BF16)
    row = lambda v: v.reshape(1, -1).astype(F32)
    col = lambda v: jnp.broadcast_to(v.reshape(-1, 1).astype(F32), (v.size, LANES))

    for l in range(depth):
        lam_init = 0.8 - 0.6 * math.exp(-0.3 * l)
        a_q, a_k, a_v, f_q, f_k, f_v, f_g, m_q, w_g = jnp.split(w_in[l], off, axis=-1)
        wqt = jnp.concatenate([a_q, f_q], axis=1).T.astype(BF16)
        wvt = jnp.concatenate([a_v, f_v], axis=1).T.astype(BF16)
        wfg = jnp.where(lane_used[None, :], f_g[:, lane_src], 0.0)
        wtok = jnp.concatenate([a_k, f_k, m_q, wfg], axis=1).astype(BF16)
        fb = jnp.where(lane_used, fx_f_bias[l][lane_src], 0.0).reshape(1, LANES).astype(F32)
        gqt = col(jnp.concatenate([jnp.tile(da_q_norm[l], 2 * DA_HEADS) * (DA_QK_DIM ** -0.5 * LOG2E),
                                   jnp.tile(fx_q_norm[l], FX_HEADS) * (FX_DIM ** -0.5 * LOG2E)]))
        gtok = row(jnp.concatenate([jnp.tile(da_k_norm[l], 2 * DA_HEADS), jnp.tile(fx_k_norm[l], FX_HEADS),
                                    jnp.tile(mem_q_norm[l], MEM_HEADS) * (MEM_DIM ** -0.5 * LOG2E)]))
        gmk = row(jnp.tile(mem_k_norm[l], MEM_HEADS))

        qat, qft, ka, kf, vat, vft, qm = _in_projection(
            x, row(norm_mix[l]), wqt, wvt, wtok, gqt, gtok, fb, part, pat, g64, g128, tri, kpos)
        km, vm = _memory_kv(mem, row(mem_norm[l]), w_mem_kv[l].astype(BF16), gmk, g128)

        oa = _diff_attention(qat.reshape(b, DA_HEADS, 2, s // TB, LANES, TB), ka.reshape(b, DA_HEADS, 2, s, LANES),
                             vat, slopes, row(da_lambda_q1[l]), row(da_lambda_k1[l]), row(da_lambda_q2[l]),
                             row(da_lambda_k2[l]), col(da_subln[l]), lam_init)
        of = _fox_attention(qft.reshape(b, FX_HEADS // 2, 2, s // TB, LANES, TB),
                            kf.reshape(b, FX_HEADS // 2, 2, s, LANES), vft)

        x = _merge(x, row(norm_mix[l]), oa, of, qm, km, vm, w_g.astype(BF16), b_gate[l].reshape(1, -1).astype(F32),
                   w_branch_da[l].astype(BF16), w_branch_fx[l].astype(BF16), w_branch_mem[l].astype(BF16),
                   w_out[l].astype(BF16))
        x = _ffn(x, row(norm_ffn[l]), w_up[l].astype(BF16), conv_w[l].astype(F32), row(conv_b[l]),
                 w_down[l].astype(BF16))
    return x
```

```python
import functools
import math

import jax
import jax.numpy as jnp
import numpy as np
from jax import lax
from jax.experimental import pallas as pl
from jax.experimental.pallas import tpu as pltpu

D_MODEL = 1024
CHUNK = 64
N_MEM = 256
EPS = 1e-6

DA_HEADS = 4
DA_QK_DIM = 64
DA_V_DIM = 128
DA_WIDTH = 512
FX_HEADS = 8
FX_DIM = 64
FX_WIDTH = 512
MEM_HEADS = 4
MEM_DIM = 128
MEM_WIDTH = 512
N_BRANCH = 3
D_FF = 2816
CONV_W = 3

IN_SIZES = (512, 512, DA_WIDTH, FX_WIDTH, FX_WIDTH, FX_WIDTH, FX_HEADS, MEM_WIDTH, N_BRANCH * D_MODEL)

LANES = 128
SUBLANES = 8
MXU_DIM = 256
VMEM_LIMIT = 56 * 1024 * 1024

LOG2E = 1.4426950408889634
NEG = -1e30

TM_PROJ = 1024
TM_IN = 512
TM_FFN = 1024
TB = 256
FC = 256
N_SPLIT = 3
HALF = LANES // 2
ONES_ROWS = 16
Z_SLOTS = 3

F32 = jnp.float32
BF16 = jnp.bfloat16

_NT = (((1,), (1,)), ((), ()))


def _dot(a, b):
    return jnp.dot(a, b, preferred_element_type=F32)


def _dot_nt(a, b):
    return lax.dot_general(a, b, _NT, preferred_element_type=F32)


def _rms_rows(x, g):
    ms = jnp.mean(x * x, axis=-1, keepdims=True)
    return x * lax.rsqrt(ms + EPS) * g


def _three_way(a):
    hi = a.astype(BF16)
    r = a - hi.astype(F32)
    mid = r.astype(BF16)
    lo = (r - mid.astype(F32)).astype(BF16)
    return hi, mid, lo


def _lane_tile(x, reps):
    return jnp.concatenate([x] * reps, axis=1) if reps > 1 else x


def _const_spec(shape):
    nd = len(shape)
    return pl.BlockSpec(shape, lambda *_: (0,) * nd, pipeline_mode=pl.Buffered(1))


def _params(*sem):
    return pltpu.CompilerParams(dimension_semantics=sem, vmem_limit_bytes=VMEM_LIMIT)


def _inproj_kernel(x_ref, nm_ref, wqt_ref, wvt_ref, wtok_ref, gqt_ref, gtok_ref, fb_ref,
                   part_ref, pat_ref, g64_ref, g128_ref, tri_ref, kpos_ref,
                   qat_ref, qft_ref, ka_ref, kf_ref, vat_ref, vft_ref, qm_ref, carry_ref):
    tm = x_ref.shape[0]
    nb = tm // TB
    h = _rms_rows(x_ref[...], nm_ref[...]).astype(BF16)
    lane = lax.broadcasted_iota(jnp.int32, (tm, LANES), 1)

    def store_blocks(out_ref, idx, yt):
        for jb in range(nb):
            out_ref[idx, jb] = yt[:, jb * TB:(jb + 1) * TB]

    def q_slab(s):
        rows = slice(s * MXU_DIM, (s + 1) * MXU_DIM)
        yt = _dot_nt(wqt_ref[rows, :], h)
        y3 = yt.reshape(MXU_DIM // DA_QK_DIM, DA_QK_DIM, tm)
        inv = lax.rsqrt(jnp.mean(y3 * y3, axis=1, keepdims=True) + EPS)
        yn = ((y3 * inv).reshape(MXU_DIM, tm) * _lane_tile(gqt_ref[rows, :], tm // LANES)).astype(BF16)
        out_ref = (qat_ref, qft_ref)[s // 2]
        for t in range(2):
            hd = 2 * (s % 2) + t
            ones_rows = pat_ref[hd * (s // 2)]
            store_blocks(out_ref, 2 * hd, jnp.concatenate([yn[t * LANES:t * LANES + HALF], ones_rows], axis=0))
            store_blocks(out_ref, 2 * hd + 1, jnp.concatenate([ones_rows, yn[t * LANES + HALF:(t + 1) * LANES]], axis=0))

    ones = jnp.ones((ONES_ROWS, tm), BF16)

    def v_slab(s):
        rows = slice(s * MXU_DIM, (s + 1) * MXU_DIM)
        vt = _dot_nt(wvt_ref[rows, :], h).astype(BF16)
        for t in range(2):
            hd = 2 * (s % 2) + t
            if s < 2:
                store_blocks(vat_ref, hd, jnp.concatenate([vt[t * LANES:(t + 1) * LANES], ones], axis=0))
            else:
                store_blocks(vft_ref, hd, jnp.concatenate(
                    [vt[t * LANES:t * LANES + HALF], ones, vt[t * LANES + HALF:(t + 1) * LANES], ones], axis=0))


    y = _dot(h, wtok_ref[...])

    def normed_pair(col, group_ref, inv_dim):
        cols = [slice(col + s * MXU_DIM, col + (s + 1) * MXU_DIM) for s in range(2)]
        ss = [_dot((y[:, sl] * y[:, sl]).astype(BF16), group_ref[...]) for sl in cols]
        return [(y[:, sl] * lax.rsqrt(q * inv_dim + EPS) * gtok_ref[:, sl]).astype(BF16) for sl, q in zip(cols, ss)]

    fg = y[:, 2 * DA_WIDTH + MEM_WIDTH:] + fb_ref[...]
    logf = jnp.minimum(fg, 0.0) - jnp.log(1.0 + jnp.exp(-jnp.abs(fg)))
    tri = tri_ref[...]
    pieces = _three_way(logf)
    local = [sum(_dot(tri, piece[r * MXU_DIM:(r + 1) * MXU_DIM]) for piece in pieces)
             for r in range(tm // MXU_DIM)]

    q_slab(0)
    for s, yn in enumerate(normed_pair(2 * DA_WIDTH, g128_ref, 1.0 / MEM_DIM)):
        qm_ref[:, s * MXU_DIM:(s + 1) * MXU_DIM] = yn
    v_slab(0)

    for s, yn in enumerate(normed_pair(0, g64_ref, 1.0 / DA_QK_DIM)):
        for t in range(2):
            hd = 2 * s + t
            kt = yn[:, t * LANES:(t + 1) * LANES]
            ka_ref[2 * hd] = jnp.where(lane < HALF, kt, kpos_ref[hd, 0])
            ka_ref[2 * hd + 1] = jnp.where(lane >= HALF, kt, kpos_ref[hd, 1])
        (q_slab, v_slab)[s](1)

    @pl.when(pl.program_id(1) == 0)
    def _():
        carry_ref[...] = jnp.zeros_like(carry_ref)

    run = carry_ref[0:1, :]
    blocks = []
    for blk in local:
        blocks.append(blk + run)
        run = blocks[-1][MXU_DIM - 1:MXU_DIM, :]
    c = jnp.concatenate(blocks, axis=0)
    carry_ref[...] = jnp.broadcast_to(run, carry_ref.shape)
    hi, mid, lo = _three_way(c * (-LOG2E))
    part = jnp.broadcast_to(part_ref[...], (tm, LANES))
    zero = jnp.zeros_like(hi)
    feat = jnp.where(part == 0, hi, jnp.where(part == 1, mid, jnp.where(part == 2, lo, zero)))
    q_slab(2)

    for s, yn in enumerate(normed_pair(DA_WIDTH, g64_ref, 1.0 / FX_DIM)):
        for t in range(2):
            a = 2 * s + t
            kt = yn[:, t * LANES:(t + 1) * LANES]
            fa = jnp.where(lane < HALF + N_SPLIT * a, zero, jnp.where(lane < HALF + N_SPLIT * (a + 1), feat, zero))
            fb = jnp.where(lane < N_SPLIT * a, zero, jnp.where(lane < N_SPLIT * (a + 1), feat, zero))
            kf_ref[2 * a] = jnp.where(lane < HALF, kt, fa)
            kf_ref[2 * a + 1] = jnp.where(lane >= HALF, kt, fb)
        (v_slab, q_slab)[s](2 + s)
    v_slab(3)


def _in_projection(x, nm, wqt, wvt, wtok, gqt, gtok, fb, part, pat, g64, g128, tri, kpos):
    b, s, d = x.shape
    tm = TM_IN

    def v_major(rows):
        return (jax.ShapeDtypeStruct((b, 4, s // TB, rows, TB), BF16),
                pl.BlockSpec((None, 4, tm // TB, rows, TB), lambda bi, i: (bi, 0, i, 0, 0)))

    va_major, va_spec = v_major(DA_V_DIM + ONES_ROWS)
    vf_major, vf_spec = v_major(2 * (FX_DIM + ONES_ROWS))
    q_major = jax.ShapeDtypeStruct((b, 8, s // TB, LANES, TB), BF16)
    q_spec = pl.BlockSpec((None, 8, tm // TB, LANES, TB), lambda bi, i: (bi, 0, i, 0, 0))
    k_major = jax.ShapeDtypeStruct((b, 8, s, LANES), BF16)
    k_spec = pl.BlockSpec((None, 8, tm, LANES), lambda bi, i: (bi, 0, i, 0))
    consts = (nm, wqt, wvt, wtok, gqt, gtok, fb, part, pat, g64, g128, tri)
    return pl.pallas_call(
        _inproj_kernel,
        grid=(b, s // tm),
        in_specs=[pl.BlockSpec((None, tm, d), lambda bi, i: (bi, i, 0))] + [_const_spec(c.shape) for c in consts]
        + [pl.BlockSpec((DA_HEADS, 2, tm, LANES), lambda bi, i: (0, 0, i, 0))],
        out_specs=[q_spec, q_spec, k_spec, k_spec, va_spec, vf_spec,
                   pl.BlockSpec((None, tm, MEM_WIDTH), lambda bi, i: (bi, i, 0))],
        out_shape=[q_major, q_major, k_major, k_major, va_major, vf_major,
                   jax.ShapeDtypeStruct((b, s, MEM_WIDTH), BF16)],
        scratch_shapes=[pltpu.VMEM((SUBLANES, LANES), F32)],
        compiler_params=_params("parallel", "arbitrary"),
        name="in_projection",
    )(x, *consts, kpos)


def _memkv_kernel(mem_ref, nm_ref, w_ref, gk_ref, g128_ref, km_ref, vm_ref):
    mh = _rms_rows(mem_ref[...], nm_ref[...]).astype(BF16)
    for s in range(2):
        sl = slice(s * MXU_DIM, (s + 1) * MXU_DIM)
        y = _dot(mh, w_ref[:, sl])
        ss = _dot((y * y).astype(BF16), g128_ref[...])
        km_ref[:, sl] = (y * lax.rsqrt(ss * (1.0 / MEM_DIM) + EPS) * gk_ref[:, sl]).astype(BF16)
        vm_ref[:, sl] = _dot(mh, w_ref[:, MEM_WIDTH + s * MXU_DIM:MEM_WIDTH + (s + 1) * MXU_DIM]).astype(BF16)


def _memory_kv(mem, nm, w, gk, g128):
    b, n, d = mem.shape
    out = jax.ShapeDtypeStruct((b, n, MEM_WIDTH), BF16)
    spec = pl.BlockSpec((None, n, MEM_WIDTH), lambda bi: (bi, 0, 0))
    return pl.pallas_call(
        _memkv_kernel,
        grid=(b,),
        in_specs=[pl.BlockSpec((None, n, d), lambda bi: (bi, 0, 0)),
                  _const_spec(nm.shape), _const_spec(w.shape), _const_spec(gk.shape), _const_spec(g128.shape)],
        out_specs=[spec, spec],
        out_shape=[out, out],
        compiler_params=_params("parallel"),
        name="memory_kv",
    )(mem, nm, w, gk, g128)


def _attend(q_ref, k_ref, v_block, corr_ref, acc_ref, z_ref, finish):
    nb = q_ref.shape[1]
    stages = [(i, j) for i in range(nb) for j in range(i + 1)]

    def scores(slot, i, j):
        zmax = []
        for c in range(2):
            z = _dot(k_ref[c, j * TB:(j + 1) * TB, :], q_ref[c, i])
            if i == j:
                z = z + corr_ref[...]
            z_ref[slot, c] = z
            zmax.append(jnp.max(z, axis=0, keepdims=True))
        return zmax

    ahead = z_ref.shape[0] - 1
    pending = [scores(t, *stages[t]) for t in range(ahead)]
    m = [None, None]
    for t, (i, j) in enumerate(stages):
        if t + ahead < len(stages):
            pending.append(scores((t + ahead) % (ahead + 1), *stages[t + ahead]))
        zmax = pending.pop(0)
        slot = t % (ahead + 1)
        for c in range(2):
            if j == 0:
                m[c] = zmax[c]
                acc_ref[c] = _dot(v_block(j, c), jnp.exp2(z_ref[slot, c] - m[c]).astype(BF16))
            else:
                m_new = jnp.maximum(m[c], zmax[c])
                alpha = jnp.exp2(m[c] - m_new)
                m[c] = m_new
                acc_ref[c] = alpha * acc_ref[c] + _dot(v_block(j, c), jnp.exp2(z_ref[slot, c] - m_new).astype(BF16))
        if j == i:
            finish(i)


def _attention_specs(s, v_rows):
    nb = s // TB
    q_spec = pl.BlockSpec((None, None, 2, nb, LANES, TB), lambda bi, h: (bi, h, 0, 0, 0, 0))
    k_spec = pl.BlockSpec((None, None, 2, s, LANES), lambda bi, h: (bi, h, 0, 0, 0))
    v_spec = pl.BlockSpec((None, None, nb, v_rows, TB), lambda bi, h: (bi, h, 0, 0, 0))
    o_spec = pl.BlockSpec((None, None, s, LANES), lambda bi, h: (bi, h, 0, 0))
    return q_spec, k_spec, v_spec, o_spec


def _diff_kernel(slope_ref, qt_ref, k_ref, vt_ref, lq1_ref, lk1_ref, lq2_ref, lk2_ref, sub_ref, o_ref,
                 acc_ref, corr_ref, z_ref, *, lam_init):
    slope = slope_ref[pl.program_id(1)] * LOG2E
    kk = lax.broadcasted_iota(jnp.int32, (TB, TB), 0)
    qq = lax.broadcasted_iota(jnp.int32, (TB, TB), 1)
    after = jnp.where(kk > qq, (2.0 * slope) * (qq - kk).astype(F32), 0.0)
    corr_ref[...] = jnp.where((kk // CHUNK) <= (qq // CHUNK), after, NEG)

    lam = (jnp.exp(jnp.sum(lq1_ref[...] * lk1_ref[...], axis=1, keepdims=True))
           - jnp.exp(jnp.sum(lq2_ref[...] * lk2_ref[...], axis=1, keepdims=True)) + lam_init)
    gain = _lane_tile(sub_ref[...], TB // LANES) * (1.0 - lam_init)

    def finish(i):
        o0 = acc_ref[0, :DA_V_DIM, :] * (1.0 / acc_ref[0, DA_V_DIM:DA_V_DIM + 1, :])
        o1 = acc_ref[1, :DA_V_DIM, :] * (1.0 / acc_ref[1, DA_V_DIM:DA_V_DIM + 1, :])
        o = o0 - lam * o1
        ms = jnp.mean(o * o, axis=0, keepdims=True)
        o_ref[i * TB:(i + 1) * TB, :] = (o * lax.rsqrt(ms + EPS) * gain).T.astype(BF16)

    _attend(qt_ref, k_ref, lambda j, c: vt_ref[j], corr_ref, acc_ref, z_ref, finish)


def _diff_attention(qt, k, vt, slopes, lq1, lk1, lq2, lk2, sub, lam_init):
    b, nh, _, nb, _, _ = qt.shape
    s = nb * TB
    rows = vt.shape[3]
    q_spec, k_spec, v_spec, o_spec = _attention_specs(s, rows)
    vec_spec = _const_spec(lq1.shape)
    return pl.pallas_call(
        functools.partial(_diff_kernel, lam_init=lam_init),
        grid=(b, nh),
        in_specs=[pl.BlockSpec(memory_space=pltpu.SMEM), q_spec, k_spec, v_spec,
                  vec_spec, vec_spec, vec_spec, vec_spec, _const_spec(sub.shape)],
        out_specs=o_spec,
        out_shape=jax.ShapeDtypeStruct((b, nh, s, LANES), BF16),
        scratch_shapes=[pltpu.VMEM((2, rows, TB), F32), pltpu.VMEM((TB, TB), F32), pltpu.VMEM((Z_SLOTS, 2, TB, TB), F32)],
        compiler_params=_params("parallel", "parallel"),
        name="diff_attention",
    )(slopes, qt, k, vt, lq1, lk1, lq2, lk2, sub)


def _fox_kernel(qt_ref, k_ref, vt_ref, o_ref, acc_ref, corr_ref, z_ref):
    kk = lax.broadcasted_iota(jnp.int32, (TB, TB), 0)
    qq = lax.broadcasted_iota(jnp.int32, (TB, TB), 1)
    corr_ref[...] = jnp.where(kk <= qq, 0.0, NEG)
    rows = acc_ref.shape[1]

    def finish(i):
        heads = [acc_ref[hh, :FX_DIM, :] * (1.0 / acc_ref[hh, FX_DIM:FX_DIM + 1, :]) for hh in range(2)]
        o_ref[i * TB:(i + 1) * TB, :] = jnp.concatenate(heads, axis=0).T.astype(BF16)

    _attend(qt_ref, k_ref, lambda j, hh: vt_ref[j, hh * rows:(hh + 1) * rows, :], corr_ref, acc_ref, z_ref, finish)


def _fox_attention(qt, k, vt):
    b, npair, _, nb, _, _ = qt.shape
    s = nb * TB
    rows = vt.shape[3] // 2
    q_spec, k_spec, v_spec, o_spec = _attention_specs(s, 2 * rows)
    return pl.pallas_call(
        _fox_kernel,
        grid=(b, npair),
        in_specs=[q_spec, k_spec, v_spec],
        out_specs=o_spec,
        out_shape=jax.ShapeDtypeStruct((b, npair, s, LANES), BF16),
        scratch_shapes=[pltpu.VMEM((2, rows, TB), F32), pltpu.VMEM((TB, TB), F32), pltpu.VMEM((Z_SLOTS, 2, TB, TB), F32)],
        compiler_params=_params("parallel", "parallel"),
        name="fox_attention",
    )(qt, k, vt)


def _merge_kernel(x_ref, nm_ref, oa_ref, of_ref, qm_ref, km_ref, vm_ref, wg_ref, bg_ref,
                  wda_ref, wfx_ref, wmem_ref, wout_ref, o_ref, om_ref, merged_ref):
    x = x_ref[...]
    h = _rms_rows(x, nm_ref[...]).astype(BF16)

    for hd in range(MEM_HEADS):
        sl = slice(hd * MEM_DIM, (hd + 1) * MEM_DIM)
        s = _dot_nt(qm_ref[:, sl], km_ref[:, sl])
        p = jnp.exp2(s - jnp.max(s, axis=1, keepdims=True))
        inv = 1.0 / jnp.sum(p, axis=1, keepdims=True)
        om_ref[:, sl] = (_dot(p.astype(BF16), vm_ref[:, sl]) * inv).astype(BF16)

    oa = jnp.concatenate([oa_ref[hd] for hd in range(oa_ref.shape[0])], axis=1)
    of = jnp.concatenate([of_ref[hd] for hd in range(of_ref.shape[0])], axis=1)
    branches = ((oa, wda_ref), (of, wfx_ref), (om_ref[...], wmem_ref))
    for n in range(D_MODEL // MXU_DIM):
        sl = slice(n * MXU_DIM, (n + 1) * MXU_DIM)
        merged = None
        for br, (o_br, w_br) in enumerate(branches):
            gsl = slice(br * D_MODEL + n * MXU_DIM, br * D_MODEL + (n + 1) * MXU_DIM)
            gate = jax.nn.sigmoid(_dot(h, wg_ref[:, gsl]) + bg_ref[:, gsl])
            term = gate * _dot(o_br, w_br[:, sl])
            merged = term if merged is None else merged + term
        merged_ref[:, sl] = merged.astype(BF16)

    o_ref[...] = x + _dot(merged_ref[...], wout_ref[...])


def _merge(x, nm, oa, of, qm, km, vm, wg, bg, wda, wfx, wmem, wout):
    b, s, d = x.shape
    tm = TM_PROJ
    row = lambda width: pl.BlockSpec((None, tm, width), lambda bi, i: (bi, i, 0))
    heads = pl.BlockSpec((None, oa.shape[1], tm, LANES), lambda bi, i: (bi, 0, i, 0))
    mem_spec = pl.BlockSpec((None, N_MEM, MEM_WIDTH), lambda bi, i: (bi, 0, 0))
    return pl.pallas_call(
        _merge_kernel,
        grid=(b, s // tm),
        in_specs=[row(d), _const_spec(nm.shape), heads, heads, row(MEM_WIDTH), mem_spec, mem_spec,
                  _const_spec(wg.shape), _const_spec(bg.shape), _const_spec(wda.shape), _const_spec(wfx.shape),
                  _const_spec(wmem.shape), _const_spec(wout.shape)],
        out_specs=row(d),
        out_shape=jax.ShapeDtypeStruct((b, s, d), F32),
        scratch_shapes=[pltpu.VMEM((tm, MEM_WIDTH), BF16), pltpu.VMEM((tm, d), BF16)],
        compiler_params=_params("parallel", "parallel"),
        name="merge",
    )(x, nm, oa, of, qm, km, vm, wg, bg, wda, wfx, wmem, wout)


def _ffn_kernel(x_ref, nf_ref, wup_ref, cw_ref, cb_ref, wdn_ref, o_ref, carry_ref, act_ref):
    tm = x_ref.shape[0]

    @pl.when(pl.program_id(1) == 0)
    def _():
        carry_ref[...] = jnp.zeros_like(carry_ref)

    x = x_ref[...]
    h2 = _rms_rows(x, nf_ref[...]).astype(BF16)
    row = lax.broadcasted_iota(jnp.int32, (SUBLANES, FC), 0)

    def conv(col):
        sl = slice(col, col + FC)
        u = _dot(h2, wup_ref[:, sl])
        prev = carry_ref[:, sl]
        carry_ref[:, sl] = u[tm - SUBLANES:tm, :]
        out = u * cw_ref[CONV_W - 1:CONV_W, sl] + cb_ref[:, sl]
        for shift in range(1, CONV_W):
            us = pltpu.roll(u, shift, 0)
            head = jnp.where(row < shift, pltpu.roll(prev, shift, 0), us[0:SUBLANES, :])
            us = jnp.concatenate([head, us[SUBLANES:, :]], axis=0)
            out = out + us * cw_ref[CONV_W - 1 - shift:CONV_W - shift, sl]
        return out

    for ch in range(D_FF // FC):
        a = conv(ch * FC)
        g = conv(D_FF + ch * FC)
        act_ref[:, ch * FC:(ch + 1) * FC] = (a * jax.nn.sigmoid(a) * g).astype(BF16)

    o_ref[...] = x + _dot(act_ref[...], wdn_ref[...])


def _ffn(x, nf, wup, cw, cb, wdn):
    b, s, d = x.shape
    tm = TM_FFN
    row = pl.BlockSpec((None, tm, d), lambda bi, i: (bi, i, 0))
    return pl.pallas_call(
        _ffn_kernel,
        grid=(b, s // tm),
        in_specs=[row, _const_spec(nf.shape), _const_spec(wup.shape), _const_spec(cw.shape), _const_spec(cb.shape),
                  _const_spec(wdn.shape)],
        out_specs=row,
        out_shape=jax.ShapeDtypeStruct((b, s, d), F32),
        scratch_shapes=[pltpu.VMEM((SUBLANES, 2 * D_FF), F32), pltpu.VMEM((tm, D_FF), BF16)],
        compiler_params=_params("parallel", "arbitrary"),
        name="conv_mlp",
    )(x, nf, wup, cw, cb, wdn)


def _block_diag_ones(group):
    idx = np.arange(MXU_DIM) // group
    return jnp.asarray(idx[:, None] == idx[None, :], dtype=BF16)


def _alibi_key_table(s):
    slopes = np.asarray([2.0 ** (-8.0 * (i + 1) / DA_HEADS) * LOG2E for i in range(DA_HEADS)], np.float32)
    rest = np.arange(s, dtype=np.float32)[None, :] * slopes[:, None]
    table = np.zeros((DA_HEADS, 2, s, LANES), BF16)
    for p in range(N_SPLIT):
        piece = rest.astype(BF16)
        rest = rest - piece.astype(np.float32)
        table[:, 0, :, HALF + p] = piece
        table[:, 1, :, p] = piece
    return table


def _gate_lane_layout():
    head = np.full((LANES,), -1, np.int32)
    part = np.full((LANES,), N_SPLIT, np.int32)
    for a in range(FX_HEADS // 2):
        for p in range(N_SPLIT):
            head[HALF + N_SPLIT * a + p] = 2 * a
            head[N_SPLIT * a + p] = 2 * a + 1
            part[HALF + N_SPLIT * a + p] = p
            part[N_SPLIT * a + p] = p
    return head, part


def kernel(x, mem, norm_mix, w_in, b_gate, da_q_norm, da_k_norm, da_lambda_q1, da_lambda_k1, da_lambda_q2,
           da_lambda_k2, da_subln, fx_q_norm, fx_k_norm, fx_f_bias, mem_norm, w_mem_kv, mem_q_norm, mem_k_norm,
           w_branch_da, w_branch_fx, w_branch_mem, w_out, norm_ffn, w_up, conv_w, conv_b, w_down):
    b, s, d = x.shape
    depth = w_in.shape[0]
    off = np.cumsum(np.array(IN_SIZES))[:-1].tolist()
    slopes = jnp.asarray([2.0 ** (-8.0 * (i + 1) / DA_HEADS) for i in range(DA_HEADS)], dtype=F32)
    g64 = _block_diag_ones(DA_QK_DIM)
    g128 = _block_diag_ones(MEM_DIM)
    tri = jnp.asarray(np.tril(np.ones((MXU_DIM, MXU_DIM))), dtype=BF16)
    lane_head, lane_part = _gate_lane_layout()
    lane_used = jnp.asarray(lane_head >= 0)
    lane_src = jnp.asarray(np.maximum(lane_head, 0))
    part = jnp.asarray(lane_part).reshape(1, LANES)
    kpos = jnp.asarray(_alibi_key_table(s))
    pat_np = np.zeros((FX_HEADS // 2, HALF, TM_IN), np.float32)
    for a in range(FX_HEADS // 2):
        pat_np[a, N_SPLIT * a:N_SPLIT * (a + 1), :] = 1.0
    pat = jnp.asarray(pat_np, dtype=BF16)
    row = lambda v: v.reshape(1, -1).astype(F32)
    col = lambda v: jnp.broadcast_to(v.reshape(-1, 1).astype(F32), (v.size, LANES))

    for l in range(depth):
        lam_init = 0.8 - 0.6 * math.exp(-0.3 * l)
        a_q, a_k, a_v, f_q, f_k, f_v, f_g, m_q, w_g = jnp.split(w_in[l], off, axis=-1)
        wqt = jnp.concatenate([a_q, f_q], axis=1).T.astype(BF16)
        wvt = jnp.concatenate([a_v, f_v], axis=1).T.astype(BF16)
        wfg = jnp.where(lane_used[None, :], f_g[:, lane_src], 0.0)
        wtok = jnp.concatenate([a_k, f_k, m_q, wfg], axis=1).astype(BF16)
        fb = jnp.where(lane_used, fx_f_bias[l][lane_src], 0.0).reshape(1, LANES).astype(F32)
        gqt = col(jnp.concatenate([jnp.tile(da_q_norm[l], 2 * DA_HEADS) * (DA_QK_DIM ** -0.5 * LOG2E),
                                   jnp.tile(fx_q_norm[l], FX_HEADS) * (FX_DIM ** -0.5 * LOG2E)]))
        gtok = row(jnp.concatenate([jnp.tile(da_k_norm[l], 2 * DA_HEADS), jnp.tile(fx_k_norm[l], FX_HEADS),
                                    jnp.tile(mem_q_norm[l], MEM_HEADS) * (MEM_DIM ** -0.5 * LOG2E)]))
        gmk = row(jnp.tile(mem_k_norm[l], MEM_HEADS))

        qat, qft, ka, kf, vat, vft, qm = _in_projection(
            x, row(norm_mix[l]), wqt, wvt, wtok, gqt, gtok, fb, part, pat, g64, g128, tri, kpos)
        km, vm = _memory_kv(mem, row(mem_norm[l]), w_mem_kv[l].astype(BF16), gmk, g128)

        oa = _diff_attention(qat.reshape(b, DA_HEADS, 2, s // TB, LANES, TB), ka.reshape(b, DA_HEADS, 2, s, LANES),
                             vat, slopes, row(da_lambda_q1[l]), row(da_lambda_k1[l]), row(da_lambda_q2[l]),
                             row(da_lambda_k2[l]), col(da_subln[l]), lam_init)
        of = _fox_attention(qft.reshape(b, FX_HEADS // 2, 2, s // TB, LANES, TB),
                            kf.reshape(b, FX_HEADS // 2, 2, s, LANES), vft)

        x = _merge(x, row(norm_mix[l]), oa, of, qm, km, vm, w_g.astype(BF16), b_gate[l].reshape(1, -1).astype(F32),
                   w_branch_da[l].astype(BF16), w_branch_fx[l].astype(BF16), w_branch_mem[l].astype(BF16),
                   w_out[l].astype(BF16))
        x = _ffn(x, row(norm_ffn[l]), w_up[l].astype(BF16), conv_w[l].astype(F32), row(conv_b[l]),
                 w_down[l].astype(BF16))
    return x
```

```python
import functools
import math

import jax
import jax.numpy as jnp
import numpy as np
from jax import lax
from jax.experimental import pallas as pl
from jax.experimental.pallas import tpu as pltpu

D_MODEL = 1024
CHUNK = 64
N_MEM = 256
EPS = 1e-6

DA_HEADS = 4
DA_QK_DIM = 64
DA_V_DIM = 128
DA_WIDTH = 512
FX_HEADS = 8
FX_DIM = 64
FX_WIDTH = 512
MEM_HEADS = 4
MEM_DIM = 128
MEM_WIDTH = 512
N_BRANCH = 3
D_FF = 2816
CONV_W = 3

IN_SIZES = (512, 512, DA_WIDTH, FX_WIDTH, FX_WIDTH, FX_WIDTH, FX_HEADS, MEM_WIDTH, N_BRANCH * D_MODEL)

LANES = 128
SUBLANES = 8
MXU_DIM = 256
VMEM_LIMIT = 56 * 1024 * 1024

LOG2E = 1.4426950408889634
NEG = -1e30

TM_PROJ = 1024
TM_IN = 512
TM_FFN = 1024
TB = 256
FC = 256
N_SPLIT = 3
HALF = LANES // 2
ONES_ROWS = 16
Z_SLOTS = 3

F32 = jnp.float32
BF16 = jnp.bfloat16

_NT = (((1,), (1,)), ((), ()))


def _dot(a, b):
    return jnp.dot(a, b, preferred_element_type=F32)


def _dot_nt(a, b):
    return lax.dot_general(a, b, _NT, preferred_element_type=F32)


def _rms_rows(x, g):
    ms = jnp.mean(x * x, axis=-1, keepdims=True)
    return x * lax.rsqrt(ms + EPS) * g


def _three_way(a):
    hi = a.astype(BF16)
    r = a - hi.astype(F32)
    mid = r.astype(BF16)
    lo = (r - mid.astype(F32)).astype(BF16)
    return hi, mid, lo


def _lane_tile(x, reps):
    return jnp.concatenate([x] * reps, axis=1) if reps > 1 else x


def _const_spec(shape):
    nd = len(shape)
    return pl.BlockSpec(shape, lambda *_: (0,) * nd, pipeline_mode=pl.Buffered(1))


def _params(*sem):
    return pltpu.CompilerParams(dimension_semantics=sem, vmem_limit_bytes=VMEM_LIMIT)


def _inproj_kernel(x_ref, nm_ref, wqt_ref, wvt_ref, wtok_ref, gqt_ref, gtok_ref, fb_ref,
                   part_ref, pat_ref, g64_ref, g128_ref, tri_ref, kpos_ref,
                   qat_ref, qft_ref, ka_ref, kf_ref, vat_ref, vft_ref, qm_ref, carry_ref, ht_ref):
    tm = x_ref.shape[0]
    nb = tm // TB
    h = _rms_rows(x_ref[...], nm_ref[...]).astype(BF16)
    ht_ref[...] = h.T
    ht = ht_ref[...]
    lane = lax.broadcasted_iota(jnp.int32, (tm, LANES), 1)

    def store_blocks(out_ref, idx, yt):
        for jb in range(nb):
            out_ref[idx, jb] = yt[:, jb * TB:(jb + 1) * TB]

    def q_slab(s):
        rows = slice(s * MXU_DIM, (s + 1) * MXU_DIM)
        yt = _dot(wqt_ref[rows, :], ht)
        y3 = yt.reshape(MXU_DIM // DA_QK_DIM, DA_QK_DIM, tm)
        inv = lax.rsqrt(jnp.mean(y3 * y3, axis=1, keepdims=True) + EPS)
        yn = ((y3 * inv).reshape(MXU_DIM, tm) * _lane_tile(gqt_ref[rows, :], tm // LANES)).astype(BF16)
        out_ref = (qat_ref, qft_ref)[s // 2]
        for t in range(2):
            hd = 2 * (s % 2) + t
            ones_rows = pat_ref[hd * (s // 2)]
            store_blocks(out_ref, 2 * hd, jnp.concatenate([yn[t * LANES:t * LANES + HALF], ones_rows], axis=0))
            store_blocks(out_ref, 2 * hd + 1, jnp.concatenate([ones_rows, yn[t * LANES + HALF:(t + 1) * LANES]], axis=0))

    ones = jnp.ones((ONES_ROWS, tm), BF16)

    def v_slab(s):
        rows = slice(s * MXU_DIM, (s + 1) * MXU_DIM)
        vt = _dot(wvt_ref[rows, :], ht).astype(BF16)
        for t in range(2):
            hd = 2 * (s % 2) + t
            if s < 2:
                store_blocks(vat_ref, hd, jnp.concatenate([vt[t * LANES:(t + 1) * LANES], ones], axis=0))
            else:
                store_blocks(vft_ref, hd, jnp.concatenate(
                    [vt[t * LANES:t * LANES + HALF], ones, vt[t * LANES + HALF:(t + 1) * LANES], ones], axis=0))


    y = _dot(h, wtok_ref[...])

    def normed_pair(col, group_ref, inv_dim):
        cols = [slice(col + s * MXU_DIM, col + (s + 1) * MXU_DIM) for s in range(2)]
        ss = [_dot((y[:, sl] * y[:, sl]).astype(BF16), group_ref[...]) for sl in cols]
        return [(y[:, sl] * lax.rsqrt(q * inv_dim + EPS) * gtok_ref[:, sl]).astype(BF16) for sl, q in zip(cols, ss)]

    fg = y[:, 2 * DA_WIDTH + MEM_WIDTH:] + fb_ref[...]
    logf = jnp.minimum(fg, 0.0) - jnp.log(1.0 + jnp.exp(-jnp.abs(fg)))
    tri = tri_ref[...]
    pieces = _three_way(logf)
    local = [sum(_dot(tri, piece[r * MXU_DIM:(r + 1) * MXU_DIM]) for piece in pieces)
             for r in range(tm // MXU_DIM)]

    q_slab(0)
    for s, yn in enumerate(normed_pair(2 * DA_WIDTH, g128_ref, 1.0 / MEM_DIM)):
        qm_ref[:, s * MXU_DIM:(s + 1) * MXU_DIM] = yn
    v_slab(0)

    for s, yn in enumerate(normed_pair(0, g64_ref, 1.0 / DA_QK_DIM)):
        for t in range(2):
            hd = 2 * s + t
            kt = yn[:, t * LANES:(t + 1) * LANES]
            ka_ref[2 * hd] = jnp.where(lane < HALF, kt, kpos_ref[hd, 0])
            ka_ref[2 * hd + 1] = jnp.where(lane >= HALF, kt, kpos_ref[hd, 1])
        (q_slab, v_slab)[s](1)

    @pl.when(pl.program_id(1) == 0)
    def _():
        carry_ref[...] = jnp.zeros_like(carry_ref)

    run = carry_ref[0:1, :]
    blocks = []
    for blk in local:
        blocks.append(blk + run)
        run = blocks[-1][MXU_DIM - 1:MXU_DIM, :]
    c = jnp.concatenate(blocks, axis=0)
    carry_ref[...] = jnp.broadcast_to(run, carry_ref.shape)
    hi, mid, lo = _three_way(c * (-LOG2E))
    part = jnp.broadcast_to(part_ref[...], (tm, LANES))
    zero = jnp.zeros_like(hi)
    feat = jnp.where(part == 0, hi, jnp.where(part == 1, mid, jnp.where(part == 2, lo, zero)))
    q_slab(2)

    for s, yn in enumerate(normed_pair(DA_WIDTH, g64_ref, 1.0 / FX_DIM)):
        for t in range(2):
            a = 2 * s + t
            kt = yn[:, t * LANES:(t + 1) * LANES]
            fa = jnp.where(lane < HALF + N_SPLIT * a, zero, jnp.where(lane < HALF + N_SPLIT * (a + 1), feat, zero))
            fb = jnp.where(lane < N_SPLIT * a, zero, jnp.where(lane < N_SPLIT * (a + 1), feat, zero))
            kf_ref[2 * a] = jnp.where(lane < HALF, kt, fa)
            kf_ref[2 * a + 1] = jnp.where(lane >= HALF, kt, fb)
        (v_slab, q_slab)[s](2 + s)
    v_slab(3)


def _in_projection(x, nm, wqt, wvt, wtok, gqt, gtok, fb, part, pat, g64, g128, tri, kpos):
    b, s, d = x.shape
    tm = TM_IN

    def v_major(rows):
        return (jax.ShapeDtypeStruct((b, 4, s // TB, rows, TB), BF16),
                pl.BlockSpec((None, 4, tm // TB, rows, TB), lambda bi, i: (bi, 0, i, 0, 0)))

    va_major, va_spec = v_major(DA_V_DIM + ONES_ROWS)
    vf_major, vf_spec = v_major(2 * (FX_DIM + ONES_ROWS))
    q_major = jax.ShapeDtypeStruct((b, 8, s // TB, LANES, TB), BF16)
    q_spec = pl.BlockSpec((None, 8, tm // TB, LANES, TB), lambda bi, i: (bi, 0, i, 0, 0))
    k_major = jax.ShapeDtypeStruct((b, 8, s, LANES), BF16)
    k_spec = pl.BlockSpec((None, 8, tm, LANES), lambda bi, i: (bi, 0, i, 0))
    consts = (nm, wqt, wvt, wtok, gqt, gtok, fb, part, pat, g64, g128, tri)
    return pl.pallas_call(
        _inproj_kernel,
        grid=(b, s // tm),
        in_specs=[pl.BlockSpec((None, tm, d), lambda bi, i: (bi, i, 0))] + [_const_spec(c.shape) for c in consts]
        + [pl.BlockSpec((DA_HEADS, 2, tm, LANES), lambda bi, i: (0, 0, i, 0))],
        out_specs=[q_spec, q_spec, k_spec, k_spec, va_spec, vf_spec,
                   pl.BlockSpec((None, tm, MEM_WIDTH), lambda bi, i: (bi, i, 0))],
        out_shape=[q_major, q_major, k_major, k_major, va_major, vf_major,
                   jax.ShapeDtypeStruct((b, s, MEM_WIDTH), BF16)],
        scratch_shapes=[pltpu.VMEM((SUBLANES, LANES), F32), pltpu.VMEM((d, tm), BF16)],
        compiler_params=_params("parallel", "arbitrary"),
        name="in_projection",
    )(x, *consts, kpos)


def _memkv_kernel(mem_ref, nm_ref, w_ref, gk_ref, g128_ref, km_ref, vm_ref):
    mh = _rms_rows(mem_ref[...], nm_ref[...]).astype(BF16)
    for s in range(2):
        sl = slice(s * MXU_DIM, (s + 1) * MXU_DIM)
        y = _dot(mh, w_ref[:, sl])
        ss = _dot((y * y).astype(BF16), g128_ref[...])
        km_ref[:, sl] = (y * lax.rsqrt(ss * (1.0 / MEM_DIM) + EPS) * gk_ref[:, sl]).astype(BF16)
        vm_ref[:, sl] = _dot(mh, w_ref[:, MEM_WIDTH + s * MXU_DIM:MEM_WIDTH + (s + 1) * MXU_DIM]).astype(BF16)


def _memory_kv(mem, nm, w, gk, g128):
    b, n, d = mem.shape
    out = jax.ShapeDtypeStruct((b, n, MEM_WIDTH), BF16)
    spec = pl.BlockSpec((None, n, MEM_WIDTH), lambda bi: (bi, 0, 0))
    return pl.pallas_call(
        _memkv_kernel,
        grid=(b,),
        in_specs=[pl.BlockSpec((None, n, d), lambda bi: (bi, 0, 0)),
                  _const_spec(nm.shape), _const_spec(w.shape), _const_spec(gk.shape), _const_spec(g128.shape)],
        out_specs=[spec, spec],
        out_shape=[out, out],
        compiler_params=_params("parallel"),
        name="memory_kv",
    )(mem, nm, w, gk, g128)


def _attend(q_ref, k_ref, v_block, corr_ref, acc_ref, z_ref, finish):
    nb = q_ref.shape[1]
    stages = [(i, j) for i in range(nb) for j in range(i + 1)]

    def scores(slot, i, j):
        zmax = []
        for c in range(2):
            z = _dot(k_ref[c, j * TB:(j + 1) * TB, :], q_ref[c, i])
            if i == j:
                z = z + corr_ref[...]
            z_ref[slot, c] = z
            zmax.append(jnp.max(z, axis=0, keepdims=True))
        return zmax

    ahead = z_ref.shape[0] - 1
    pending = [scores(t, *stages[t]) for t in range(ahead)]
    m = [None, None]
    for t, (i, j) in enumerate(stages):
        if t + ahead < len(stages):
            pending.append(scores((t + ahead) % (ahead + 1), *stages[t + ahead]))
        zmax = pending.pop(0)
        slot = t % (ahead + 1)
        for c in range(2):
            if j == 0:
                m[c] = zmax[c]
                acc_ref[c] = _dot(v_block(j, c), jnp.exp2(z_ref[slot, c] - m[c]).astype(BF16))
            else:
                m_new = jnp.maximum(m[c], zmax[c])
                alpha = jnp.exp2(m[c] - m_new)
                m[c] = m_new
                acc_ref[c] = alpha * acc_ref[c] + _dot(v_block(j, c), jnp.exp2(z_ref[slot, c] - m_new).astype(BF16))
        if j == i:
            finish(i)


def _attention_specs(s, v_rows):
    nb = s // TB
    q_spec = pl.BlockSpec((None, None, 2, nb, LANES, TB), lambda bi, h: (bi, h, 0, 0, 0, 0))
    k_spec = pl.BlockSpec((None, None, 2, s, LANES), lambda bi, h: (bi, h, 0, 0, 0))
    v_spec = pl.BlockSpec((None, None, nb, v_rows, TB), lambda bi, h: (bi, h, 0, 0, 0))
    o_spec = pl.BlockSpec((None, None, s, LANES), lambda bi, h: (bi, h, 0, 0))
    return q_spec, k_spec, v_spec, o_spec


def _diff_kernel(slope_ref, qt_ref, k_ref, vt_ref, lq1_ref, lk1_ref, lq2_ref, lk2_ref, sub_ref, o_ref,
                 acc_ref, corr_ref, z_ref, *, lam_init):
    slope = slope_ref[pl.program_id(1)] * LOG2E
    kk = lax.broadcasted_iota(jnp.int32, (TB, TB), 0)
    qq = lax.broadcasted_iota(jnp.int32, (TB, TB), 1)
    after = jnp.where(kk > qq, (2.0 * slope) * (qq - kk).astype(F32), 0.0)
    corr_ref[...] = jnp.where((kk // CHUNK) <= (qq // CHUNK), after, NEG)

    lam = (jnp.exp(jnp.sum(lq1_ref[...] * lk1_ref[...], axis=1, keepdims=True))
           - jnp.exp(jnp.sum(lq2_ref[...] * lk2_ref[...], axis=1, keepdims=True)) + lam_init)
    gain = _lane_tile(sub_ref[...], TB // LANES) * (1.0 - lam_init)

    def finish(i):
        o0 = acc_ref[0, :DA_V_DIM, :] * (1.0 / acc_ref[0, DA_V_DIM:DA_V_DIM + 1, :])
        o1 = acc_ref[1, :DA_V_DIM, :] * (1.0 / acc_ref[1, DA_V_DIM:DA_V_DIM + 1, :])
        o = o0 - lam * o1
        ms = jnp.mean(o * o, axis=0, keepdims=True)
        o_ref[i * TB:(i + 1) * TB, :] = (o * lax.rsqrt(ms + EPS) * gain).T.astype(BF16)

    _attend(qt_ref, k_ref, lambda j, c: vt_ref[j], corr_ref, acc_ref, z_ref, finish)


def _diff_attention(qt, k, vt, slopes, lq1, lk1, lq2, lk2, sub, lam_init):
    b, nh, _, nb, _, _ = qt.shape
    s = nb * TB
    rows = vt.shape[3]
    q_spec, k_spec, v_spec, o_spec = _attention_specs(s, rows)
    vec_spec = _const_spec(lq1.shape)
    return pl.pallas_call(
        functools.partial(_diff_kernel, lam_init=lam_init),
        grid=(b, nh),
        in_specs=[pl.BlockSpec(memory_space=pltpu.SMEM), q_spec, k_spec, v_spec,
                  vec_spec, vec_spec, vec_spec, vec_spec, _const_spec(sub.shape)],
        out_specs=o_spec,
        out_shape=jax.ShapeDtypeStruct((b, nh, s, LANES), BF16),
        scratch_shapes=[pltpu.VMEM((2, rows, TB), F32), pltpu.VMEM((TB, TB), F32), pltpu.VMEM((Z_SLOTS, 2, TB, TB), F32)],
        compiler_params=_params("parallel", "parallel"),
        name="diff_attention",
    )(slopes, qt, k, vt, lq1, lk1, lq2, lk2, sub)


def _fox_kernel(qt_ref, k_ref, vt_ref, o_ref, acc_ref, corr_ref, z_ref):
    kk = lax.broadcasted_iota(jnp.int32, (TB, TB), 0)
    qq = lax.broadcasted_iota(jnp.int32, (TB, TB), 1)
    corr_ref[...] = jnp.where(kk <= qq, 0.0, NEG)
    rows = acc_ref.shape[1]

    def finish(i):
        heads = [acc_ref[hh, :FX_DIM, :] * (1.0 / acc_ref[hh, FX_DIM:FX_DIM + 1, :]) for hh in range(2)]
        o_ref[i * TB:(i + 1) * TB, :] = jnp.concatenate(heads, axis=0).T.astype(BF16)

    _attend(qt_ref, k_ref, lambda j, hh: vt_ref[j, hh * rows:(hh + 1) * rows, :], corr_ref, acc_ref, z_ref, finish)


def _fox_attention(qt, k, vt):
    b, npair, _, nb, _, _ = qt.shape
    s = nb * TB
    rows = vt.shape[3] // 2
    q_spec, k_spec, v_spec, o_spec = _attention_specs(s, 2 * rows)
    return pl.pallas_call(
        _fox_kernel,
        grid=(b, npair),
        in_specs=[q_spec, k_spec, v_spec],
        out_specs=o_spec,
        out_shape=jax.ShapeDtypeStruct((b, npair, s, LANES), BF16),
        scratch_shapes=[pltpu.VMEM((2, rows, TB), F32), pltpu.VMEM((TB, TB), F32), pltpu.VMEM((Z_SLOTS, 2, TB, TB), F32)],
        compiler_params=_params("parallel", "parallel"),
        name="fox_attention",
    )(qt, k, vt)


def _merge_kernel(x_ref, nm_ref, oa_ref, of_ref, qm_ref, km_ref, vm_ref, wg_ref, bg_ref,
                  wda_ref, wfx_ref, wmem_ref, wout_ref, o_ref, om_ref, merged_ref):
    x = x_ref[...]
    h = _rms_rows(x, nm_ref[...]).astype(BF16)

    for hd in range(MEM_HEADS):
        sl = slice(hd * MEM_DIM, (hd + 1) * MEM_DIM)
        s = _dot_nt(qm_ref[:, sl], km_ref[:, sl])
        p = jnp.exp2(s - jnp.max(s, axis=1, keepdims=True))
        inv = 1.0 / jnp.sum(p, axis=1, keepdims=True)
        om_ref[:, sl] = (_dot(p.astype(BF16), vm_ref[:, sl]) * inv).astype(BF16)

    oa = jnp.concatenate([oa_ref[hd] for hd in range(oa_ref.shape[0])], axis=1)
    of = jnp.concatenate([of_ref[hd] for hd in range(of_ref.shape[0])], axis=1)
    branches = ((oa, wda_ref), (of, wfx_ref), (om_ref[...], wmem_ref))
    for n in range(D_MODEL // MXU_DIM):
        sl = slice(n * MXU_DIM, (n + 1) * MXU_DIM)
        merged = None
        for br, (o_br, w_br) in enumerate(branches):
            gsl = slice(br * D_MODEL + n * MXU_DIM, br * D_MODEL + (n + 1) * MXU_DIM)
            gate = jax.nn.sigmoid(_dot(h, wg_ref[:, gsl]) + bg_ref[:, gsl])
            term = gate * _dot(o_br, w_br[:, sl])
            merged = term if merged is None else merged + term
        merged_ref[:, sl] = merged.astype(BF16)

    o_ref[...] = x + _dot(merged_ref[...], wout_ref[...])


def _merge(x, nm, oa, of, qm, km, vm, wg, bg, wda, wfx, wmem, wout):
    b, s, d = x.shape
    tm = TM_PROJ
    row = lambda width: pl.BlockSpec((None, tm, width), lambda bi, i: (bi, i, 0))
    heads = pl.BlockSpec((None, oa.shape[1], tm, LANES), lambda bi, i: (bi, 0, i, 0))
    mem_spec = pl.BlockSpec((None, N_MEM, MEM_WIDTH), lambda bi, i: (bi, 0, 0))
    return pl.pallas_call(
        _merge_kernel,
        grid=(b, s // tm),
        in_specs=[row(d), _const_spec(nm.shape), heads, heads, row(MEM_WIDTH), mem_spec, mem_spec,
                  _const_spec(wg.shape), _const_spec(bg.shape), _const_spec(wda.shape), _const_spec(wfx.shape),
                  _const_spec(wmem.shape), _const_spec(wout.shape)],
        out_specs=row(d),
        out_shape=jax.ShapeDtypeStruct((b, s, d), F32),
        scratch_shapes=[pltpu.VMEM((tm, MEM_WIDTH), BF16), pltpu.VMEM((tm, d), BF16)],
        compiler_params=_params("parallel", "parallel"),
        name="merge",
    )(x, nm, oa, of, qm, km, vm, wg, bg, wda, wfx, wmem, wout)


def _ffn_kernel(x_ref, nf_ref, wup_ref, cw_ref, cb_ref, wdn_ref, o_ref, carry_ref, act_ref):
    tm = x_ref.shape[0]

    @pl.when(pl.program_id(1) == 0)
    def _():
        carry_ref[...] = jnp.zeros_like(carry_ref)

    x = x_ref[...]
    h2 = _rms_rows(x, nf_ref[...]).astype(BF16)
    row = lax.broadcasted_iota(jnp.int32, (SUBLANES, FC), 0)

    def conv(col):
        sl = slice(col, col + FC)
        u = _dot(h2, wup_ref[:, sl])
        prev = carry_ref[:, sl]
        carry_ref[:, sl] = u[tm - SUBLANES:tm, :]
        out = u * cw_ref[CONV_W - 1:CONV_W, sl] + cb_ref[:, sl]
        for shift in range(1, CONV_W):
            us = pltpu.roll(u, shift, 0)
            head = jnp.where(row < shift, pltpu.roll(prev, shift, 0), us[0:SUBLANES, :])
            us = jnp.concatenate([head, us[SUBLANES:, :]], axis=0)
            out = out + us * cw_ref[CONV_W - 1 - shift:CONV_W - shift, sl]
        return out

    for ch in range(D_FF // FC):
        a = conv(ch * FC)
        g = conv(D_FF + ch * FC)
        act_ref[:, ch * FC:(ch + 1) * FC] = (a * jax.nn.sigmoid(a) * g).astype(BF16)

    o_ref[...] = x + _dot(act_ref[...], wdn_ref[...])


def _ffn(x, nf, wup, cw, cb, wdn):
    b, s, d = x.shape
    tm = TM_FFN
    row = pl.BlockSpec((None, tm, d), lambda bi, i: (bi, i, 0))
    return pl.pallas_call(
        _ffn_kernel,
        grid=(b, s // tm),
        in_specs=[row, _const_spec(nf.shape), _const_spec(wup.shape), _const_spec(cw.shape), _const_spec(cb.shape),
                  _const_spec(wdn.shape)],
        out_specs=row,
        out_shape=jax.ShapeDtypeStruct((b, s, d), F32),
        scratch_shapes=[pltpu.VMEM((SUBLANES, 2 * D_FF), F32), pltpu.VMEM((tm, D_FF), BF16)],
        compiler_params=_params("parallel", "arbitrary"),
        name="conv_mlp",
    )(x, nf, wup, cw, cb, wdn)


def _block_diag_ones(group):
    idx = np.arange(MXU_DIM) // group
    return jnp.asarray(idx[:, None] == idx[None, :], dtype=BF16)


def _alibi_key_table(s):
    slopes = np.asarray([2.0 ** (-8.0 * (i + 1) / DA_HEADS) * LOG2E for i in range(DA_HEADS)], np.float32)
    rest = np.arange(s, dtype=np.float32)[None, :] * slopes[:, None]
    table = np.zeros((DA_HEADS, 2, s, LANES), BF16)
    for p in range(N_SPLIT):
        piece = rest.astype(BF16)
        rest = rest - piece.astype(np.float32)
        table[:, 0, :, HALF + p] = piece
        table[:, 1, :, p] = piece
    return table


def _gate_lane_layout():
    head = np.full((LANES,), -1, np.int32)
    part = np.full((LANES,), N_SPLIT, np.int32)
    for a in range(FX_HEADS // 2):
        for p in range(N_SPLIT):
            head[HALF + N_SPLIT * a + p] = 2 * a
            head[N_SPLIT * a + p] = 2 * a + 1
            part[HALF + N_SPLIT * a + p] = p
            part[N_SPLIT * a + p] = p
    return head, part


def kernel(x, mem, norm_mix, w_in, b_gate, da_q_norm, da_k_norm, da_lambda_q1, da_lambda_k1, da_lambda_q2,
           da_lambda_k2, da_subln, fx_q_norm, fx_k_norm, fx_f_bias, mem_norm, w_mem_kv, mem_q_norm, mem_k_norm,
           w_branch_da, w_branch_fx, w_branch_mem, w_out, norm_ffn, w_up, conv_w, conv_b, w_down):
    b, s, d = x.shape
    depth = w_in.shape[0]
    off = np.cumsum(np.array(IN_SIZES))[:-1].tolist()
    slopes = jnp.asarray([2.0 ** (-8.0 * (i + 1) / DA_HEADS) for i in range(DA_HEADS)], dtype=F32)
    g64 = _block_diag_ones(DA_QK_DIM)
    g128 = _block_diag_ones(MEM_DIM)
    tri = jnp.asarray(np.tril(np.ones((MXU_DIM, MXU_DIM))), dtype=BF16)
    lane_head, lane_part = _gate_lane_layout()
    lane_used = jnp.asarray(lane_head >= 0)
    lane_src = jnp.asarray(np.maximum(lane_head, 0))
    part = jnp.asarray(lane_part).reshape(1, LANES)
    kpos = jnp.asarray(_alibi_key_table(s))
    pat_np = np.zeros((FX_HEADS // 2, HALF, TM_IN), np.float32)
    for a in range(FX_HEADS // 2):
        pat_np[a, N_SPLIT * a:N_SPLIT * (a + 1), :] = 1.0
    pat = jnp.asarray(pat_np, dtype=BF16)
    row = lambda v: v.reshape(1, -1).astype(F32)
    col = lambda v: jnp.broadcast_to(v.reshape(-1, 1).astype(F32), (v.size, LANES))

    for l in range(depth):
        lam_init = 0.8 - 0.6 * math.exp(-0.3 * l)
        a_q, a_k, a_v, f_q, f_k, f_v, f_g, m_q, w_g = jnp.split(w_in[l], off, axis=-1)
        wqt = jnp.concatenate([a_q, f_q], axis=1).T.astype(BF16)
        wvt = jnp.concatenate([a_v, f_v], axis=1).T.astype(BF16)
        wfg = jnp.where(lane_used[None, :], f_g[:, lane_src], 0.0)
        wtok = jnp.concatenate([a_k, f_k, m_q, wfg], axis=1).astype(BF16)
        fb = jnp.where(lane_used, fx_f_bias[l][lane_src], 0.0).reshape(1, LANES).astype(F32)
        gqt = col(jnp.concatenate([jnp.tile(da_q_norm[l], 2 * DA_HEADS) * (DA_QK_DIM ** -0.5 * LOG2E),
                                   jnp.tile(fx_q_norm[l], FX_HEADS) * (FX_DIM ** -0.5 * LOG2E)]))
        gtok = row(jnp.concatenate([jnp.tile(da_k_norm[l], 2 * DA_HEADS), jnp.tile(fx_k_norm[l], FX_HEADS),
                                    jnp.tile(mem_q_norm[l], MEM_HEADS) * (MEM_DIM ** -0.5 * LOG2E)]))
        gmk = row(jnp.tile(mem_k_norm[l], MEM_HEADS))

        qat, qft, ka, kf, vat, vft, qm = _in_projection(
            x, row(norm_mix[l]), wqt, wvt, wtok, gqt, gtok, fb, part, pat, g64, g128, tri, kpos)
        km, vm = _memory_kv(mem, row(mem_norm[l]), w_mem_kv[l].astype(BF16), gmk, g128)

        oa = _diff_attention(qat.reshape(b, DA_HEADS, 2, s // TB, LANES, TB), ka.reshape(b, DA_HEADS, 2, s, LANES),
                             vat, slopes, row(da_lambda_q1[l]), row(da_lambda_k1[l]), row(da_lambda_q2[l]),
                             row(da_lambda_k2[l]), col(da_subln[l]), lam_init)
        of = _fox_attention(qft.reshape(b, FX_HEADS // 2, 2, s // TB, LANES, TB),
                            kf.reshape(b, FX_HEADS // 2, 2, s, LANES), vft)

        x = _merge(x, row(norm_mix[l]), oa, of, qm, km, vm, w_g.astype(BF16), b_gate[l].reshape(1, -1).astype(F32),
                   w_branch_da[l].astype(BF16), w_branch_fx[l].astype(BF16), w_branch_mem[l].astype(BF16),
                   w_out[l].astype(BF16))
        x = _ffn(x, row(norm_ffn[l]), w_up[l].astype(BF16), conv_w[l].astype(F32), row(conv_b[l]),
                 w_down[l].astype(BF16))
    return x
```

```python
import functools
import math

import jax
import jax.numpy as jnp
import numpy as np
from jax import lax
from jax.experimental import pallas as pl
from jax.experimental.pallas import tpu as pltpu

D_MODEL = 1024
CHUNK = 64
N_MEM = 256
EPS = 1e-6

DA_HEADS = 4
DA_QK_DIM = 64
DA_V_DIM = 128
DA_WIDTH = 512
FX_HEADS = 8
FX_DIM = 64
FX_WIDTH = 512
MEM_HEADS = 4
MEM_DIM = 128
MEM_WIDTH = 512
N_BRANCH = 3
D_FF = 2816
CONV_W = 3

IN_SIZES = (512, 512, DA_WIDTH, FX_WIDTH, FX_WIDTH, FX_WIDTH, FX_HEADS, MEM_WIDTH, N_BRANCH * D_MODEL)

LANES = 128
SUBLANES = 8
MXU_DIM = 256
VMEM_LIMIT = 56 * 1024 * 1024

LOG2E = 1.4426950408889634
NEG = -1e30

TM_PROJ = 1024
TM_IN = 512
TM_FFN = 1024
TB = 256
FC = 256
N_SPLIT = 3
HALF = LANES // 2
ONES_ROWS = 16
Z_SLOTS = 3
HEADS_PER_STEP = 4

F32 = jnp.float32
BF16 = jnp.bfloat16

_NT = (((1,), (1,)), ((), ()))


def _dot(a, b):
    return jnp.dot(a, b, preferred_element_type=F32)


def _dot_nt(a, b):
    return lax.dot_general(a, b, _NT, preferred_element_type=F32)


def _rms_rows(x, g):
    ms = jnp.mean(x * x, axis=-1, keepdims=True)
    return x * lax.rsqrt(ms + EPS) * g


def _three_way(a):
    hi = a.astype(BF16)
    r = a - hi.astype(F32)
    mid = r.astype(BF16)
    lo = (r - mid.astype(F32)).astype(BF16)
    return hi, mid, lo


def _lane_tile(x, reps):
    return jnp.concatenate([x] * reps, axis=1) if reps > 1 else x


def _const_spec(shape):
    nd = len(shape)
    return pl.BlockSpec(shape, lambda *_: (0,) * nd, pipeline_mode=pl.Buffered(1))


def _params(*sem):
    return pltpu.CompilerParams(dimension_semantics=sem, vmem_limit_bytes=VMEM_LIMIT)


def _inproj_kernel(x_ref, nm_ref, wqt_ref, wvt_ref, wtok_ref, gqt_ref, gtok_ref, fb_ref,
                   part_ref, pat_ref, g64_ref, g128_ref, tri_ref, kpos_ref,
                   qat_ref, qft_ref, ka_ref, kf_ref, vat_ref, vft_ref, qm_ref, carry_ref, ht_ref):
    tm = x_ref.shape[0]
    nb = tm // TB
    h = _rms_rows(x_ref[...], nm_ref[...]).astype(BF16)
    ht_ref[...] = h.T
    ht = ht_ref[...]
    lane = lax.broadcasted_iota(jnp.int32, (tm, LANES), 1)

    def store_blocks(out_ref, idx, yt):
        for jb in range(nb):
            out_ref[idx, jb] = yt[:, jb * TB:(jb + 1) * TB]

    def q_slab(s):
        rows = slice(s * MXU_DIM, (s + 1) * MXU_DIM)
        yt = _dot(wqt_ref[rows, :], ht)
        y3 = yt.reshape(MXU_DIM // DA_QK_DIM, DA_QK_DIM, tm)
        inv = lax.rsqrt(jnp.mean(y3 * y3, axis=1, keepdims=True) + EPS)
        yn = ((y3 * inv).reshape(MXU_DIM, tm) * _lane_tile(gqt_ref[rows, :], tm // LANES)).astype(BF16)
        out_ref = (qat_ref, qft_ref)[s // 2]
        for t in range(2):
            hd = 2 * (s % 2) + t
            ones_rows = pat_ref[hd * (s // 2)]
            store_blocks(out_ref, 2 * hd, jnp.concatenate([yn[t * LANES:t * LANES + HALF], ones_rows], axis=0))
            store_blocks(out_ref, 2 * hd + 1, jnp.concatenate([ones_rows, yn[t * LANES + HALF:(t + 1) * LANES]], axis=0))

    ones = jnp.ones((ONES_ROWS, tm), BF16)

    def v_slab(s):
        rows = slice(s * MXU_DIM, (s + 1) * MXU_DIM)
        vt = _dot(wvt_ref[rows, :], ht).astype(BF16)
        for t in range(2):
            hd = 2 * (s % 2) + t
            if s < 2:
                store_blocks(vat_ref, hd, jnp.concatenate([vt[t * LANES:(t + 1) * LANES], ones], axis=0))
            else:
                store_blocks(vft_ref, hd, jnp.concatenate(
                    [vt[t * LANES:t * LANES + HALF], ones, vt[t * LANES + HALF:(t + 1) * LANES], ones], axis=0))


    y = _dot(h, wtok_ref[...])

    def normed_pair(col, group_ref, inv_dim):
        cols = [slice(col + s * MXU_DIM, col + (s + 1) * MXU_DIM) for s in range(2)]
        ss = [_dot((y[:, sl] * y[:, sl]).astype(BF16), group_ref[...]) for sl in cols]
        return [(y[:, sl] * lax.rsqrt(q * inv_dim + EPS) * gtok_ref[:, sl]).astype(BF16) for sl, q in zip(cols, ss)]

    fg = y[:, 2 * DA_WIDTH + MEM_WIDTH:] + fb_ref[...]
    logf = jnp.minimum(fg, 0.0) - jnp.log(1.0 + jnp.exp(-jnp.abs(fg)))
    tri = tri_ref[...]
    pieces = _three_way(logf)
    local = [sum(_dot(tri, piece[r * MXU_DIM:(r + 1) * MXU_DIM]) for piece in pieces)
             for r in range(tm // MXU_DIM)]

    q_slab(0)
    for s, yn in enumerate(normed_pair(2 * DA_WIDTH, g128_ref, 1.0 / MEM_DIM)):
        qm_ref[:, s * MXU_DIM:(s + 1) * MXU_DIM] = yn
    v_slab(0)

    for s, yn in enumerate(normed_pair(0, g64_ref, 1.0 / DA_QK_DIM)):
        for t in range(2):
            hd = 2 * s + t
            kt = yn[:, t * LANES:(t + 1) * LANES]
            ka_ref[2 * hd] = jnp.where(lane < HALF, kt, kpos_ref[hd, 0])
            ka_ref[2 * hd + 1] = jnp.where(lane >= HALF, kt, kpos_ref[hd, 1])
        (q_slab, v_slab)[s](1)

    @pl.when(pl.program_id(1) == 0)
    def _():
        carry_ref[...] = jnp.zeros_like(carry_ref)

    run = carry_ref[0:1, :]
    blocks = []
    for blk in local:
        blocks.append(blk + run)
        run = blocks[-1][MXU_DIM - 1:MXU_DIM, :]
    c = jnp.concatenate(blocks, axis=0)
    carry_ref[...] = jnp.broadcast_to(run, carry_ref.shape)
    hi, mid, lo = _three_way(c * (-LOG2E))
    part = jnp.broadcast_to(part_ref[...], (tm, LANES))
    zero = jnp.zeros_like(hi)
    feat = jnp.where(part == 0, hi, jnp.where(part == 1, mid, jnp.where(part == 2, lo, zero)))
    q_slab(2)

    for s, yn in enumerate(normed_pair(DA_WIDTH, g64_ref, 1.0 / FX_DIM)):
        for t in range(2):
            a = 2 * s + t
            kt = yn[:, t * LANES:(t + 1) * LANES]
            fa = jnp.where(lane < HALF + N_SPLIT * a, zero, jnp.where(lane < HALF + N_SPLIT * (a + 1), feat, zero))
            fb = jnp.where(lane < N_SPLIT * a, zero, jnp.where(lane < N_SPLIT * (a + 1), feat, zero))
            kf_ref[2 * a] = jnp.where(lane < HALF, kt, fa)
            kf_ref[2 * a + 1] = jnp.where(lane >= HALF, kt, fb)
        (v_slab, q_slab)[s](2 + s)
    v_slab(3)


def _in_projection(x, nm, wqt, wvt, wtok, gqt, gtok, fb, part, pat, g64, g128, tri, kpos):
    b, s, d = x.shape
    tm = TM_IN

    def v_major(rows):
        return (jax.ShapeDtypeStruct((b, 4, s // TB, rows, TB), BF16),
                pl.BlockSpec((None, 4, tm // TB, rows, TB), lambda bi, i: (bi, 0, i, 0, 0)))

    va_major, va_spec = v_major(DA_V_DIM + ONES_ROWS)
    vf_major, vf_spec = v_major(2 * (FX_DIM + ONES_ROWS))
    q_major = jax.ShapeDtypeStruct((b, 8, s // TB, LANES, TB), BF16)
    q_spec = pl.BlockSpec((None, 8, tm // TB, LANES, TB), lambda bi, i: (bi, 0, i, 0, 0))
    k_major = jax.ShapeDtypeStruct((b, 8, s, LANES), BF16)
    k_spec = pl.BlockSpec((None, 8, tm, LANES), lambda bi, i: (bi, 0, i, 0))
    consts = (nm, wqt, wvt, wtok, gqt, gtok, fb, part, pat, g64, g128, tri)
    return pl.pallas_call(
        _inproj_kernel,
        grid=(b, s // tm),
        in_specs=[pl.BlockSpec((None, tm, d), lambda bi, i: (bi, i, 0))] + [_const_spec(c.shape) for c in consts]
        + [pl.BlockSpec((DA_HEADS, 2, tm, LANES), lambda bi, i: (0, 0, i, 0))],
        out_specs=[q_spec, q_spec, k_spec, k_spec, va_spec, vf_spec,
                   pl.BlockSpec((None, tm, MEM_WIDTH), lambda bi, i: (bi, i, 0))],
        out_shape=[q_major, q_major, k_major, k_major, va_major, vf_major,
                   jax.ShapeDtypeStruct((b, s, MEM_WIDTH), BF16)],
        scratch_shapes=[pltpu.VMEM((SUBLANES, LANES), F32), pltpu.VMEM((d, tm), BF16)],
        compiler_params=_params("parallel", "arbitrary"),
        name="in_projection",
    )(x, *consts, kpos)


def _memkv_kernel(mem_ref, nm_ref, w_ref, gk_ref, g128_ref, km_ref, vm_ref):
    mh = _rms_rows(mem_ref[...], nm_ref[...]).astype(BF16)
    for s in range(2):
        sl = slice(s * MXU_DIM, (s + 1) * MXU_DIM)
        y = _dot(mh, w_ref[:, sl])
        ss = _dot((y * y).astype(BF16), g128_ref[...])
        km_ref[:, sl] = (y * lax.rsqrt(ss * (1.0 / MEM_DIM) + EPS) * gk_ref[:, sl]).astype(BF16)
        vm_ref[:, sl] = _dot(mh, w_ref[:, MEM_WIDTH + s * MXU_DIM:MEM_WIDTH + (s + 1) * MXU_DIM]).astype(BF16)


def _memory_kv(mem, nm, w, gk, g128):
    b, n, d = mem.shape
    out = jax.ShapeDtypeStruct((b, n, MEM_WIDTH), BF16)
    spec = pl.BlockSpec((None, n, MEM_WIDTH), lambda bi: (bi, 0, 0))
    return pl.pallas_call(
        _memkv_kernel,
        grid=(b,),
        in_specs=[pl.BlockSpec((None, n, d), lambda bi: (bi, 0, 0)),
                  _const_spec(nm.shape), _const_spec(w.shape), _const_spec(gk.shape), _const_spec(g128.shape)],
        out_specs=[spec, spec],
        out_shape=[out, out],
        compiler_params=_params("parallel"),
        name="memory_kv",
    )(mem, nm, w, gk, g128)


def _attend(heads, z_ref):
    nb = heads[0].q.shape[1]
    stages = [(hd, i, j) for hd in heads for i in range(nb) for j in range(i + 1)]

    def scores(slot, hd, i, j):
        zmax = []
        for c in range(2):
            z = _dot(hd.k[c, j * TB:(j + 1) * TB, :], hd.q[c, i])
            if i == j:
                z = z + hd.corr[...]
            z_ref[slot, c] = z
            zmax.append(jnp.max(z, axis=0, keepdims=True))
        return zmax

    ahead = z_ref.shape[0] - 1
    pending = [scores(t, *stages[t]) for t in range(ahead)]
    m = [None, None]
    for t, (hd, i, j) in enumerate(stages):
        if t + ahead < len(stages):
            pending.append(scores((t + ahead) % (ahead + 1), *stages[t + ahead]))
        zmax = pending.pop(0)
        slot = t % (ahead + 1)
        for c in range(2):
            if j == 0:
                m[c] = zmax[c]
                hd.acc[c] = _dot(hd.v(j, c), jnp.exp2(z_ref[slot, c] - m[c]).astype(BF16))
            else:
                m_new = jnp.maximum(m[c], zmax[c])
                alpha = jnp.exp2(m[c] - m_new)
                m[c] = m_new
                hd.acc[c] = alpha * hd.acc[c] + _dot(hd.v(j, c), jnp.exp2(z_ref[slot, c] - m_new).astype(BF16))
        if j == i:
            hd.finish(i)


class _Head:
    def __init__(self, q, k, v, corr, acc, finish):
        self.q, self.k, self.v, self.corr, self.acc, self.finish = q, k, v, corr, acc, finish


def _attention_specs(s, v_rows):
    nb = s // TB
    g = HEADS_PER_STEP
    q_spec = pl.BlockSpec((None, g, 2, nb, LANES, TB), lambda bi, h: (bi, h, 0, 0, 0, 0))
    k_spec = pl.BlockSpec((None, g, 2, s, LANES), lambda bi, h: (bi, h, 0, 0, 0))
    v_spec = pl.BlockSpec((None, g, nb, v_rows, TB), lambda bi, h: (bi, h, 0, 0, 0))
    o_spec = pl.BlockSpec((None, g, s, LANES), lambda bi, h: (bi, h, 0, 0))
    return q_spec, k_spec, v_spec, o_spec


def _attention_scratch(rows):
    g = HEADS_PER_STEP
    return [pltpu.VMEM((g, 2, rows, TB), F32), pltpu.VMEM((g, TB, TB), F32), pltpu.VMEM((Z_SLOTS, 2, TB, TB), F32)]


def _diff_kernel(slope_ref, qt_ref, k_ref, vt_ref, lq1_ref, lk1_ref, lq2_ref, lk2_ref, sub_ref, o_ref,
                 acc_ref, corr_ref, z_ref, *, lam_init):
    kk = lax.broadcasted_iota(jnp.int32, (TB, TB), 0)
    qq = lax.broadcasted_iota(jnp.int32, (TB, TB), 1)
    lam = (jnp.exp(jnp.sum(lq1_ref[...] * lk1_ref[...], axis=1, keepdims=True))
           - jnp.exp(jnp.sum(lq2_ref[...] * lk2_ref[...], axis=1, keepdims=True)) + lam_init)
    gain = _lane_tile(sub_ref[...], TB // LANES) * (1.0 - lam_init)

    def head(g):
        slope = slope_ref[pl.program_id(1) * HEADS_PER_STEP + g] * LOG2E
        after = jnp.where(kk > qq, (2.0 * slope) * (qq - kk).astype(F32), 0.0)
        corr_ref[g] = jnp.where((kk // CHUNK) <= (qq // CHUNK), after, NEG)
        acc = acc_ref.at[g]

        def finish(i):
            o0 = acc[0, :DA_V_DIM, :] * (1.0 / acc[0, DA_V_DIM:DA_V_DIM + 1, :])
            o1 = acc[1, :DA_V_DIM, :] * (1.0 / acc[1, DA_V_DIM:DA_V_DIM + 1, :])
            o = o0 - lam * o1
            ms = jnp.mean(o * o, axis=0, keepdims=True)
            o_ref[g, i * TB:(i + 1) * TB, :] = (o * lax.rsqrt(ms + EPS) * gain).T.astype(BF16)

        return _Head(qt_ref.at[g], k_ref.at[g], lambda j, c: vt_ref[g, j], corr_ref.at[g], acc, finish)

    _attend([head(g) for g in range(HEADS_PER_STEP)], z_ref)


def _diff_attention(qt, k, vt, slopes, lq1, lk1, lq2, lk2, sub, lam_init):
    b, nh, _, nb, _, _ = qt.shape
    s = nb * TB
    rows = vt.shape[3]
    q_spec, k_spec, v_spec, o_spec = _attention_specs(s, rows)
    vec_spec = _const_spec(lq1.shape)
    return pl.pallas_call(
        functools.partial(_diff_kernel, lam_init=lam_init),
        grid=(b, nh // HEADS_PER_STEP),
        in_specs=[pl.BlockSpec(memory_space=pltpu.SMEM), q_spec, k_spec, v_spec,
                  vec_spec, vec_spec, vec_spec, vec_spec, _const_spec(sub.shape)],
        out_specs=o_spec,
        out_shape=jax.ShapeDtypeStruct((b, nh, s, LANES), BF16),
        scratch_shapes=_attention_scratch(rows),
        compiler_params=_params("parallel", "parallel"),
        name="diff_attention",
    )(slopes, qt, k, vt, lq1, lk1, lq2, lk2, sub)


def _fox_kernel(qt_ref, k_ref, vt_ref, o_ref, acc_ref, corr_ref, z_ref):
    kk = lax.broadcasted_iota(jnp.int32, (TB, TB), 0)
    qq = lax.broadcasted_iota(jnp.int32, (TB, TB), 1)
    corr_ref[0] = jnp.where(kk <= qq, 0.0, NEG)
    rows = acc_ref.shape[2]

    def head(g):
        acc = acc_ref.at[g]

        def finish(i):
            pair = [acc[hh, :FX_DIM, :] * (1.0 / acc[hh, FX_DIM:FX_DIM + 1, :]) for hh in range(2)]
            o_ref[g, i * TB:(i + 1) * TB, :] = jnp.concatenate(pair, axis=0).T.astype(BF16)

        return _Head(qt_ref.at[g], k_ref.at[g], lambda j, hh: vt_ref[g, j, hh * rows:(hh + 1) * rows, :],
                     corr_ref.at[0], acc, finish)

    _attend([head(g) for g in range(HEADS_PER_STEP)], z_ref)


def _fox_attention(qt, k, vt):
    b, npair, _, nb, _, _ = qt.shape
    s = nb * TB
    rows = vt.shape[3] // 2
    q_spec, k_spec, v_spec, o_spec = _attention_specs(s, 2 * rows)
    return pl.pallas_call(
        _fox_kernel,
        grid=(b, npair // HEADS_PER_STEP),
        in_specs=[q_spec, k_spec, v_spec],
        out_specs=o_spec,
        out_shape=jax.ShapeDtypeStruct((b, npair, s, LANES), BF16),
        scratch_shapes=_attention_scratch(rows),
        compiler_params=_params("parallel", "parallel"),
        name="fox_attention",
    )(qt, k, vt)


def _merge_kernel(x_ref, nm_ref, oa_ref, of_ref, qm_ref, km_ref, vm_ref, wg_ref, bg_ref,
                  wda_ref, wfx_ref, wmem_ref, wout_ref, o_ref, om_ref, merged_ref):
    x = x_ref[...]
    h = _rms_rows(x, nm_ref[...]).astype(BF16)

    for hd in range(MEM_HEADS):
        sl = slice(hd * MEM_DIM, (hd + 1) * MEM_DIM)
        s = _dot_nt(qm_ref[:, sl], km_ref[:, sl])
        p = jnp.exp2(s - jnp.max(s, axis=1, keepdims=True))
        inv = 1.0 / jnp.sum(p, axis=1, keepdims=True)
        om_ref[:, sl] = (_dot(p.astype(BF16), vm_ref[:, sl]) * inv).astype(BF16)

    oa = jnp.concatenate([oa_ref[hd] for hd in range(oa_ref.shape[0])], axis=1)
    of = jnp.concatenate([of_ref[hd] for hd in range(of_ref.shape[0])], axis=1)
    branches = ((oa, wda_ref), (of, wfx_ref), (om_ref[...], wmem_ref))
    for n in range(D_MODEL // MXU_DIM):
        sl = slice(n * MXU_DIM, (n + 1) * MXU_DIM)
        merged = None
        for br, (o_br, w_br) in enumerate(branches):
            gsl = slice(br * D_MODEL + n * MXU_DIM, br * D_MODEL + (n + 1) * MXU_DIM)
            gate = jax.nn.sigmoid(_dot(h, wg_ref[:, gsl]) + bg_ref[:, gsl])
            term = gate * _dot(o_br, w_br[:, sl])
            merged = term if merged is None else merged + term
        merged_ref[:, sl] = merged.astype(BF16)

    o_ref[...] = x + _dot(merged_ref[...], wout_ref[...])


def _merge(x, nm, oa, of, qm, km, vm, wg, bg, wda, wfx, wmem, wout):
    b, s, d = x.shape
    tm = TM_PROJ
    row = lambda width: pl.BlockSpec((None, tm, width), lambda bi, i: (bi, i, 0))
    heads = pl.BlockSpec((None, oa.shape[1], tm, LANES), lambda bi, i: (bi, 0, i, 0))
    mem_spec = pl.BlockSpec((None, N_MEM, MEM_WIDTH), lambda bi, i: (bi, 0, 0))
    return pl.pallas_call(
        _merge_kernel,
        grid=(b, s // tm),
        in_specs=[row(d), _const_spec(nm.shape), heads, heads, row(MEM_WIDTH), mem_spec, mem_spec,
                  _const_spec(wg.shape), _const_spec(bg.shape), _const_spec(wda.shape), _const_spec(wfx.shape),
                  _const_spec(wmem.shape), _const_spec(wout.shape)],
        out_specs=row(d),
        out_shape=jax.ShapeDtypeStruct((b, s, d), F32),
        scratch_shapes=[pltpu.VMEM((tm, MEM_WIDTH), BF16), pltpu.VMEM((tm, d), BF16)],
        compiler_params=_params("parallel", "parallel"),
        name="merge",
    )(x, nm, oa, of, qm, km, vm, wg, bg, wda, wfx, wmem, wout)


def _ffn_kernel(x_ref, nf_ref, wup_ref, cw_ref, cb_ref, wdn_ref, o_ref, carry_ref, act_ref):
    tm = x_ref.shape[0]

    @pl.when(pl.program_id(1) == 0)
    def _():
        carry_ref[...] = jnp.zeros_like(carry_ref)

    x = x_ref[...]
    h2 = _rms_rows(x, nf_ref[...]).astype(BF16)
    row = lax.broadcasted_iota(jnp.int32, (SUBLANES, FC), 0)

    def conv(col):
        sl = slice(col, col + FC)
        u = _dot(h2, wup_ref[:, sl])
        prev = carry_ref[:, sl]
        carry_ref[:, sl] = u[tm - SUBLANES:tm, :]
        out = u * cw_ref[CONV_W - 1:CONV_W, sl] + cb_ref[:, sl]
        for shift in range(1, CONV_W):
            us = pltpu.roll(u, shift, 0)
            head = jnp.where(row < shift, pltpu.roll(prev, shift, 0), us[0:SUBLANES, :])
            us = jnp.concatenate([head, us[SUBLANES:, :]], axis=0)
            out = out + us * cw_ref[CONV_W - 1 - shift:CONV_W - shift, sl]
        return out

    for ch in range(D_FF // FC):
        a = conv(ch * FC)
        g = conv(D_FF + ch * FC)
        act_ref[:, ch * FC:(ch + 1) * FC] = (a * jax.nn.sigmoid(a) * g).astype(BF16)

    o_ref[...] = x + _dot(act_ref[...], wdn_ref[...])


def _ffn(x, nf, wup, cw, cb, wdn):
    b, s, d = x.shape
    tm = TM_FFN
    row = pl.BlockSpec((None, tm, d), lambda bi, i: (bi, i, 0))
    return pl.pallas_call(
        _ffn_kernel,
        grid=(b, s // tm),
        in_specs=[row, _const_spec(nf.shape), _const_spec(wup.shape), _const_spec(cw.shape), _const_spec(cb.shape),
                  _const_spec(wdn.shape)],
        out_specs=row,
        out_shape=jax.ShapeDtypeStruct((b, s, d), F32),
        scratch_shapes=[pltpu.VMEM((SUBLANES, 2 * D_FF), F32), pltpu.VMEM((tm, D_FF), BF16)],
        compiler_params=_params("parallel", "arbitrary"),
        name="conv_mlp",
    )(x, nf, wup, cw, cb, wdn)


def _block_diag_ones(group):
    idx = np.arange(MXU_DIM) // group
    return jnp.asarray(idx[:, None] == idx[None, :], dtype=BF16)


def _alibi_key_table(s):
    slopes = np.asarray([2.0 ** (-8.0 * (i + 1) / DA_HEADS) * LOG2E for i in range(DA_HEADS)], np.float32)
    rest = np.arange(s, dtype=np.float32)[None, :] * slopes[:, None]
    table = np.zeros((DA_HEADS, 2, s, LANES), BF16)
    for p in range(N_SPLIT):
        piece = rest.astype(BF16)
        rest = rest - piece.astype(np.float32)
        table[:, 0, :, HALF + p] = piece
        table[:, 1, :, p] = piece
    return table


def _gate_lane_layout():
    head = np.full((LANES,), -1, np.int32)
    part = np.full((LANES,), N_SPLIT, np.int32)
    for a in range(FX_HEADS // 2):
        for p in range(N_SPLIT):
            head[HALF + N_SPLIT * a + p] = 2 * a
            head[N_SPLIT * a + p] = 2 * a + 1
            part[HALF + N_SPLIT * a + p] = p
            part[N_SPLIT * a + p] = p
    return head, part


def kernel(x, mem, norm_mix, w_in, b_gate, da_q_norm, da_k_norm, da_lambda_q1, da_lambda_k1, da_lambda_q2,
           da_lambda_k2, da_subln, fx_q_norm, fx_k_norm, fx_f_bias, mem_norm, w_mem_kv, mem_q_norm, mem_k_norm,
           w_branch_da, w_branch_fx, w_branch_mem, w_out, norm_ffn, w_up, conv_w, conv_b, w_down):
    b, s, d = x.shape
    depth = w_in.shape[0]
    off = np.cumsum(np.array(IN_SIZES))[:-1].tolist()
    slopes = jnp.asarray([2.0 ** (-8.0 * (i + 1) / DA_HEADS) for i in range(DA_HEADS)], dtype=F32)
    g64 = _block_diag_ones(DA_QK_DIM)
    g128 = _block_diag_ones(MEM_DIM)
    tri = jnp.asarray(np.tril(np.ones((MXU_DIM, MXU_DIM))), dtype=BF16)
    lane_head, lane_part = _gate_lane_layout()
    lane_used = jnp.asarray(lane_head >= 0)
    lane_src = jnp.asarray(np.maximum(lane_head, 0))
    part = jnp.asarray(lane_part).reshape(1, LANES)
    kpos = jnp.asarray(_alibi_key_table(s))
    pat_np = np.zeros((FX_HEADS // 2, HALF, TM_IN), np.float32)
    for a in range(FX_HEADS // 2):
        pat_np[a, N_SPLIT * a:N_SPLIT * (a + 1), :] = 1.0
    pat = jnp.asarray(pat_np, dtype=BF16)
    row = lambda v: v.reshape(1, -1).astype(F32)
    col = lambda v: jnp.broadcast_to(v.reshape(-1, 1).astype(F32), (v.size, LANES))

    for l in range(depth):
        lam_init = 0.8 - 0.6 * math.exp(-0.3 * l)
        a_q, a_k, a_v, f_q, f_k, f_v, f_g, m_q, w_g = jnp.split(w_in[l], off, axis=-1)
        wqt = jnp.concatenate([a_q, f_q], axis=1).T.astype(BF16)
        wvt = jnp.concatenate([a_v, f_v], axis=1).T.astype(BF16)
        wfg = jnp.where(lane_used[None, :], f_g[:, lane_src], 0.0)
        wtok = jnp.concatenate([a_k, f_k, m_q, wfg], axis=1).astype(BF16)
        fb = jnp.where(lane_used, fx_f_bias[l][lane_src], 0.0).reshape(1, LANES).astype(F32)
        gqt = col(jnp.concatenate([jnp.tile(da_q_norm[l], 2 * DA_HEADS) * (DA_QK_DIM ** -0.5 * LOG2E),
                                   jnp.tile(fx_q_norm[l], FX_HEADS) * (FX_DIM ** -0.5 * LOG2E)]))
        gtok = row(jnp.concatenate([jnp.tile(da_k_norm[l], 2 * DA_HEADS), jnp.tile(fx_k_norm[l], FX_HEADS),
                                    jnp.tile(mem_q_norm[l], MEM_HEADS) * (MEM_DIM ** -0.5 * LOG2E)]))
        gmk = row(jnp.tile(mem_k_norm[l], MEM_HEADS))

        qat, qft, ka, kf, vat, vft, qm = _in_projection(
            x, row(norm_mix[l]), wqt, wvt, wtok, gqt, gtok, fb, part, pat, g64, g128, tri, kpos)
        km, vm = _memory_kv(mem, row(mem_norm[l]), w_mem_kv[l].astype(BF16), gmk, g128)

        oa = _diff_attention(qat.reshape(b, DA_HEADS, 2, s // TB, LANES, TB), ka.reshape(b, DA_HEADS, 2, s, LANES),
                             vat, slopes, row(da_lambda_q1[l]), row(da_lambda_k1[l]), row(da_lambda_q2[l]),
                             row(da_lambda_k2[l]), col(da_subln[l]), lam_init)
        of = _fox_attention(qft.reshape(b, FX_HEADS // 2, 2, s // TB, LANES, TB),
                            kf.reshape(b, FX_HEADS // 2, 2, s, LANES), vft)

        x = _merge(x, row(norm_mix[l]), oa, of, qm, km, vm, w_g.astype(BF16), b_gate[l].reshape(1, -1).astype(F32),
                   w_branch_da[l].astype(BF16), w_branch_fx[l].astype(BF16), w_branch_mem[l].astype(BF16),
                   w_out[l].astype(BF16))
        x = _ffn(x, row(norm_ffn[l]), w_up[l].astype(BF16), conv_w[l].astype(F32), row(conv_b[l]),
                 w_down[l].astype(BF16))
    return x
```

```python
import functools
import math

import jax
import jax.numpy as jnp
import numpy as np
from jax import lax
from jax.experimental import pallas as pl
from jax.experimental.pallas import tpu as pltpu

D_MODEL = 1024
CHUNK = 64
N_MEM = 256
EPS = 1e-6

DA_HEADS = 4
DA_QK_DIM = 64
DA_V_DIM = 128
DA_WIDTH = 512
FX_HEADS = 8
FX_DIM = 64
FX_WIDTH = 512
MEM_HEADS = 4
MEM_DIM = 128
MEM_WIDTH = 512
N_BRANCH = 3
D_FF = 2816
CONV_W = 3

IN_SIZES = (512, 512, DA_WIDTH, FX_WIDTH, FX_WIDTH, FX_WIDTH, FX_HEADS, MEM_WIDTH, N_BRANCH * D_MODEL)

LANES = 128
SUBLANES = 8
MXU_DIM = 256
VMEM_LIMIT = 56 * 1024 * 1024
VMEM_LIMIT_IN = 62 * 1024 * 1024

LOG2E = 1.4426950408889634
NEG = -1e30

TM_PROJ = 1024
TM_IN = 1024
TM_FFN = 1024
TB = 256
FC = 256
N_SPLIT = 3
HALF = LANES // 2
ONES_ROWS = 16
Z_SLOTS = 3
HEADS_PER_STEP = 4

F32 = jnp.float32
BF16 = jnp.bfloat16

_NT = (((1,), (1,)), ((), ()))


def _dot(a, b):
    return jnp.dot(a, b, preferred_element_type=F32)


def _dot_nt(a, b):
    return lax.dot_general(a, b, _NT, preferred_element_type=F32)


def _rms_rows(x, g):
    ms = jnp.mean(x * x, axis=-1, keepdims=True)
    return x * lax.rsqrt(ms + EPS) * g


def _three_way(a):
    hi = a.astype(BF16)
    r = a - hi.astype(F32)
    mid = r.astype(BF16)
    lo = (r - mid.astype(F32)).astype(BF16)
    return hi, mid, lo


def _lane_tile(x, reps):
    return jnp.concatenate([x] * reps, axis=1) if reps > 1 else x


def _const_spec(shape):
    nd = len(shape)
    return pl.BlockSpec(shape, lambda *_: (0,) * nd, pipeline_mode=pl.Buffered(1))


def _params(*sem, vmem_limit=VMEM_LIMIT):
    return pltpu.CompilerParams(dimension_semantics=sem, vmem_limit_bytes=vmem_limit)


def _inproj_kernel(x_ref, nm_ref, wqt_ref, wvt_ref, wtok_ref, gqt_ref, gtok_ref, fb_ref,
                   part_ref, pat_ref, g64_ref, g128_ref, tri_ref, kpos_ref,
                   qat_ref, qft_ref, ka_ref, kf_ref, vat_ref, vft_ref, qm_ref, carry_ref, ht_ref):
    tm = x_ref.shape[0]
    nb = tm // TB
    h = _rms_rows(x_ref[...], nm_ref[...]).astype(BF16)
    ht_ref[...] = h.T
    ht = ht_ref[...]
    lane = lax.broadcasted_iota(jnp.int32, (tm, LANES), 1)

    def store_blocks(out_ref, idx, yt):
        for jb in range(nb):
            out_ref[idx, jb] = yt[:, jb * TB:(jb + 1) * TB]

    def q_slab(s):
        rows = slice(s * MXU_DIM, (s + 1) * MXU_DIM)
        yt = _dot(wqt_ref[rows, :], ht)
        y3 = yt.reshape(MXU_DIM // DA_QK_DIM, DA_QK_DIM, tm)
        inv = lax.rsqrt(jnp.mean(y3 * y3, axis=1, keepdims=True) + EPS)
        yn = ((y3 * inv).reshape(MXU_DIM, tm) * _lane_tile(gqt_ref[rows, :], tm // LANES)).astype(BF16)
        out_ref = (qat_ref, qft_ref)[s // 2]
        for t in range(2):
            hd = 2 * (s % 2) + t
            ones_rows = pat_ref[hd * (s // 2)]
            store_blocks(out_ref, 2 * hd, jnp.concatenate([yn[t * LANES:t * LANES + HALF], ones_rows], axis=0))
            store_blocks(out_ref, 2 * hd + 1, jnp.concatenate([ones_rows, yn[t * LANES + HALF:(t + 1) * LANES]], axis=0))

    ones = jnp.ones((ONES_ROWS, tm), BF16)

    def v_slab(s):
        rows = slice(s * MXU_DIM, (s + 1) * MXU_DIM)
        vt = _dot(wvt_ref[rows, :], ht).astype(BF16)
        for t in range(2):
            hd = 2 * (s % 2) + t
            if s < 2:
                store_blocks(vat_ref, hd, jnp.concatenate([vt[t * LANES:(t + 1) * LANES], ones], axis=0))
            else:
                store_blocks(vft_ref, hd, jnp.concatenate(
                    [vt[t * LANES:t * LANES + HALF], ones, vt[t * LANES + HALF:(t + 1) * LANES], ones], axis=0))


    y = _dot(h, wtok_ref[...])

    def normed_pair(col, group_ref, inv_dim):
        cols = [slice(col + s * MXU_DIM, col + (s + 1) * MXU_DIM) for s in range(2)]
        ss = [_dot((y[:, sl] * y[:, sl]).astype(BF16), group_ref[...]) for sl in cols]
        return [(y[:, sl] * lax.rsqrt(q * inv_dim + EPS) * gtok_ref[:, sl]).astype(BF16) for sl, q in zip(cols, ss)]

    fg = y[:, 2 * DA_WIDTH + MEM_WIDTH:] + fb_ref[...]
    logf = jnp.minimum(fg, 0.0) - jnp.log(1.0 + jnp.exp(-jnp.abs(fg)))
    tri = tri_ref[...]
    pieces = _three_way(logf)
    local = [sum(_dot(tri, piece[r * MXU_DIM:(r + 1) * MXU_DIM]) for piece in pieces)
             for r in range(tm // MXU_DIM)]

    q_slab(0)
    for s, yn in enumerate(normed_pair(2 * DA_WIDTH, g128_ref, 1.0 / MEM_DIM)):
        qm_ref[:, s * MXU_DIM:(s + 1) * MXU_DIM] = yn
    v_slab(0)

    for s, yn in enumerate(normed_pair(0, g64_ref, 1.0 / DA_QK_DIM)):
        for t in range(2):
            hd = 2 * s + t
            kt = yn[:, t * LANES:(t + 1) * LANES]
            ka_ref[2 * hd] = jnp.where(lane < HALF, kt, kpos_ref[hd, 0])
            ka_ref[2 * hd + 1] = jnp.where(lane >= HALF, kt, kpos_ref[hd, 1])
        (q_slab, v_slab)[s](1)

    @pl.when(pl.program_id(1) == 0)
    def _():
        carry_ref[...] = jnp.zeros_like(carry_ref)

    run = carry_ref[0:1, :]
    blocks = []
    for blk in local:
        blocks.append(blk + run)
        run = blocks[-1][MXU_DIM - 1:MXU_DIM, :]
    c = jnp.concatenate(blocks, axis=0)
    carry_ref[...] = jnp.broadcast_to(run, carry_ref.shape)
    hi, mid, lo = _three_way(c * (-LOG2E))
    part = jnp.broadcast_to(part_ref[...], (tm, LANES))
    zero = jnp.zeros_like(hi)
    feat = jnp.where(part == 0, hi, jnp.where(part == 1, mid, jnp.where(part == 2, lo, zero)))
    q_slab(2)

    for s, yn in enumerate(normed_pair(DA_WIDTH, g64_ref, 1.0 / FX_DIM)):
        for t in range(2):
            a = 2 * s + t
            kt = yn[:, t * LANES:(t + 1) * LANES]
            fa = jnp.where(lane < HALF + N_SPLIT * a, zero, jnp.where(lane < HALF + N_SPLIT * (a + 1), feat, zero))
            fb = jnp.where(lane < N_SPLIT * a, zero, jnp.where(lane < N_SPLIT * (a + 1), feat, zero))
            kf_ref[2 * a] = jnp.where(lane < HALF, kt, fa)
            kf_ref[2 * a + 1] = jnp.where(lane >= HALF, kt, fb)
        (v_slab, q_slab)[s](2 + s)
    v_slab(3)


def _in_projection(x, nm, wqt, wvt, wtok, gqt, gtok, fb, part, pat, g64, g128, tri, kpos):
    b, s, d = x.shape
    tm = TM_IN

    def v_major(rows):
        return (jax.ShapeDtypeStruct((b, DA_HEADS, s // TB, rows, TB), BF16),
                pl.BlockSpec((None, DA_HEADS, tm // TB, rows, TB), lambda bi, i: (bi, 0, i, 0, 0)))

    va_major, va_spec = v_major(DA_V_DIM + ONES_ROWS)
    vf_major, vf_spec = v_major(2 * (FX_DIM + ONES_ROWS))
    tiles = 2 * DA_HEADS
    q_major = jax.ShapeDtypeStruct((b, tiles, s // TB, LANES, TB), BF16)
    q_spec = pl.BlockSpec((None, tiles, tm // TB, LANES, TB), lambda bi, i: (bi, 0, i, 0, 0))
    k_major = jax.ShapeDtypeStruct((b, tiles, s, LANES), BF16)
    k_spec = pl.BlockSpec((None, tiles, tm, LANES), lambda bi, i: (bi, 0, i, 0))
    consts = (nm, wqt, wvt, wtok, gqt, gtok, fb, part, pat, g64, g128, tri)
    return pl.pallas_call(
        _inproj_kernel,
        grid=(b, s // tm),
        in_specs=[pl.BlockSpec((None, tm, d), lambda bi, i: (bi, i, 0))] + [_const_spec(c.shape) for c in consts]
        + [pl.BlockSpec((DA_HEADS, 2, tm, LANES), lambda bi, i: (0, 0, i, 0))],
        out_specs=[q_spec, q_spec, k_spec, k_spec, va_spec, vf_spec,
                   pl.BlockSpec((None, tm, MEM_WIDTH), lambda bi, i: (bi, i, 0))],
        out_shape=[q_major, q_major, k_major, k_major, va_major, vf_major,
                   jax.ShapeDtypeStruct((b, s, MEM_WIDTH), BF16)],
        scratch_shapes=[pltpu.VMEM((SUBLANES, LANES), F32), pltpu.VMEM((d, tm), BF16)],
        compiler_params=_params("parallel", "arbitrary", vmem_limit=VMEM_LIMIT_IN),
        name="in_projection",
    )(x, *consts, kpos)


def _memkv_kernel(mem_ref, nm_ref, w_ref, gk_ref, g128_ref, km_ref, vm_ref):
    mh = _rms_rows(mem_ref[...], nm_ref[...]).astype(BF16)
    for s in range(2):
        sl = slice(s * MXU_DIM, (s + 1) * MXU_DIM)
        y = _dot(mh, w_ref[:, sl])
        ss = _dot((y * y).astype(BF16), g128_ref[...])
        km_ref[:, sl] = (y * lax.rsqrt(ss * (1.0 / MEM_DIM) + EPS) * gk_ref[:, sl]).astype(BF16)
        vm_ref[:, sl] = _dot(mh, w_ref[:, MEM_WIDTH + s * MXU_DIM:MEM_WIDTH + (s + 1) * MXU_DIM]).astype(BF16)


def _memory_kv(mem, nm, w, gk, g128):
    b, n, d = mem.shape
    out = jax.ShapeDtypeStruct((b, n, MEM_WIDTH), BF16)
    spec = pl.BlockSpec((None, n, MEM_WIDTH), lambda bi: (bi, 0, 0))
    return pl.pallas_call(
        _memkv_kernel,
        grid=(b,),
        in_specs=[pl.BlockSpec((None, n, d), lambda bi: (bi, 0, 0)),
                  _const_spec(nm.shape), _const_spec(w.shape), _const_spec(gk.shape), _const_spec(g128.shape)],
        out_specs=[spec, spec],
        out_shape=[out, out],
        compiler_params=_params("parallel"),
        name="memory_kv",
    )(mem, nm, w, gk, g128)


def _attend(heads, z_ref):
    nb = heads[0].q.shape[1]
    stages = [(hd, i, j) for hd in heads for i in range(nb) for j in range(i + 1)]

    def scores(slot, hd, i, j):
        zmax = []
        for c in range(2):
            z = _dot(hd.k[c, j * TB:(j + 1) * TB, :], hd.q[c, i])
            if i == j:
                z = z + hd.corr[...]
            z_ref[slot, c] = z
            zmax.append(jnp.max(z, axis=0, keepdims=True))
        return zmax

    ahead = z_ref.shape[0] - 1
    pending = [scores(t, *stages[t]) for t in range(ahead)]
    m = [None, None]
    for t, (hd, i, j) in enumerate(stages):
        if t + ahead < len(stages):
            pending.append(scores((t + ahead) % (ahead + 1), *stages[t + ahead]))
        zmax = pending.pop(0)
        slot = t % (ahead + 1)
        for c in range(2):
            if j == 0:
                m[c] = zmax[c]
                hd.acc[c] = _dot(hd.v(j, c), jnp.exp2(z_ref[slot, c] - m[c]).astype(BF16))
            else:
                m_new = jnp.maximum(m[c], zmax[c])
                alpha = jnp.exp2(m[c] - m_new)
                m[c] = m_new
                hd.acc[c] = alpha * hd.acc[c] + _dot(hd.v(j, c), jnp.exp2(z_ref[slot, c] - m_new).astype(BF16))
        if j == i:
            hd.finish(i)


class _Head:
    def __init__(self, q, k, v, corr, acc, finish):
        self.q, self.k, self.v, self.corr, self.acc, self.finish = q, k, v, corr, acc, finish


def _attention_specs(s, v_rows):
    nb = s // TB
    g = HEADS_PER_STEP
    q_spec = pl.BlockSpec((None, g, 2, nb, LANES, TB), lambda bi, h: (bi, h, 0, 0, 0, 0))
    k_spec = pl.BlockSpec((None, g, 2, s, LANES), lambda bi, h: (bi, h, 0, 0, 0))
    v_spec = pl.BlockSpec((None, g, nb, v_rows, TB), lambda bi, h: (bi, h, 0, 0, 0))
    o_spec = pl.BlockSpec((None, g, s, LANES), lambda bi, h: (bi, h, 0, 0))
    return q_spec, k_spec, v_spec, o_spec


def _attention_scratch(rows):
    g = HEADS_PER_STEP
    return [pltpu.VMEM((g, 2, rows, TB), F32), pltpu.VMEM((g, TB, TB), F32), pltpu.VMEM((Z_SLOTS, 2, TB, TB), F32)]


def _diff_kernel(slope_ref, qt_ref, k_ref, vt_ref, lq1_ref, lk1_ref, lq2_ref, lk2_ref, sub_ref, o_ref,
                 acc_ref, corr_ref, z_ref, *, lam_init):
    kk = lax.broadcasted_iota(jnp.int32, (TB, TB), 0)
    qq = lax.broadcasted_iota(jnp.int32, (TB, TB), 1)
    lam = (jnp.exp(jnp.sum(lq1_ref[...] * lk1_ref[...], axis=1, keepdims=True))
           - jnp.exp(jnp.sum(lq2_ref[...] * lk2_ref[...], axis=1, keepdims=True)) + lam_init)
    gain = _lane_tile(sub_ref[...], TB // LANES) * (1.0 - lam_init)

    def head(g):
        slope = slope_ref[pl.program_id(1) * HEADS_PER_STEP + g] * LOG2E
        after = jnp.where(kk > qq, (2.0 * slope) * (qq - kk).astype(F32), 0.0)
        corr_ref[g] = jnp.where((kk // CHUNK) <= (qq // CHUNK), after, NEG)
        acc = acc_ref.at[g]

        def finish(i):
            o0 = acc[0, :DA_V_DIM, :] * (1.0 / acc[0, DA_V_DIM:DA_V_DIM + 1, :])
            o1 = acc[1, :DA_V_DIM, :] * (1.0 / acc[1, DA_V_DIM:DA_V_DIM + 1, :])
            o = o0 - lam * o1
            ms = jnp.mean(o * o, axis=0, keepdims=True)
            o_ref[g, i * TB:(i + 1) * TB, :] = (o * lax.rsqrt(ms + EPS) * gain).T.astype(BF16)

        return _Head(qt_ref.at[g], k_ref.at[g], lambda j, c: vt_ref[g, j], corr_ref.at[g], acc, finish)

    _attend([head(g) for g in range(HEADS_PER_STEP)], z_ref)


def _diff_attention(qt, k, vt, slopes, lq1, lk1, lq2, lk2, sub, lam_init):
    b, nh, _, nb, _, _ = qt.shape
    s = nb * TB
    rows = vt.shape[3]
    q_spec, k_spec, v_spec, o_spec = _attention_specs(s, rows)
    vec_spec = _const_spec(lq1.shape)
    return pl.pallas_call(
        functools.partial(_diff_kernel, lam_init=lam_init),
        grid=(b, nh // HEADS_PER_STEP),
        in_specs=[pl.BlockSpec(memory_space=pltpu.SMEM), q_spec, k_spec, v_spec,
                  vec_spec, vec_spec, vec_spec, vec_spec, _const_spec(sub.shape)],
        out_specs=o_spec,
        out_shape=jax.ShapeDtypeStruct((b, nh, s, LANES), BF16),
        scratch_shapes=_attention_scratch(rows),
        compiler_params=_params("parallel", "parallel"),
        name="diff_attention",
    )(slopes, qt, k, vt, lq1, lk1, lq2, lk2, sub)


def _fox_kernel(qt_ref, k_ref, vt_ref, o_ref, acc_ref, corr_ref, z_ref):
    kk = lax.broadcasted_iota(jnp.int32, (TB, TB), 0)
    qq = lax.broadcasted_iota(jnp.int32, (TB, TB), 1)
    corr_ref[0] = jnp.where(kk <= qq, 0.0, NEG)
    rows = acc_ref.shape[2]

    def head(g):
        acc = acc_ref.at[g]

        def finish(i):
            pair = [acc[hh, :FX_DIM, :] * (1.0 / acc[hh, FX_DIM:FX_DIM + 1, :]) for hh in range(2)]
            o_ref[g, i * TB:(i + 1) * TB, :] = jnp.concatenate(pair, axis=0).T.astype(BF16)

        return _Head(qt_ref.at[g], k_ref.at[g], lambda j, hh: vt_ref[g, j, hh * rows:(hh + 1) * rows, :],
                     corr_ref.at[0], acc, finish)

    _attend([head(g) for g in range(HEADS_PER_STEP)], z_ref)


def _fox_attention(qt, k, vt):
    b, npair, _, nb, _, _ = qt.shape
    s = nb * TB
    rows = vt.shape[3] // 2
    q_spec, k_spec, v_spec, o_spec = _attention_specs(s, 2 * rows)
    return pl.pallas_call(
        _fox_kernel,
        grid=(b, npair // HEADS_PER_STEP),
        in_specs=[q_spec, k_spec, v_spec],
        out_specs=o_spec,
        out_shape=jax.ShapeDtypeStruct((b, npair, s, LANES), BF16),
        scratch_shapes=_attention_scratch(rows),
        compiler_params=_params("parallel", "parallel"),
        name="fox_attention",
    )(qt, k, vt)


def _merge_kernel(x_ref, nm_ref, oa_ref, of_ref, qm_ref, km_ref, vm_ref, wg_ref, bg_ref,
                  wda_ref, wfx_ref, wmem_ref, wout_ref, o_ref, om_ref, merged_ref):
    x = x_ref[...]
    h = _rms_rows(x, nm_ref[...]).astype(BF16)

    for hd in range(MEM_HEADS):
        sl = slice(hd * MEM_DIM, (hd + 1) * MEM_DIM)
        s = _dot_nt(qm_ref[:, sl], km_ref[:, sl])
        p = jnp.exp2(s - jnp.max(s, axis=1, keepdims=True))
        inv = 1.0 / jnp.sum(p, axis=1, keepdims=True)
        om_ref[:, sl] = (_dot(p.astype(BF16), vm_ref[:, sl]) * inv).astype(BF16)

    oa = jnp.concatenate([oa_ref[hd] for hd in range(oa_ref.shape[0])], axis=1)
    of = jnp.concatenate([of_ref[hd] for hd in range(of_ref.shape[0])], axis=1)
    branches = ((oa, wda_ref), (of, wfx_ref), (om_ref[...], wmem_ref))
    for n in range(D_MODEL // MXU_DIM):
        sl = slice(n * MXU_DIM, (n + 1) * MXU_DIM)
        merged = None
        for br, (o_br, w_br) in enumerate(branches):
            gsl = slice(br * D_MODEL + n * MXU_DIM, br * D_MODEL + (n + 1) * MXU_DIM)
            gate = jax.nn.sigmoid(_dot(h, wg_ref[:, gsl]) + bg_ref[:, gsl])
            term = gate * _dot(o_br, w_br[:, sl])
            merged = term if merged is None else merged + term
        merged_ref[:, sl] = merged.astype(BF16)

    o_ref[...] = x + _dot(merged_ref[...], wout_ref[...])


def _merge(x, nm, oa, of, qm, km, vm, wg, bg, wda, wfx, wmem, wout):
    b, s, d = x.shape
    tm = TM_PROJ
    row = lambda width: pl.BlockSpec((None, tm, width), lambda bi, i: (bi, i, 0))
    heads = pl.BlockSpec((None, oa.shape[1], tm, LANES), lambda bi, i: (bi, 0, i, 0))
    mem_spec = pl.BlockSpec((None, N_MEM, MEM_WIDTH), lambda bi, i: (bi, 0, 0))
    return pl.pallas_call(
        _merge_kernel,
        grid=(b, s // tm),
        in_specs=[row(d), _const_spec(nm.shape), heads, heads, row(MEM_WIDTH), mem_spec, mem_spec,
                  _const_spec(wg.shape), _const_spec(bg.shape), _const_spec(wda.shape), _const_spec(wfx.shape),
                  _const_spec(wmem.shape), _const_spec(wout.shape)],
        out_specs=row(d),
        out_shape=jax.ShapeDtypeStruct((b, s, d), F32),
        scratch_shapes=[pltpu.VMEM((tm, MEM_WIDTH), BF16), pltpu.VMEM((tm, d), BF16)],
        compiler_params=_params("parallel", "parallel"),
        name="merge",
    )(x, nm, oa, of, qm, km, vm, wg, bg, wda, wfx, wmem, wout)


def _ffn_kernel(x_ref, nf_ref, wup_ref, cw_ref, cb_ref, wdn_ref, o_ref, carry_ref, act_ref):
    tm = x_ref.shape[0]

    @pl.when(pl.program_id(1) == 0)
    def _():
        carry_ref[...] = jnp.zeros_like(carry_ref)

    x = x_ref[...]
    h2 = _rms_rows(x, nf_ref[...]).astype(BF16)
    row = lax.broadcasted_iota(jnp.int32, (SUBLANES, FC), 0)

    def conv(col):
        sl = slice(col, col + FC)
        u = _dot(h2, wup_ref[:, sl])
        prev = carry_ref[:, sl]
        carry_ref[:, sl] = u[tm - SUBLANES:tm, :]
        out = u * cw_ref[CONV_W - 1:CONV_W, sl] + cb_ref[:, sl]
        for shift in range(1, CONV_W):
            us = pltpu.roll(u, shift, 0)
            head = jnp.where(row < shift, pltpu.roll(prev, shift, 0), us[0:SUBLANES, :])
            us = jnp.concatenate([head, us[SUBLANES:, :]], axis=0)
            out = out + us * cw_ref[CONV_W - 1 - shift:CONV_W - shift, sl]
        return out

    for ch in range(D_FF // FC):
        a = conv(ch * FC)
        g = conv(D_FF + ch * FC)
        act_ref[:, ch * FC:(ch + 1) * FC] = (a * jax.nn.sigmoid(a) * g).astype(BF16)

    o_ref[...] = x + _dot(act_ref[...], wdn_ref[...])


def _ffn(x, nf, wup, cw, cb, wdn):
    b, s, d = x.shape
    tm = TM_FFN
    row = pl.BlockSpec((None, tm, d), lambda bi, i: (bi, i, 0))
    return pl.pallas_call(
        _ffn_kernel,
        grid=(b, s // tm),
        in_specs=[row, _const_spec(nf.shape), _const_spec(wup.shape), _const_spec(cw.shape), _const_spec(cb.shape),
                  _const_spec(wdn.shape)],
        out_specs=row,
        out_shape=jax.ShapeDtypeStruct((b, s, d), F32),
        scratch_shapes=[pltpu.VMEM((SUBLANES, 2 * D_FF), F32), pltpu.VMEM((tm, D_FF), BF16)],
        compiler_params=_params("parallel", "arbitrary"),
        name="conv_mlp",
    )(x, nf, wup, cw, cb, wdn)


def _block_diag_ones(group):
    idx = np.arange(MXU_DIM) // group
    return jnp.asarray(idx[:, None] == idx[None, :], dtype=BF16)


def _alibi_key_table(s):
    slopes = np.asarray([2.0 ** (-8.0 * (i + 1) / DA_HEADS) * LOG2E for i in range(DA_HEADS)], np.float32)
    rest = np.arange(s, dtype=np.float32)[None, :] * slopes[:, None]
    table = np.zeros((DA_HEADS, 2, s, LANES), BF16)
    for p in range(N_SPLIT):
        piece = rest.astype(BF16)
        rest = rest - piece.astype(np.float32)
        table[:, 0, :, HALF + p] = piece
        table[:, 1, :, p] = piece
    return table


def _gate_lane_layout():
    head = np.full((LANES,), -1, np.int32)
    part = np.full((LANES,), N_SPLIT, np.int32)
    for a in range(FX_HEADS // 2):
        for p in range(N_SPLIT):
            head[HALF + N_SPLIT * a + p] = 2 * a
            head[N_SPLIT * a + p] = 2 * a + 1
            part[HALF + N_SPLIT * a + p] = p
            part[N_SPLIT * a + p] = p
    return head, part


def kernel(x, mem, norm_mix, w_in, b_gate, da_q_norm, da_k_norm, da_lambda_q1, da_lambda_k1, da_lambda_q2,
           da_lambda_k2, da_subln, fx_q_norm, fx_k_norm, fx_f_bias, mem_norm, w_mem_kv, mem_q_norm, mem_k_norm,
           w_branch_da, w_branch_fx, w_branch_mem, w_out, norm_ffn, w_up, conv_w, conv_b, w_down):
    b, s, d = x.shape
    depth = w_in.shape[0]
    off = np.cumsum(np.array(IN_SIZES))[:-1].tolist()
    slopes = jnp.asarray([2.0 ** (-8.0 * (i + 1) / DA_HEADS) for i in range(DA_HEADS)], dtype=F32)
    g64 = _block_diag_ones(DA_QK_DIM)
    g128 = _block_diag_ones(MEM_DIM)
    tri = jnp.asarray(np.tril(np.ones((MXU_DIM, MXU_DIM))), dtype=BF16)
    lane_head, lane_part = _gate_lane_layout()
    lane_used = jnp.asarray(lane_head >= 0)
    lane_src = jnp.asarray(np.maximum(lane_head, 0))
    part = jnp.asarray(lane_part).reshape(1, LANES)
    kpos = jnp.asarray(_alibi_key_table(s))
    pat_np = np.zeros((FX_HEADS // 2, HALF, TM_IN), np.float32)
    for a in range(FX_HEADS // 2):
        pat_np[a, N_SPLIT * a:N_SPLIT * (a + 1), :] = 1.0
    pat = jnp.asarray(pat_np, dtype=BF16)
    row = lambda v: v.reshape(1, -1).astype(F32)
    col = lambda v: jnp.broadcast_to(v.reshape(-1, 1).astype(F32), (v.size, LANES))

    for l in range(depth):
        lam_init = 0.8 - 0.6 * math.exp(-0.3 * l)
        a_q, a_k, a_v, f_q, f_k, f_v, f_g, m_q, w_g = jnp.split(w_in[l], off, axis=-1)
        wqt = jnp.concatenate([a_q, f_q], axis=1).T.astype(BF16)
        wvt = jnp.concatenate([a_v, f_v], axis=1).T.astype(BF16)
        wfg = jnp.where(lane_used[None, :], f_g[:, lane_src], 0.0)
        wtok = jnp.concatenate([a_k, f_k, m_q, wfg], axis=1).astype(BF16)
        fb = jnp.where(lane_used, fx_f_bias[l][lane_src], 0.0).reshape(1, LANES).astype(F32)
        gqt = col(jnp.concatenate([jnp.tile(da_q_norm[l], 2 * DA_HEADS) * (DA_QK_DIM ** -0.5 * LOG2E),
                                   jnp.tile(fx_q_norm[l], FX_HEADS) * (FX_DIM ** -0.5 * LOG2E)]))
        gtok = row(jnp.concatenate([jnp.tile(da_k_norm[l], 2 * DA_HEADS), jnp.tile(fx_k_norm[l], FX_HEADS),
                                    jnp.tile(mem_q_norm[l], MEM_HEADS) * (MEM_DIM ** -0.5 * LOG2E)]))
        gmk = row(jnp.tile(mem_k_norm[l], MEM_HEADS))

        qat, qft, ka, kf, vat, vft, qm = _in_projection(
            x, row(norm_mix[l]), wqt, wvt, wtok, gqt, gtok, fb, part, pat, g64, g128, tri, kpos)
        km, vm = _memory_kv(mem, row(mem_norm[l]), w_mem_kv[l].astype(BF16), gmk, g128)

        oa = _diff_attention(qat.reshape(b, DA_HEADS, 2, s // TB, LANES, TB), ka.reshape(b, DA_HEADS, 2, s, LANES),
                             vat, slopes, row(da_lambda_q1[l]), row(da_lambda_k1[l]), row(da_lambda_q2[l]),
                             row(da_lambda_k2[l]), col(da_subln[l]), lam_init)
        of = _fox_attention(qft.reshape(b, FX_HEADS // 2, 2, s // TB, LANES, TB),
                            kf.reshape(b, FX_HEADS // 2, 2, s, LANES), vft)

        x = _merge(x, row(norm_mix[l]), oa, of, qm, km, vm, w_g.astype(BF16), b_gate[l].reshape(1, -1).astype(F32),
                   w_branch_da[l].astype(BF16), w_branch_fx[l].astype(BF16), w_branch_mem[l].astype(BF16),
                   w_out[l].astype(BF16))
        x = _ffn(x, row(norm_ffn[l]), w_up[l].astype(BF16), conv_w[l].astype(F32), row(conv_b[l]),
                 w_down[l].astype(BF16))
    return x
```

```python
import functools
import math

import jax
import jax.numpy as jnp
import numpy as np
from jax import lax
from jax.experimental import pallas as pl
from jax.experimental.pallas import tpu as pltpu

D_MODEL = 1024
CHUNK = 64
N_MEM = 256
EPS = 1e-6

DA_HEADS = 4
DA_QK_DIM = 64
DA_V_DIM = 128
DA_WIDTH = 512
FX_HEADS = 8
FX_DIM = 64
FX_WIDTH = 512
MEM_HEADS = 4
MEM_DIM = 128
MEM_WIDTH = 512
N_BRANCH = 3
D_FF = 2816
CONV_W = 3

IN_SIZES = (512, 512, DA_WIDTH, FX_WIDTH, FX_WIDTH, FX_WIDTH, FX_HEADS, MEM_WIDTH, N_BRANCH * D_MODEL)

LANES = 128
SUBLANES = 8
MXU_DIM = 256
VMEM_LIMIT = 56 * 1024 * 1024
VMEM_LIMIT_IN = 62 * 1024 * 1024

LOG2E = 1.4426950408889634
NEG = -1e30

TM_PROJ = 1024
TM_IN = 1024
TM_FFN = 1024
TB = 256
FC = 256
N_SPLIT = 3
HALF = LANES // 2
ONES_ROWS = 16
DA_V_ROWS = 160
Z_SLOTS = 3
HEADS_PER_STEP = 4

F32 = jnp.float32
BF16 = jnp.bfloat16

_NT = (((1,), (1,)), ((), ()))


def _dot(a, b):
    return jnp.dot(a, b, preferred_element_type=F32)


def _dot_nt(a, b):
    return lax.dot_general(a, b, _NT, preferred_element_type=F32)


def _rms_rows(x, g):
    ms = jnp.mean(x * x, axis=-1, keepdims=True)
    return x * lax.rsqrt(ms + EPS) * g


def _three_way(a):
    hi = a.astype(BF16)
    r = a - hi.astype(F32)
    mid = r.astype(BF16)
    lo = (r - mid.astype(F32)).astype(BF16)
    return hi, mid, lo


def _lane_tile(x, reps):
    return jnp.concatenate([x] * reps, axis=1) if reps > 1 else x


def _const_spec(shape):
    nd = len(shape)
    return pl.BlockSpec(shape, lambda *_: (0,) * nd, pipeline_mode=pl.Buffered(1))


def _params(*sem, vmem_limit=VMEM_LIMIT):
    return pltpu.CompilerParams(dimension_semantics=sem, vmem_limit_bytes=vmem_limit)


def _inproj_kernel(x_ref, nm_ref, wqt_ref, wvt_ref, wtok_ref, gqt_ref, gtok_ref, fb_ref,
                   part_ref, pat_ref, g64_ref, g128_ref, tri_ref, kpos_ref,
                   qat_ref, qft_ref, ka_ref, kf_ref, vat_ref, vft_ref, qm_ref, carry_ref, ht_ref):
    tm = x_ref.shape[0]
    nb = tm // TB
    h = _rms_rows(x_ref[...], nm_ref[...]).astype(BF16)
    ht_ref[...] = h.T
    ht = ht_ref[...]
    lane = lax.broadcasted_iota(jnp.int32, (tm, LANES), 1)

    def store_blocks(out_ref, idx, yt):
        for jb in range(nb):
            out_ref[idx, jb] = yt[:, jb * TB:(jb + 1) * TB]

    def q_slab(s):
        rows = slice(s * MXU_DIM, (s + 1) * MXU_DIM)
        yt = _dot(wqt_ref[rows, :], ht)
        y3 = yt.reshape(MXU_DIM // DA_QK_DIM, DA_QK_DIM, tm)
        inv = lax.rsqrt(jnp.mean(y3 * y3, axis=1, keepdims=True) + EPS)
        yn = ((y3 * inv).reshape(MXU_DIM, tm) * _lane_tile(gqt_ref[rows, :], tm // LANES)).astype(BF16)
        out_ref = (qat_ref, qft_ref)[s // 2]
        for t in range(2):
            hd = 2 * (s % 2) + t
            ones_rows = pat_ref[hd * (s // 2)]
            store_blocks(out_ref, 2 * hd, jnp.concatenate([yn[t * LANES:t * LANES + HALF], ones_rows], axis=0))
            store_blocks(out_ref, 2 * hd + 1, jnp.concatenate([ones_rows, yn[t * LANES + HALF:(t + 1) * LANES]], axis=0))

    ones = jnp.ones((ONES_ROWS, tm), BF16)

    def v_slab(s):
        rows = slice(s * MXU_DIM, (s + 1) * MXU_DIM)
        vt = _dot(wvt_ref[rows, :], ht).astype(BF16)
        for t in range(2):
            hd = 2 * (s % 2) + t
            if s < 2:
                store_blocks(vat_ref, hd, jnp.concatenate(
                    [vt[t * LANES:(t + 1) * LANES]] + [ones] * ((DA_V_ROWS - DA_V_DIM) // ONES_ROWS), axis=0))
            else:
                store_blocks(vft_ref, hd, jnp.concatenate(
                    [vt[t * LANES:t * LANES + HALF], ones, vt[t * LANES + HALF:(t + 1) * LANES], ones], axis=0))


    y = _dot(h, wtok_ref[...])

    def normed_pair(col, group_ref, inv_dim):
        cols = [slice(col + s * MXU_DIM, col + (s + 1) * MXU_DIM) for s in range(2)]
        ss = [_dot((y[:, sl] * y[:, sl]).astype(BF16), group_ref[...]) for sl in cols]
        return [(y[:, sl] * lax.rsqrt(q * inv_dim + EPS) * gtok_ref[:, sl]).astype(BF16) for sl, q in zip(cols, ss)]

    fg = y[:, 2 * DA_WIDTH + MEM_WIDTH:] + fb_ref[...]
    logf = jnp.minimum(fg, 0.0) - jnp.log(1.0 + jnp.exp(-jnp.abs(fg)))
    tri = tri_ref[...]
    pieces = _three_way(logf)
    local = [sum(_dot(tri, piece[r * MXU_DIM:(r + 1) * MXU_DIM]) for piece in pieces)
             for r in range(tm // MXU_DIM)]

    q_slab(0)
    for s, yn in enumerate(normed_pair(2 * DA_WIDTH, g128_ref, 1.0 / MEM_DIM)):
        qm_ref[:, s * MXU_DIM:(s + 1) * MXU_DIM] = yn
    v_slab(0)

    for s, yn in enumerate(normed_pair(0, g64_ref, 1.0 / DA_QK_DIM)):
        for t in range(2):
            hd = 2 * s + t
            kt = yn[:, t * LANES:(t + 1) * LANES]
            ka_ref[2 * hd] = jnp.where(lane < HALF, kt, kpos_ref[hd, 0])
            ka_ref[2 * hd + 1] = jnp.where(lane >= HALF, kt, kpos_ref[hd, 1])
        (q_slab, v_slab)[s](1)

    @pl.when(pl.program_id(1) == 0)
    def _():
        carry_ref[...] = jnp.zeros_like(carry_ref)

    run = carry_ref[0:1, :]
    blocks = []
    for blk in local:
        blocks.append(blk + run)
        run = blocks[-1][MXU_DIM - 1:MXU_DIM, :]
    c = jnp.concatenate(blocks, axis=0)
    carry_ref[...] = jnp.broadcast_to(run, carry_ref.shape)
    hi, mid, lo = _three_way(c * (-LOG2E))
    part = jnp.broadcast_to(part_ref[...], (tm, LANES))
    zero = jnp.zeros_like(hi)
    feat = jnp.where(part == 0, hi, jnp.where(part == 1, mid, jnp.where(part == 2, lo, zero)))
    q_slab(2)

    for s, yn in enumerate(normed_pair(DA_WIDTH, g64_ref, 1.0 / FX_DIM)):
        for t in range(2):
            a = 2 * s + t
            kt = yn[:, t * LANES:(t + 1) * LANES]
            fa = jnp.where(lane < HALF + N_SPLIT * a, zero, jnp.where(lane < HALF + N_SPLIT * (a + 1), feat, zero))
            fb = jnp.where(lane < N_SPLIT * a, zero, jnp.where(lane < N_SPLIT * (a + 1), feat, zero))
            kf_ref[2 * a] = jnp.where(lane < HALF, kt, fa)
            kf_ref[2 * a + 1] = jnp.where(lane >= HALF, kt, fb)
        (v_slab, q_slab)[s](2 + s)
    v_slab(3)


def _in_projection(x, nm, wqt, wvt, wtok, gqt, gtok, fb, part, pat, g64, g128, tri, kpos):
    b, s, d = x.shape
    tm = TM_IN

    def v_major(rows):
        return (jax.ShapeDtypeStruct((b, DA_HEADS, s // TB, rows, TB), BF16),
                pl.BlockSpec((None, DA_HEADS, tm // TB, rows, TB), lambda bi, i: (bi, 0, i, 0, 0)))

    va_major, va_spec = v_major(DA_V_ROWS)
    vf_major, vf_spec = v_major(2 * (FX_DIM + ONES_ROWS))
    tiles = 2 * DA_HEADS
    q_major = jax.ShapeDtypeStruct((b, tiles, s // TB, LANES, TB), BF16)
    q_spec = pl.BlockSpec((None, tiles, tm // TB, LANES, TB), lambda bi, i: (bi, 0, i, 0, 0))
    k_major = jax.ShapeDtypeStruct((b, tiles, s, LANES), BF16)
    k_spec = pl.BlockSpec((None, tiles, tm, LANES), lambda bi, i: (bi, 0, i, 0))
    consts = (nm, wqt, wvt, wtok, gqt, gtok, fb, part, pat, g64, g128, tri)
    return pl.pallas_call(
        _inproj_kernel,
        grid=(b, s // tm),
        in_specs=[pl.BlockSpec((None, tm, d), lambda bi, i: (bi, i, 0))] + [_const_spec(c.shape) for c in consts]
        + [pl.BlockSpec((DA_HEADS, 2, tm, LANES), lambda bi, i: (0, 0, i, 0))],
        out_specs=[q_spec, q_spec, k_spec, k_spec, va_spec, vf_spec,
                   pl.BlockSpec((None, tm, MEM_WIDTH), lambda bi, i: (bi, i, 0))],
        out_shape=[q_major, q_major, k_major, k_major, va_major, vf_major,
                   jax.ShapeDtypeStruct((b, s, MEM_WIDTH), BF16)],
        scratch_shapes=[pltpu.VMEM((SUBLANES, LANES), F32), pltpu.VMEM((d, tm), BF16)],
        compiler_params=_params("parallel", "arbitrary", vmem_limit=VMEM_LIMIT_IN),
        name="in_projection",
    )(x, *consts, kpos)


def _memkv_kernel(mem_ref, nm_ref, w_ref, gk_ref, g128_ref, km_ref, vm_ref):
    mh = _rms_rows(mem_ref[...], nm_ref[...]).astype(BF16)
    for s in range(2):
        sl = slice(s * MXU_DIM, (s + 1) * MXU_DIM)
        y = _dot(mh, w_ref[:, sl])
        ss = _dot((y * y).astype(BF16), g128_ref[...])
        km_ref[:, sl] = (y * lax.rsqrt(ss * (1.0 / MEM_DIM) + EPS) * gk_ref[:, sl]).astype(BF16)
        vm_ref[:, sl] = _dot(mh, w_ref[:, MEM_WIDTH + s * MXU_DIM:MEM_WIDTH + (s + 1) * MXU_DIM]).astype(BF16)


def _memory_kv(mem, nm, w, gk, g128):
    b, n, d = mem.shape
    out = jax.ShapeDtypeStruct((b, n, MEM_WIDTH), BF16)
    spec = pl.BlockSpec((None, n, MEM_WIDTH), lambda bi: (bi, 0, 0))
    return pl.pallas_call(
        _memkv_kernel,
        grid=(b,),
        in_specs=[pl.BlockSpec((None, n, d), lambda bi: (bi, 0, 0)),
                  _const_spec(nm.shape), _const_spec(w.shape), _const_spec(gk.shape), _const_spec(g128.shape)],
        out_specs=[spec, spec],
        out_shape=[out, out],
        compiler_params=_params("parallel"),
        name="memory_kv",
    )(mem, nm, w, gk, g128)


def _attend(heads, z_ref):
    nb = heads[0].q.shape[1]
    stages = [(hd, i, j) for hd in heads for i in range(nb) for j in range(i + 1)]

    def scores(slot, hd, i, j):
        zmax = []
        for c in range(2):
            z = _dot(hd.k[c, j * TB:(j + 1) * TB, :], hd.q[c, i])
            if i == j:
                z = z + hd.corr[...]
            z_ref[slot, c] = z
            zmax.append(jnp.max(z, axis=0, keepdims=True))
        return zmax

    ahead = z_ref.shape[0] - 1
    pending = [scores(t, *stages[t]) for t in range(ahead)]
    m = [None, None]
    for t, (hd, i, j) in enumerate(stages):
        if t + ahead < len(stages):
            pending.append(scores((t + ahead) % (ahead + 1), *stages[t + ahead]))
        zmax = pending.pop(0)
        slot = t % (ahead + 1)
        for c in range(2):
            if j == 0:
                m[c] = zmax[c]
                hd.acc[c] = _dot(hd.v(j, c), jnp.exp2(z_ref[slot, c] - m[c]).astype(BF16))
            else:
                m_new = jnp.maximum(m[c], zmax[c])
                alpha = jnp.exp2(m[c] - m_new)
                m[c] = m_new
                hd.acc[c] = alpha * hd.acc[c] + _dot(hd.v(j, c), jnp.exp2(z_ref[slot, c] - m_new).astype(BF16))
        if j == i:
            hd.finish(i)


class _Head:
    def __init__(self, q, k, v, corr, acc, finish):
        self.q, self.k, self.v, self.corr, self.acc, self.finish = q, k, v, corr, acc, finish


def _attention_specs(s, v_rows):
    nb = s // TB
    g = HEADS_PER_STEP
    q_spec = pl.BlockSpec((None, g, 2, nb, LANES, TB), lambda bi, h: (bi, h, 0, 0, 0, 0))
    k_spec = pl.BlockSpec((None, g, 2, s, LANES), lambda bi, h: (bi, h, 0, 0, 0))
    v_spec = pl.BlockSpec((None, g, nb, v_rows, TB), lambda bi, h: (bi, h, 0, 0, 0))
    o_spec = pl.BlockSpec((None, g, s, LANES), lambda bi, h: (bi, h, 0, 0))
    return q_spec, k_spec, v_spec, o_spec


def _attention_scratch(rows):
    g = HEADS_PER_STEP
    return [pltpu.VMEM((g, 2, rows, TB), F32), pltpu.VMEM((g, TB, TB), F32), pltpu.VMEM((Z_SLOTS, 2, TB, TB), F32)]


def _diff_kernel(slope_ref, qt_ref, k_ref, vt_ref, lq1_ref, lk1_ref, lq2_ref, lk2_ref, sub_ref, o_ref,
                 acc_ref, corr_ref, z_ref, *, lam_init):
    kk = lax.broadcasted_iota(jnp.int32, (TB, TB), 0)
    qq = lax.broadcasted_iota(jnp.int32, (TB, TB), 1)
    lam = (jnp.exp(jnp.sum(lq1_ref[...] * lk1_ref[...], axis=1, keepdims=True))
           - jnp.exp(jnp.sum(lq2_ref[...] * lk2_ref[...], axis=1, keepdims=True)) + lam_init)
    gain = _lane_tile(sub_ref[...], TB // LANES) * (1.0 - lam_init)

    def head(g):
        slope = slope_ref[pl.program_id(1) * HEADS_PER_STEP + g] * LOG2E
        after = jnp.where(kk > qq, (2.0 * slope) * (qq - kk).astype(F32), 0.0)
        corr_ref[g] = jnp.where((kk // CHUNK) <= (qq // CHUNK), after, NEG)
        acc = acc_ref.at[g]

        def finish(i):
            o0 = acc[0, :DA_V_DIM, :] * (1.0 / acc[0, DA_V_DIM:DA_V_DIM + 1, :])
            o1 = acc[1, :DA_V_DIM, :] * (1.0 / acc[1, DA_V_DIM:DA_V_DIM + 1, :])
            o = o0 - lam * o1
            ms = jnp.mean(o * o, axis=0, keepdims=True)
            o_ref[g, i * TB:(i + 1) * TB, :] = (o * lax.rsqrt(ms + EPS) * gain).T.astype(BF16)

        return _Head(qt_ref.at[g], k_ref.at[g], lambda j, c: vt_ref[g, j, :DA_V_DIM + ONES_ROWS, :], corr_ref.at[g],
                     acc, finish)

    _attend([head(g) for g in range(HEADS_PER_STEP)], z_ref)


def _diff_attention(qt, k, vt, slopes, lq1, lk1, lq2, lk2, sub, lam_init):
    b, nh, _, nb, _, _ = qt.shape
    s = nb * TB
    rows = vt.shape[3]
    q_spec, k_spec, v_spec, o_spec = _attention_specs(s, rows)
    vec_spec = _const_spec(lq1.shape)
    return pl.pallas_call(
        functools.partial(_diff_kernel, lam_init=lam_init),
        grid=(b, nh // HEADS_PER_STEP),
        in_specs=[pl.BlockSpec(memory_space=pltpu.SMEM), q_spec, k_spec, v_spec,
                  vec_spec, vec_spec, vec_spec, vec_spec, _const_spec(sub.shape)],
        out_specs=o_spec,
        out_shape=jax.ShapeDtypeStruct((b, nh, s, LANES), BF16),
        scratch_shapes=_attention_scratch(DA_V_DIM + ONES_ROWS),
        compiler_params=_params("parallel", "parallel"),
        name="diff_attention",
    )(slopes, qt, k, vt, lq1, lk1, lq2, lk2, sub)


def _fox_kernel(qt_ref, k_ref, vt_ref, o_ref, acc_ref, corr_ref, z_ref):
    kk = lax.broadcasted_iota(jnp.int32, (TB, TB), 0)
    qq = lax.broadcasted_iota(jnp.int32, (TB, TB), 1)
    corr_ref[0] = jnp.where(kk <= qq, 0.0, NEG)
    rows = acc_ref.shape[2]

    def head(g):
        acc = acc_ref.at[g]

        def finish(i):
            pair = [acc[hh, :FX_DIM, :] * (1.0 / acc[hh, FX_DIM:FX_DIM + 1, :]) for hh in range(2)]
            o_ref[g, i * TB:(i + 1) * TB, :] = jnp.concatenate(pair, axis=0).T.astype(BF16)

        return _Head(qt_ref.at[g], k_ref.at[g], lambda j, hh: vt_ref[g, j, hh * rows:(hh + 1) * rows, :],
                     corr_ref.at[0], acc, finish)

    _attend([head(g) for g in range(HEADS_PER_STEP)], z_ref)


def _fox_attention(qt, k, vt):
    b, npair, _, nb, _, _ = qt.shape
    s = nb * TB
    rows = vt.shape[3] // 2
    q_spec, k_spec, v_spec, o_spec = _attention_specs(s, 2 * rows)
    return pl.pallas_call(
        _fox_kernel,
        grid=(b, npair // HEADS_PER_STEP),
        in_specs=[q_spec, k_spec, v_spec],
        out_specs=o_spec,
        out_shape=jax.ShapeDtypeStruct((b, npair, s, LANES), BF16),
        scratch_shapes=_attention_scratch(rows),
        compiler_params=_params("parallel", "parallel"),
        name="fox_attention",
    )(qt, k, vt)


def _merge_kernel(x_ref, nm_ref, oa_ref, of_ref, qm_ref, km_ref, vm_ref, wg_ref, bg_ref,
                  wda_ref, wfx_ref, wmem_ref, wout_ref, o_ref, om_ref, merged_ref):
    x = x_ref[...]
    h = _rms_rows(x, nm_ref[...]).astype(BF16)

    for hd in range(MEM_HEADS):
        sl = slice(hd * MEM_DIM, (hd + 1) * MEM_DIM)
        s = _dot_nt(qm_ref[:, sl], km_ref[:, sl])
        p = jnp.exp2(s - jnp.max(s, axis=1, keepdims=True))
        inv = 1.0 / jnp.sum(p, axis=1, keepdims=True)
        om_ref[:, sl] = (_dot(p.astype(BF16), vm_ref[:, sl]) * inv).astype(BF16)

    oa = jnp.concatenate([oa_ref[hd] for hd in range(oa_ref.shape[0])], axis=1)
    of = jnp.concatenate([of_ref[hd] for hd in range(of_ref.shape[0])], axis=1)
    branches = ((oa, wda_ref), (of, wfx_ref), (om_ref[...], wmem_ref))
    for n in range(D_MODEL // MXU_DIM):
        sl = slice(n * MXU_DIM, (n + 1) * MXU_DIM)
        merged = None
        for br, (o_br, w_br) in enumerate(branches):
            gsl = slice(br * D_MODEL + n * MXU_DIM, br * D_MODEL + (n + 1) * MXU_DIM)
            gate = jax.nn.sigmoid(_dot(h, wg_ref[:, gsl]) + bg_ref[:, gsl])
            term = gate * _dot(o_br, w_br[:, sl])
            merged = term if merged is None else merged + term
        merged_ref[:, sl] = merged.astype(BF16)

    o_ref[...] = x + _dot(merged_ref[...], wout_ref[...])


def _merge(x, nm, oa, of, qm, km, vm, wg, bg, wda, wfx, wmem, wout):
    b, s, d = x.shape
    tm = TM_PROJ
    row = lambda width: pl.BlockSpec((None, tm, width), lambda bi, i: (bi, i, 0))
    heads = pl.BlockSpec((None, oa.shape[1], tm, LANES), lambda bi, i: (bi, 0, i, 0))
    mem_spec = pl.BlockSpec((None, N_MEM, MEM_WIDTH), lambda bi, i: (bi, 0, 0))
    return pl.pallas_call(
        _merge_kernel,
        grid=(b, s // tm),
        in_specs=[row(d), _const_spec(nm.shape), heads, heads, row(MEM_WIDTH), mem_spec, mem_spec,
                  _const_spec(wg.shape), _const_spec(bg.shape), _const_spec(wda.shape), _const_spec(wfx.shape),
                  _const_spec(wmem.shape), _const_spec(wout.shape)],
        out_specs=row(d),
        out_shape=jax.ShapeDtypeStruct((b, s, d), F32),
        scratch_shapes=[pltpu.VMEM((tm, MEM_WIDTH), BF16), pltpu.VMEM((tm, d), BF16)],
        compiler_params=_params("parallel", "parallel"),
        name="merge",
    )(x, nm, oa, of, qm, km, vm, wg, bg, wda, wfx, wmem, wout)


def _ffn_kernel(x_ref, nf_ref, wup_ref, cw_ref, cb_ref, wdn_ref, o_ref, carry_ref, act_ref):
    tm = x_ref.shape[0]

    @pl.when(pl.program_id(1) == 0)
    def _():
        carry_ref[...] = jnp.zeros_like(carry_ref)

    x = x_ref[...]
    h2 = _rms_rows(x, nf_ref[...]).astype(BF16)
    row = lax.broadcasted_iota(jnp.int32, (SUBLANES, FC), 0)

    def conv(col):
        sl = slice(col, col + FC)
        u = _dot(h2, wup_ref[:, sl])
        prev = carry_ref[:, sl]
        carry_ref[:, sl] = u[tm - SUBLANES:tm, :]
        out = u * cw_ref[CONV_W - 1:CONV_W, sl] + cb_ref[:, sl]
        for shift in range(1, CONV_W):
            us = pltpu.roll(u, shift, 0)
            head = jnp.where(row < shift, pltpu.roll(prev, shift, 0), us[0:SUBLANES, :])
            us = jnp.concatenate([head, us[SUBLANES:, :]], axis=0)
            out = out + us * cw_ref[CONV_W - 1 - shift:CONV_W - shift, sl]
        return out

    for ch in range(D_FF // FC):
        a = conv(ch * FC)
        g = conv(D_FF + ch * FC)
        act_ref[:, ch * FC:(ch + 1) * FC] = (a * jax.nn.sigmoid(a) * g).astype(BF16)

    o_ref[...] = x + _dot(act_ref[...], wdn_ref[...])


def _ffn(x, nf, wup, cw, cb, wdn):
    b, s, d = x.shape
    tm = TM_FFN
    row = pl.BlockSpec((None, tm, d), lambda bi, i: (bi, i, 0))
    return pl.pallas_call(
        _ffn_kernel,
        grid=(b, s // tm),
        in_specs=[row, _const_spec(nf.shape), _const_spec(wup.shape), _const_spec(cw.shape), _const_spec(cb.shape),
                  _const_spec(wdn.shape)],
        out_specs=row,
        out_shape=jax.ShapeDtypeStruct((b, s, d), F32),
        scratch_shapes=[pltpu.VMEM((SUBLANES, 2 * D_FF), F32), pltpu.VMEM((tm, D_FF), BF16)],
        compiler_params=_params("parallel", "arbitrary"),
        name="conv_mlp",
    )(x, nf, wup, cw, cb, wdn)


def _block_diag_ones(group):
    idx = np.arange(MXU_DIM) // group
    return jnp.asarray(idx[:, None] == idx[None, :], dtype=BF16)


def _alibi_key_table(s):
    slopes = np.asarray([2.0 ** (-8.0 * (i + 1) / DA_HEADS) * LOG2E for i in range(DA_HEADS)], np.float32)
    rest = np.arange(s, dtype=np.float32)[None, :] * slopes[:, None]
    table = np.zeros((DA_HEADS, 2, s, LANES), BF16)
    for p in range(N_SPLIT):
        piece = rest.astype(BF16)
        rest = rest - piece.astype(np.float32)
        table[:, 0, :, HALF + p] = piece
        table[:, 1, :, p] = piece
    return table


def _gate_lane_layout():
    head = np.full((LANES,), -1, np.int32)
    part = np.full((LANES,), N_SPLIT, np.int32)
    for a in range(FX_HEADS // 2):
        for p in range(N_SPLIT):
            head[HALF + N_SPLIT * a + p] = 2 * a
            head[N_SPLIT * a + p] = 2 * a + 1
            part[HALF + N_SPLIT * a + p] = p
            part[N_SPLIT * a + p] = p
    return head, part


def kernel(x, mem, norm_mix, w_in, b_gate, da_q_norm, da_k_norm, da_lambda_q1, da_lambda_k1, da_lambda_q2,
           da_lambda_k2, da_subln, fx_q_norm, fx_k_norm, fx_f_bias, mem_norm, w_mem_kv, mem_q_norm, mem_k_norm,
           w_branch_da, w_branch_fx, w_branch_mem, w_out, norm_ffn, w_up, conv_w, conv_b, w_down):
    b, s, d = x.shape
    depth = w_in.shape[0]
    off = np.cumsum(np.array(IN_SIZES))[:-1].tolist()
    slopes = jnp.asarray([2.0 ** (-8.0 * (i + 1) / DA_HEADS) for i in range(DA_HEADS)], dtype=F32)
    g64 = _block_diag_ones(DA_QK_DIM)
    g128 = _block_diag_ones(MEM_DIM)
    tri = jnp.asarray(np.tril(np.ones((MXU_DIM, MXU_DIM))), dtype=BF16)
    lane_head, lane_part = _gate_lane_layout()
    lane_used = jnp.asarray(lane_head >= 0)
    lane_src = jnp.asarray(np.maximum(lane_head, 0))
    part = jnp.asarray(lane_part).reshape(1, LANES)
    kpos = jnp.asarray(_alibi_key_table(s))
    pat_np = np.zeros((FX_HEADS // 2, HALF, TM_IN), np.float32)
    for a in range(FX_HEADS // 2):
        pat_np[a, N_SPLIT * a:N_SPLIT * (a + 1), :] = 1.0
    pat = jnp.asarray(pat_np, dtype=BF16)
    row = lambda v: v.reshape(1, -1).astype(F32)
    col = lambda v: jnp.broadcast_to(v.reshape(-1, 1).astype(F32), (v.size, LANES))

    for l in range(depth):
        lam_init = 0.8 - 0.6 * math.exp(-0.3 * l)
        a_q, a_k, a_v, f_q, f_k, f_v, f_g, m_q, w_g = jnp.split(w_in[l], off, axis=-1)
        wqt = jnp.concatenate([a_q, f_q], axis=1).T.astype(BF16)
        wvt = jnp.concatenate([a_v, f_v], axis=1).T.astype(BF16)
        wfg = jnp.where(lane_used[None, :], f_g[:, lane_src], 0.0)
        wtok = jnp.concatenate([a_k, f_k, m_q, wfg], axis=1).astype(BF16)
        fb = jnp.where(lane_used, fx_f_bias[l][lane_src], 0.0).reshape(1, LANES).astype(F32)
        gqt = col(jnp.concatenate([jnp.tile(da_q_norm[l], 2 * DA_HEADS) * (DA_QK_DIM ** -0.5 * LOG2E),
                                   jnp.tile(fx_q_norm[l], FX_HEADS) * (FX_DIM ** -0.5 * LOG2E)]))
        gtok = row(jnp.concatenate([jnp.tile(da_k_norm[l], 2 * DA_HEADS), jnp.tile(fx_k_norm[l], FX_HEADS),
                                    jnp.tile(mem_q_norm[l], MEM_HEADS) * (MEM_DIM ** -0.5 * LOG2E)]))
        gmk = row(jnp.tile(mem_k_norm[l], MEM_HEADS))

        qat, qft, ka, kf, vat, vft, qm = _in_projection(
            x, row(norm_mix[l]), wqt, wvt, wtok, gqt, gtok, fb, part, pat, g64, g128, tri, kpos)
        km, vm = _memory_kv(mem, row(mem_norm[l]), w_mem_kv[l].astype(BF16), gmk, g128)

        oa = _diff_attention(qat.reshape(b, DA_HEADS, 2, s // TB, LANES, TB), ka.reshape(b, DA_HEADS, 2, s, LANES),
                             vat, slopes, row(da_lambda_q1[l]), row(da_lambda_k1[l]), row(da_lambda_q2[l]),
                             row(da_lambda_k2[l]), col(da_subln[l]), lam_init)
        of = _fox_attention(qft.reshape(b, FX_HEADS // 2, 2, s // TB, LANES, TB),
                            kf.reshape(b, FX_HEADS // 2, 2, s, LANES), vft)

        x = _merge(x, row(norm_mix[l]), oa, of, qm, km, vm, w_g.astype(BF16), b_gate[l].reshape(1, -1).astype(F32),
                   w_branch_da[l].astype(BF16), w_branch_fx[l].astype(BF16), w_branch_mem[l].astype(BF16),
                   w_out[l].astype(BF16))
        x = _ffn(x, row(norm_ffn[l]), w_up[l].astype(BF16), conv_w[l].astype(F32), row(conv_b[l]),
                 w_down[l].astype(BF16))
    return x
```

```python
import functools
import math

import jax
import jax.numpy as jnp
import numpy as np
from jax import lax
from jax.experimental import pallas as pl
from jax.experimental.pallas import tpu as pltpu

D_MODEL = 1024
CHUNK = 64
N_MEM = 256
EPS = 1e-6

DA_HEADS = 4
DA_QK_DIM = 64
DA_V_DIM = 128
DA_WIDTH = 512
FX_HEADS = 8
FX_DIM = 64
FX_WIDTH = 512
MEM_HEADS = 4
MEM_DIM = 128
MEM_WIDTH = 512
N_BRANCH = 3
D_FF = 2816
CONV_W = 3

IN_SIZES = (512, 512, DA_WIDTH, FX_WIDTH, FX_WIDTH, FX_WIDTH, FX_HEADS, MEM_WIDTH, N_BRANCH * D_MODEL)

LANES = 128
SUBLANES = 8
MXU_DIM = 256
VMEM_LIMIT = 56 * 1024 * 1024
VMEM_LIMIT_IN = 62 * 1024 * 1024

LOG2E = 1.4426950408889634
NEG = -1e30

TM_PROJ = 1024
TM_IN = 1024
TM_FFN = 1024
TB = 256
FC = 256
N_SPLIT = 3
HALF = LANES // 2
ONES_ROWS = 16
Z_SLOTS = 3
DIFF_HEADS_PER_STEP = 2
FOX_HEADS_PER_STEP = 4

F32 = jnp.float32
BF16 = jnp.bfloat16

_NT = (((1,), (1,)), ((), ()))


def _dot(a, b):
    return jnp.dot(a, b, preferred_element_type=F32)


def _dot_nt(a, b):
    return lax.dot_general(a, b, _NT, preferred_element_type=F32)


def _rms_rows(x, g):
    ms = jnp.mean(x * x, axis=-1, keepdims=True)
    return x * lax.rsqrt(ms + EPS) * g


def _three_way(a):
    hi = a.astype(BF16)
    r = a - hi.astype(F32)
    mid = r.astype(BF16)
    lo = (r - mid.astype(F32)).astype(BF16)
    return hi, mid, lo


def _lane_tile(x, reps):
    return jnp.concatenate([x] * reps, axis=1) if reps > 1 else x


def _const_spec(shape):
    nd = len(shape)
    return pl.BlockSpec(shape, lambda *_: (0,) * nd, pipeline_mode=pl.Buffered(1))


def _params(*sem, vmem_limit=VMEM_LIMIT):
    return pltpu.CompilerParams(dimension_semantics=sem, vmem_limit_bytes=vmem_limit)


def _inproj_kernel(x_ref, nm_ref, wqt_ref, wvt_ref, wtok_ref, gqt_ref, gtok_ref, fb_ref,
                   part_ref, pat_ref, g64_ref, g128_ref, tri_ref, kpos_ref,
                   qat_ref, qft_ref, ka_ref, kf_ref, vat_ref, vft_ref, qm_ref, carry_ref, ht_ref):
    tm = x_ref.shape[0]
    nb = tm // TB
    h = _rms_rows(x_ref[...], nm_ref[...]).astype(BF16)
    ht_ref[...] = h.T
    ht = ht_ref[...]
    lane = lax.broadcasted_iota(jnp.int32, (tm, LANES), 1)

    def store_blocks(out_ref, idx, yt):
        for jb in range(nb):
            out_ref[idx, jb] = yt[:, jb * TB:(jb + 1) * TB]

    def q_slab(s):
        rows = slice(s * MXU_DIM, (s + 1) * MXU_DIM)
        yt = _dot(wqt_ref[rows, :], ht)
        y3 = yt.reshape(MXU_DIM // DA_QK_DIM, DA_QK_DIM, tm)
        inv = lax.rsqrt(jnp.mean(y3 * y3, axis=1, keepdims=True) + EPS)
        yn = ((y3 * inv).reshape(MXU_DIM, tm) * _lane_tile(gqt_ref[rows, :], tm // LANES)).astype(BF16)
        out_ref = (qat_ref, qft_ref)[s // 2]
        for t in range(2):
            hd = 2 * (s % 2) + t
            ones_rows = pat_ref[hd * (s // 2)]
            store_blocks(out_ref, 2 * hd, jnp.concatenate([yn[t * LANES:t * LANES + HALF], ones_rows], axis=0))
            store_blocks(out_ref, 2 * hd + 1, jnp.concatenate([ones_rows, yn[t * LANES + HALF:(t + 1) * LANES]], axis=0))

    ones = jnp.ones((ONES_ROWS, tm), BF16)

    def v_slab(s):
        rows = slice(s * MXU_DIM, (s + 1) * MXU_DIM)
        vt = _dot(wvt_ref[rows, :], ht).astype(BF16)
        for t in range(2):
            hd = 2 * (s % 2) + t
            if s < 2:
                store_blocks(vat_ref, hd, jnp.concatenate([vt[t * LANES:(t + 1) * LANES], ones], axis=0))
            else:
                store_blocks(vft_ref, hd, jnp.concatenate(
                    [vt[t * LANES:t * LANES + HALF], ones, vt[t * LANES + HALF:(t + 1) * LANES], ones], axis=0))


    y = _dot(h, wtok_ref[...])

    def normed_pair(col, group_ref):
        cols = [slice(col + s * MXU_DIM, col + (s + 1) * MXU_DIM) for s in range(2)]
        ms = [_dot((y[:, sl] * y[:, sl]).astype(BF16), group_ref[...]) for sl in cols]
        return [(y[:, sl] * lax.rsqrt(q + EPS) * gtok_ref[:, sl]).astype(BF16) for sl, q in zip(cols, ms)]

    fg = y[:, 2 * DA_WIDTH + MEM_WIDTH:] + fb_ref[...]
    logf = jnp.minimum(fg, 0.0) - jnp.log(1.0 + jnp.exp(-jnp.abs(fg)))
    tri = tri_ref[...]
    pieces = _three_way(logf)
    local = [sum(_dot(tri, piece[r * MXU_DIM:(r + 1) * MXU_DIM]) for piece in pieces)
             for r in range(tm // MXU_DIM)]

    q_slab(0)
    for s, yn in enumerate(normed_pair(2 * DA_WIDTH, g128_ref)):
        qm_ref[:, s * MXU_DIM:(s + 1) * MXU_DIM] = yn
    v_slab(0)

    for s, yn in enumerate(normed_pair(0, g64_ref)):
        for t in range(2):
            hd = 2 * s + t
            kt = yn[:, t * LANES:(t + 1) * LANES]
            ka_ref[2 * hd] = jnp.where(lane < HALF, kt, kpos_ref[hd, 0])
            ka_ref[2 * hd + 1] = jnp.where(lane >= HALF, kt, kpos_ref[hd, 1])
        (q_slab, v_slab)[s](1)

    @pl.when(pl.program_id(1) == 0)
    def _():
        carry_ref[...] = jnp.zeros_like(carry_ref)

    run = carry_ref[0:1, :]
    blocks = []
    for blk in local:
        blocks.append(blk + run)
        run = blocks[-1][MXU_DIM - 1:MXU_DIM, :]
    c = jnp.concatenate(blocks, axis=0)
    carry_ref[...] = jnp.broadcast_to(run, carry_ref.shape)
    hi, mid, lo = _three_way(c * (-LOG2E))
    part = jnp.broadcast_to(part_ref[...], (tm, LANES))
    zero = jnp.zeros_like(hi)
    feat = jnp.where(part == 0, hi, jnp.where(part == 1, mid, jnp.where(part == 2, lo, zero)))
    q_slab(2)

    for s, yn in enumerate(normed_pair(DA_WIDTH, g64_ref)):
        for t in range(2):
            a = 2 * s + t
            kt = yn[:, t * LANES:(t + 1) * LANES]
            fa = jnp.where(lane < HALF + N_SPLIT * a, zero, jnp.where(lane < HALF + N_SPLIT * (a + 1), feat, zero))
            fb = jnp.where(lane < N_SPLIT * a, zero, jnp.where(lane < N_SPLIT * (a + 1), feat, zero))
            kf_ref[2 * a] = jnp.where(lane < HALF, kt, fa)
            kf_ref[2 * a + 1] = jnp.where(lane >= HALF, kt, fb)
        (v_slab, q_slab)[s](2 + s)
    v_slab(3)


def _in_projection(x, nm, wqt, wvt, wtok, gqt, gtok, fb, part, pat, g64, g128, tri, kpos):
    b, s, d = x.shape
    tm = TM_IN

    def v_major(rows):
        return (jax.ShapeDtypeStruct((b, DA_HEADS, s // TB, rows, TB), BF16),
                pl.BlockSpec((None, DA_HEADS, tm // TB, rows, TB), lambda bi, i: (bi, 0, i, 0, 0)))

    va_major, va_spec = v_major(DA_V_DIM + ONES_ROWS)
    vf_major, vf_spec = v_major(2 * (FX_DIM + ONES_ROWS))
    tiles = 2 * DA_HEADS
    q_major = jax.ShapeDtypeStruct((b, tiles, s // TB, LANES, TB), BF16)
    q_spec = pl.BlockSpec((None, tiles, tm // TB, LANES, TB), lambda bi, i: (bi, 0, i, 0, 0))
    k_major = jax.ShapeDtypeStruct((b, tiles, s, LANES), BF16)
    k_spec = pl.BlockSpec((None, tiles, tm, LANES), lambda bi, i: (bi, 0, i, 0))
    consts = (nm, wqt, wvt, wtok, gqt, gtok, fb, part, pat, g64, g128, tri)
    return pl.pallas_call(
        _inproj_kernel,
        grid=(b, s // tm),
        in_specs=[pl.BlockSpec((None, tm, d), lambda bi, i: (bi, i, 0))] + [_const_spec(c.shape) for c in consts]
        + [pl.BlockSpec((DA_HEADS, 2, tm, LANES), lambda bi, i: (0, 0, i, 0))],
        out_specs=[q_spec, q_spec, k_spec, k_spec, va_spec, vf_spec,
                   pl.BlockSpec((None, tm, MEM_WIDTH), lambda bi, i: (bi, i, 0))],
        out_shape=[q_major, q_major, k_major, k_major, va_major, vf_major,
                   jax.ShapeDtypeStruct((b, s, MEM_WIDTH), BF16)],
        scratch_shapes=[pltpu.VMEM((SUBLANES, LANES), F32), pltpu.VMEM((d, tm), BF16)],
        compiler_params=_params("parallel", "arbitrary", vmem_limit=VMEM_LIMIT_IN),
        name="in_projection",
    )(x, *consts, kpos)


def _memkv_kernel(mem_ref, nm_ref, w_ref, gk_ref, g128_ref, km_ref, vm_ref):
    mh = _rms_rows(mem_ref[...], nm_ref[...]).astype(BF16)
    for s in range(2):
        sl = slice(s * MXU_DIM, (s + 1) * MXU_DIM)
        y = _dot(mh, w_ref[:, sl])
        ms = _dot((y * y).astype(BF16), g128_ref[...])
        km_ref[:, sl] = (y * lax.rsqrt(ms + EPS) * gk_ref[:, sl]).astype(BF16)
        vm_ref[:, sl] = _dot(mh, w_ref[:, MEM_WIDTH + s * MXU_DIM:MEM_WIDTH + (s + 1) * MXU_DIM]).astype(BF16)


def _memory_kv(mem, nm, w, gk, g128):
    b, n, d = mem.shape
    out = jax.ShapeDtypeStruct((b, n, MEM_WIDTH), BF16)
    spec = pl.BlockSpec((None, n, MEM_WIDTH), lambda bi: (bi, 0, 0))
    return pl.pallas_call(
        _memkv_kernel,
        grid=(b,),
        in_specs=[pl.BlockSpec((None, n, d), lambda bi: (bi, 0, 0)),
                  _const_spec(nm.shape), _const_spec(w.shape), _const_spec(gk.shape), _const_spec(g128.shape)],
        out_specs=[spec, spec],
        out_shape=[out, out],
        compiler_params=_params("parallel"),
        name="memory_kv",
    )(mem, nm, w, gk, g128)


def _attend(heads, z_ref):
    nb = heads[0].q.shape[1]
    stages = [(hd, i, j) for hd in heads for i in range(nb) for j in range(i + 1)]

    def scores(slot, hd, i, j):
        zmax = []
        for c in range(2):
            z = _dot(hd.k[c, j * TB:(j + 1) * TB, :], hd.q[c, i])
            if i == j:
                z = z + hd.corr[...]
            z_ref[slot, c] = z
            zmax.append(jnp.max(z, axis=0, keepdims=True))
        return zmax

    ahead = z_ref.shape[0] - 1
    pending = [scores(t, *stages[t]) for t in range(ahead)]
    m = [None, None]
    for t, (hd, i, j) in enumerate(stages):
        if t + ahead < len(stages):
            pending.append(scores((t + ahead) % (ahead + 1), *stages[t + ahead]))
        zmax = pending.pop(0)
        slot = t % (ahead + 1)
        for c in range(2):
            if j == 0:
                m[c] = zmax[c]
                hd.acc[c] = _dot(hd.v(j, c), jnp.exp2(z_ref[slot, c] - m[c]).astype(BF16))
            else:
                m_new = jnp.maximum(m[c], zmax[c])
                alpha = jnp.exp2(m[c] - m_new)
                m[c] = m_new
                hd.acc[c] = alpha * hd.acc[c] + _dot(hd.v(j, c), jnp.exp2(z_ref[slot, c] - m_new).astype(BF16))
        if j == i:
            hd.finish(i)


class _Head:
    def __init__(self, q, k, v, corr, acc, finish):
        self.q, self.k, self.v, self.corr, self.acc, self.finish = q, k, v, corr, acc, finish


def _attention_specs(s, v_rows, g):
    nb = s // TB
    q_spec = pl.BlockSpec((None, g, 2, nb, LANES, TB), lambda bi, h: (bi, h, 0, 0, 0, 0))
    k_spec = pl.BlockSpec((None, g, 2, s, LANES), lambda bi, h: (bi, h, 0, 0, 0))
    v_spec = pl.BlockSpec((None, g, nb, v_rows, TB), lambda bi, h: (bi, h, 0, 0, 0))
    o_spec = pl.BlockSpec((None, g, s, LANES), lambda bi, h: (bi, h, 0, 0))
    return q_spec, k_spec, v_spec, o_spec


def _attention_scratch(rows, g):
    return [pltpu.VMEM((g, 2, rows, TB), F32), pltpu.VMEM((g, TB, TB), F32), pltpu.VMEM((Z_SLOTS, 2, TB, TB), F32)]


def _diff_kernel(slope_ref, qt_ref, k_ref, vt_ref, lq1_ref, lk1_ref, lq2_ref, lk2_ref, sub_ref, o_ref,
                 acc_ref, corr_ref, z_ref, *, lam_init):
    kk = lax.broadcasted_iota(jnp.int32, (TB, TB), 0)
    qq = lax.broadcasted_iota(jnp.int32, (TB, TB), 1)
    lam = (jnp.exp(jnp.sum(lq1_ref[...] * lk1_ref[...], axis=1, keepdims=True))
           - jnp.exp(jnp.sum(lq2_ref[...] * lk2_ref[...], axis=1, keepdims=True)) + lam_init)
    gain = _lane_tile(sub_ref[...], TB // LANES) * (1.0 - lam_init)

    def head(g):
        slope = slope_ref[pl.program_id(1) * DIFF_HEADS_PER_STEP + g] * LOG2E
        after = jnp.where(kk > qq, (2.0 * slope) * (qq - kk).astype(F32), 0.0)
        corr_ref[g] = jnp.where((kk // CHUNK) <= (qq // CHUNK), after, NEG)
        acc = acc_ref.at[g]

        def finish(i):
            o0 = acc[0, :DA_V_DIM, :] * (1.0 / acc[0, DA_V_DIM:DA_V_DIM + 1, :])
            o1 = acc[1, :DA_V_DIM, :] * (1.0 / acc[1, DA_V_DIM:DA_V_DIM + 1, :])
            o = o0 - lam * o1
            ms = jnp.mean(o * o, axis=0, keepdims=True)
            o_ref[g, i * TB:(i + 1) * TB, :] = (o * lax.rsqrt(ms + EPS) * gain).T.astype(BF16)

        return _Head(qt_ref.at[g], k_ref.at[g], lambda j, c: vt_ref[g, j], corr_ref.at[g], acc, finish)

    _attend([head(g) for g in range(qt_ref.shape[0])], z_ref)


def _diff_attention(qt, k, vt, slopes, lq1, lk1, lq2, lk2, sub, lam_init):
    b, nh, _, nb, _, _ = qt.shape
    s = nb * TB
    rows = vt.shape[3]
    q_spec, k_spec, v_spec, o_spec = _attention_specs(s, rows, DIFF_HEADS_PER_STEP)
    vec_spec = _const_spec(lq1.shape)
    return pl.pallas_call(
        functools.partial(_diff_kernel, lam_init=lam_init),
        grid=(b, nh // DIFF_HEADS_PER_STEP),
        in_specs=[pl.BlockSpec(memory_space=pltpu.SMEM), q_spec, k_spec, v_spec,
                  vec_spec, vec_spec, vec_spec, vec_spec, _const_spec(sub.shape)],
        out_specs=o_spec,
        out_shape=jax.ShapeDtypeStruct((b, nh, s, LANES), BF16),
        scratch_shapes=_attention_scratch(rows, DIFF_HEADS_PER_STEP),
        compiler_params=_params("parallel", "parallel"),
        name="diff_attention",
    )(slopes, qt, k, vt, lq1, lk1, lq2, lk2, sub)


def _fox_kernel(qt_ref, k_ref, vt_ref, o_ref, acc_ref, corr_ref, z_ref):
    kk = lax.broadcasted_iota(jnp.int32, (TB, TB), 0)
    qq = lax.broadcasted_iota(jnp.int32, (TB, TB), 1)
    corr_ref[0] = jnp.where(kk <= qq, 0.0, NEG)
    rows = acc_ref.shape[2]

    def head(g):
        acc = acc_ref.at[g]

        def finish(i):
            pair = [acc[hh, :FX_DIM, :] * (1.0 / acc[hh, FX_DIM:FX_DIM + 1, :]) for hh in range(2)]
            o_ref[g, i * TB:(i + 1) * TB, :] = jnp.concatenate(pair, axis=0).T.astype(BF16)

        return _Head(qt_ref.at[g], k_ref.at[g], lambda j, hh: vt_ref[g, j, hh * rows:(hh + 1) * rows, :],
                     corr_ref.at[0], acc, finish)

    _attend([head(g) for g in range(qt_ref.shape[0])], z_ref)


def _fox_attention(qt, k, vt):
    b, npair, _, nb, _, _ = qt.shape
    s = nb * TB
    rows = vt.shape[3] // 2
    q_spec, k_spec, v_spec, o_spec = _attention_specs(s, 2 * rows, FOX_HEADS_PER_STEP)
    return pl.pallas_call(
        _fox_kernel,
        grid=(b, npair // FOX_HEADS_PER_STEP),
        in_specs=[q_spec, k_spec, v_spec],
        out_specs=o_spec,
        out_shape=jax.ShapeDtypeStruct((b, npair, s, LANES), BF16),
        scratch_shapes=_attention_scratch(rows, FOX_HEADS_PER_STEP),
        compiler_params=_params("parallel", "parallel"),
        name="fox_attention",
    )(qt, k, vt)


def _merge_kernel(x_ref, nm_ref, oa_ref, of_ref, qm_ref, km_ref, vm_ref, wg_ref, bg_ref,
                  wda_ref, wfx_ref, wmem_ref, wout_ref, o_ref, om_ref, merged_ref):
    x = x_ref[...]
    h = _rms_rows(x, nm_ref[...]).astype(BF16)

    for hd in range(MEM_HEADS):
        sl = slice(hd * MEM_DIM, (hd + 1) * MEM_DIM)
        s = _dot_nt(qm_ref[:, sl], km_ref[:, sl])
        p = jnp.exp2(s - jnp.max(s, axis=1, keepdims=True))
        inv = 1.0 / jnp.sum(p, axis=1, keepdims=True)
        om_ref[:, sl] = (_dot(p.astype(BF16), vm_ref[:, sl]) * inv).astype(BF16)

    oa = jnp.concatenate([oa_ref[hd] for hd in range(oa_ref.shape[0])], axis=1)
    of = jnp.concatenate([of_ref[hd] for hd in range(of_ref.shape[0])], axis=1)
    branches = ((oa, wda_ref), (of, wfx_ref), (om_ref[...], wmem_ref))
    for n in range(D_MODEL // MXU_DIM):
        sl = slice(n * MXU_DIM, (n + 1) * MXU_DIM)
        merged = None
        for br, (o_br, w_br) in enumerate(branches):
            gsl = slice(br * D_MODEL + n * MXU_DIM, br * D_MODEL + (n + 1) * MXU_DIM)
            gate = jax.nn.sigmoid(_dot(h, wg_ref[:, gsl]) + bg_ref[:, gsl])
            term = gate * _dot(o_br, w_br[:, sl])
            merged = term if merged is None else merged + term
        merged_ref[:, sl] = merged.astype(BF16)

    o_ref[...] = x + _dot(merged_ref[...], wout_ref[...])


def _merge(x, nm, oa, of, qm, km, vm, wg, bg, wda, wfx, wmem, wout):
    b, s, d = x.shape
    tm = TM_PROJ
    row = lambda width: pl.BlockSpec((None, tm, width), lambda bi, i: (bi, i, 0))
    heads = pl.BlockSpec((None, oa.shape[1], tm, LANES), lambda bi, i: (bi, 0, i, 0))
    mem_spec = pl.BlockSpec((None, N_MEM, MEM_WIDTH), lambda bi, i: (bi, 0, 0))
    return pl.pallas_call(
        _merge_kernel,
        grid=(b, s // tm),
        in_specs=[row(d), _const_spec(nm.shape), heads, heads, row(MEM_WIDTH), mem_spec, mem_spec,
                  _const_spec(wg.shape), _const_spec(bg.shape), _const_spec(wda.shape), _const_spec(wfx.shape),
                  _const_spec(wmem.shape), _const_spec(wout.shape)],
        out_specs=row(d),
        out_shape=jax.ShapeDtypeStruct((b, s, d), F32),
        scratch_shapes=[pltpu.VMEM((tm, MEM_WIDTH), BF16), pltpu.VMEM((tm, d), BF16)],
        compiler_params=_params("parallel", "parallel"),
        name="merge",
    )(x, nm, oa, of, qm, km, vm, wg, bg, wda, wfx, wmem, wout)


def _ffn_kernel(x_ref, nf_ref, wup_ref, cw_ref, cb_ref, wdn_ref, o_ref, carry_ref, act_ref):
    tm = x_ref.shape[0]

    @pl.when(pl.program_id(1) == 0)
    def _():
        carry_ref[...] = jnp.zeros_like(carry_ref)

    x = x_ref[...]
    h2 = _rms_rows(x, nf_ref[...]).astype(BF16)
    row = lax.broadcasted_iota(jnp.int32, (SUBLANES, FC), 0)

    def conv(col):
        sl = slice(col, col + FC)
        u = _dot(h2, wup_ref[:, sl])
        prev = carry_ref[:, sl]
        carry_ref[:, sl] = u[tm - SUBLANES:tm, :]
        out = u * cw_ref[CONV_W - 1:CONV_W, sl] + cb_ref[:, sl]
        for shift in range(1, CONV_W):
            us = pltpu.roll(u, shift, 0)
            head = jnp.where(row < shift, pltpu.roll(prev, shift, 0), us[0:SUBLANES, :])
            us = jnp.concatenate([head, us[SUBLANES:, :]], axis=0)
            out = out + us * cw_ref[CONV_W - 1 - shift:CONV_W - shift, sl]
        return out

    for ch in range(D_FF // FC):
        a = conv(ch * FC)
        g = conv(D_FF + ch * FC)
        act_ref[:, ch * FC:(ch + 1) * FC] = (a * jax.nn.sigmoid(a) * g).astype(BF16)

    o_ref[...] = x + _dot(act_ref[...], wdn_ref[...])


def _ffn(x, nf, wup, cw, cb, wdn):
    b, s, d = x.shape
    tm = TM_FFN
    row = pl.BlockSpec((None, tm, d), lambda bi, i: (bi, i, 0))
    return pl.pallas_call(
        _ffn_kernel,
        grid=(b, s // tm),
        in_specs=[row, _const_spec(nf.shape), _const_spec(wup.shape), _const_spec(cw.shape), _const_spec(cb.shape),
                  _const_spec(wdn.shape)],
        out_specs=row,
        out_shape=jax.ShapeDtypeStruct((b, s, d), F32),
        scratch_shapes=[pltpu.VMEM((SUBLANES, 2 * D_FF), F32), pltpu.VMEM((tm, D_FF), BF16)],
        compiler_params=_params("parallel", "arbitrary"),
        name="conv_mlp",
    )(x, nf, wup, cw, cb, wdn)


def _group_mean_matrix(group):
    idx = np.arange(MXU_DIM) // group
    return jnp.asarray((idx[:, None] == idx[None, :]) / group, dtype=BF16)


def _alibi_key_table(s):
    slopes = np.asarray([2.0 ** (-8.0 * (i + 1) / DA_HEADS) * LOG2E for i in range(DA_HEADS)], np.float32)
    rest = np.arange(s, dtype=np.float32)[None, :] * slopes[:, None]
    table = np.zeros((DA_HEADS, 2, s, LANES), BF16)
    for p in range(N_SPLIT):
        piece = rest.astype(BF16)
        rest = rest - piece.astype(np.float32)
        table[:, 0, :, HALF + p] = piece
        table[:, 1, :, p] = piece
    return table


def _gate_lane_layout():
    head = np.full((LANES,), -1, np.int32)
    part = np.full((LANES,), N_SPLIT, np.int32)
    for a in range(FX_HEADS // 2):
        for p in range(N_SPLIT):
            head[HALF + N_SPLIT * a + p] = 2 * a
            head[N_SPLIT * a + p] = 2 * a + 1
            part[HALF + N_SPLIT * a + p] = p
            part[N_SPLIT * a + p] = p
    return head, part


def kernel(x, mem, norm_mix, w_in, b_gate, da_q_norm, da_k_norm, da_lambda_q1, da_lambda_k1, da_lambda_q2,
           da_lambda_k2, da_subln, fx_q_norm, fx_k_norm, fx_f_bias, mem_norm, w_mem_kv, mem_q_norm, mem_k_norm,
           w_branch_da, w_branch_fx, w_branch_mem, w_out, norm_ffn, w_up, conv_w, conv_b, w_down):
    b, s, d = x.shape
    depth = w_in.shape[0]
    off = np.cumsum(np.array(IN_SIZES))[:-1].tolist()
    slopes = jnp.asarray([2.0 ** (-8.0 * (i + 1) / DA_HEADS) for i in range(DA_HEADS)], dtype=F32)
    g64 = _group_mean_matrix(DA_QK_DIM)
    g128 = _group_mean_matrix(MEM_DIM)
    tri = jnp.asarray(np.tril(np.ones((MXU_DIM, MXU_DIM))), dtype=BF16)
    lane_head, lane_part = _gate_lane_layout()
    lane_used = jnp.asarray(lane_head >= 0)
    lane_src = jnp.asarray(np.maximum(lane_head, 0))
    part = jnp.asarray(lane_part).reshape(1, LANES)
    kpos = jnp.asarray(_alibi_key_table(s))
    pat_np = np.zeros((FX_HEADS // 2, HALF, TM_IN), np.float32)
    for a in range(FX_HEADS // 2):
        pat_np[a, N_SPLIT * a:N_SPLIT * (a + 1), :] = 1.0
    pat = jnp.asarray(pat_np, dtype=BF16)
    row = lambda v: v.reshape(1, -1).astype(F32)
    col = lambda v: jnp.broadcast_to(v.reshape(-1, 1).astype(F32), (v.size, LANES))

    for l in range(depth):
        lam_init = 0.8 - 0.6 * math.exp(-0.3 * l)
        a_q, a_k, a_v, f_q, f_k, f_v, f_g, m_q, w_g = jnp.split(w_in[l], off, axis=-1)
        wqt = jnp.concatenate([a_q, f_q], axis=1).T.astype(BF16)
        wvt = jnp.concatenate([a_v, f_v], axis=1).T.astype(BF16)
        wfg = jnp.where(lane_used[None, :], f_g[:, lane_src], 0.0)
        wtok = jnp.concatenate([a_k, f_k, m_q, wfg], axis=1).astype(BF16)
        fb = jnp.where(lane_used, fx_f_bias[l][lane_src], 0.0).reshape(1, LANES).astype(F32)
        gqt = col(jnp.concatenate([jnp.tile(da_q_norm[l], 2 * DA_HEADS) * (DA_QK_DIM ** -0.5 * LOG2E),
                                   jnp.tile(fx_q_norm[l], FX_HEADS) * (FX_DIM ** -0.5 * LOG2E)]))
        gtok = row(jnp.concatenate([jnp.tile(da_k_norm[l], 2 * DA_HEADS), jnp.tile(fx_k_norm[l], FX_HEADS),
                                    jnp.tile(mem_q_norm[l], MEM_HEADS) * (MEM_DIM ** -0.5 * LOG2E)]))
        gmk = row(jnp.tile(mem_k_norm[l], MEM_HEADS))

        qat, qft, ka, kf, vat, vft, qm = _in_projection(
            x, row(norm_mix[l]), wqt, wvt, wtok, gqt, gtok, fb, part, pat, g64, g128, tri, kpos)
        km, vm = _memory_kv(mem, row(mem_norm[l]), w_mem_kv[l].astype(BF16), gmk, g128)

        oa = _diff_attention(qat.reshape(b, DA_HEADS, 2, s // TB, LANES, TB), ka.reshape(b, DA_HEADS, 2, s, LANES),
                             vat, slopes, row(da_lambda_q1[l]), row(da_lambda_k1[l]), row(da_lambda_q2[l]),
                             row(da_lambda_k2[l]), col(da_subln[l]), lam_init)
        of = _fox_attention(qft.reshape(b, FX_HEADS // 2, 2, s // TB, LANES, TB),
                            kf.reshape(b, FX_HEADS // 2, 2, s, LANES), vft)

        x = _merge(x, row(norm_mix[l]), oa, of, qm, km, vm, w_g.astype(BF16), b_gate[l].reshape(1, -1).astype(F32),
                   w_branch_da[l].astype(BF16), w_branch_fx[l].astype(BF16), w_branch_mem[l].astype(BF16),
                   w_out[l].astype(BF16))
        x = _ffn(x, row(norm_ffn[l]), w_up[l].astype(BF16), conv_w[l].astype(F32), row(conv_b[l]),
                 w_down[l].astype(BF16))
    return x
```

```python
import functools
import math

import jax
import jax.numpy as jnp
import numpy as np
from jax import lax
from jax.experimental import pallas as pl
from jax.experimental.pallas import tpu as pltpu

D_MODEL = 1024
CHUNK = 64
N_MEM = 256
EPS = 1e-6

DA_HEADS = 4
DA_QK_DIM = 64
DA_V_DIM = 128
DA_WIDTH = 512
FX_HEADS = 8
FX_DIM = 64
FX_WIDTH = 512
MEM_HEADS = 4
MEM_DIM = 128
MEM_WIDTH = 512
N_BRANCH = 3
D_FF = 2816
CONV_W = 3

IN_SIZES = (512, 512, DA_WIDTH, FX_WIDTH, FX_WIDTH, FX_WIDTH, FX_HEADS, MEM_WIDTH, N_BRANCH * D_MODEL)

LANES = 128
SUBLANES = 8
MXU_DIM = 256
VMEM_LIMIT = 56 * 1024 * 1024
VMEM_LIMIT_IN = 62 * 1024 * 1024

LOG2E = 1.4426950408889634
NEG = -1e30

TM_PROJ = 1024
TM_IN = 1024
TM_FFN = 1024
TM_MEM = 1024
TB = 256
FC = 256
N_SPLIT = 3
HALF = LANES // 2
ONES_ROWS = 16
Z_SLOTS = 3
DIFF_HEADS_PER_STEP = 2
FOX_HEADS_PER_STEP = 4

F32 = jnp.float32
BF16 = jnp.bfloat16

_NT = (((1,), (1,)), ((), ()))


def _dot(a, b):
    return jnp.dot(a, b, preferred_element_type=F32)


def _dot_nt(a, b):
    return lax.dot_general(a, b, _NT, preferred_element_type=F32)


def _rms_rows(x, g):
    ms = jnp.mean(x * x, axis=-1, keepdims=True)
    return x * lax.rsqrt(ms + EPS) * g


def _three_way(a):
    hi = a.astype(BF16)
    r = a - hi.astype(F32)
    mid = r.astype(BF16)
    lo = (r - mid.astype(F32)).astype(BF16)
    return hi, mid, lo


def _lane_tile(x, reps):
    return jnp.concatenate([x] * reps, axis=1) if reps > 1 else x


def _const_spec(shape):
    nd = len(shape)
    return pl.BlockSpec(shape, lambda *_: (0,) * nd, pipeline_mode=pl.Buffered(1))


def _params(*sem, vmem_limit=VMEM_LIMIT):
    return pltpu.CompilerParams(dimension_semantics=sem, vmem_limit_bytes=vmem_limit)


def _inproj_kernel(x_ref, nm_ref, wqt_ref, wvt_ref, wtok_ref, gqt_ref, gtok_ref, fb_ref,
                   part_ref, pat_ref, g64_ref, g128_ref, tri_ref, kpos_ref,
                   qat_ref, qft_ref, ka_ref, kf_ref, vat_ref, vft_ref, qm_ref, carry_ref, ht_ref):
    tm = x_ref.shape[0]
    nb = tm // TB
    h = _rms_rows(x_ref[...], nm_ref[...]).astype(BF16)
    ht_ref[...] = h.T
    ht = ht_ref[...]
    lane = lax.broadcasted_iota(jnp.int32, (tm, LANES), 1)

    def store_blocks(out_ref, idx, yt):
        for jb in range(nb):
            out_ref[idx, jb] = yt[:, jb * TB:(jb + 1) * TB]

    def q_slab(s):
        rows = slice(s * MXU_DIM, (s + 1) * MXU_DIM)
        yt = _dot(wqt_ref[rows, :], ht)
        y3 = yt.reshape(MXU_DIM // DA_QK_DIM, DA_QK_DIM, tm)
        inv = lax.rsqrt(jnp.mean(y3 * y3, axis=1, keepdims=True) + EPS)
        yn = ((y3 * inv).reshape(MXU_DIM, tm) * _lane_tile(gqt_ref[rows, :], tm // LANES)).astype(BF16)
        out_ref = (qat_ref, qft_ref)[s // 2]
        for t in range(2):
            hd = 2 * (s % 2) + t
            ones_rows = pat_ref[hd * (s // 2)]
            store_blocks(out_ref, 2 * hd, jnp.concatenate([yn[t * LANES:t * LANES + HALF], ones_rows], axis=0))
            store_blocks(out_ref, 2 * hd + 1, jnp.concatenate([ones_rows, yn[t * LANES + HALF:(t + 1) * LANES]], axis=0))

    ones = jnp.ones((ONES_ROWS, tm), BF16)

    def v_slab(s):
        rows = slice(s * MXU_DIM, (s + 1) * MXU_DIM)
        vt = _dot(wvt_ref[rows, :], ht).astype(BF16)
        for t in range(2):
            hd = 2 * (s % 2) + t
            if s < 2:
                store_blocks(vat_ref, hd, jnp.concatenate([vt[t * LANES:(t + 1) * LANES], ones], axis=0))
            else:
                store_blocks(vft_ref, hd, jnp.concatenate(
                    [vt[t * LANES:t * LANES + HALF], ones, vt[t * LANES + HALF:(t + 1) * LANES], ones], axis=0))


    y = _dot(h, wtok_ref[...])

    def normed_pair(col, group_ref):
        cols = [slice(col + s * MXU_DIM, col + (s + 1) * MXU_DIM) for s in range(2)]
        ms = [_dot((y[:, sl] * y[:, sl]).astype(BF16), group_ref[...]) for sl in cols]
        return [(y[:, sl] * lax.rsqrt(q + EPS) * gtok_ref[:, sl]).astype(BF16) for sl, q in zip(cols, ms)]

    fg = y[:, 2 * DA_WIDTH + MEM_WIDTH:] + fb_ref[...]
    logf = jnp.minimum(fg, 0.0) - jnp.log(1.0 + jnp.exp(-jnp.abs(fg)))
    tri = tri_ref[...]
    pieces = _three_way(logf)
    local = [sum(_dot(tri, piece[r * MXU_DIM:(r + 1) * MXU_DIM]) for piece in pieces)
             for r in range(tm // MXU_DIM)]

    q_slab(0)
    for s, yn in enumerate(normed_pair(2 * DA_WIDTH, g128_ref)):
        qm_ref[:, s * MXU_DIM:(s + 1) * MXU_DIM] = yn
    v_slab(0)

    for s, yn in enumerate(normed_pair(0, g64_ref)):
        for t in range(2):
            hd = 2 * s + t
            kt = yn[:, t * LANES:(t + 1) * LANES]
            ka_ref[2 * hd] = jnp.where(lane < HALF, kt, kpos_ref[hd, 0])
            ka_ref[2 * hd + 1] = jnp.where(lane >= HALF, kt, kpos_ref[hd, 1])
        (q_slab, v_slab)[s](1)

    @pl.when(pl.program_id(1) == 0)
    def _():
        carry_ref[...] = jnp.zeros_like(carry_ref)

    run = carry_ref[0:1, :]
    blocks = []
    for blk in local:
        blocks.append(blk + run)
        run = blocks[-1][MXU_DIM - 1:MXU_DIM, :]
    c = jnp.concatenate(blocks, axis=0)
    carry_ref[...] = jnp.broadcast_to(run, carry_ref.shape)
    hi, mid, lo = _three_way(c * (-LOG2E))
    part = jnp.broadcast_to(part_ref[...], (tm, LANES))
    zero = jnp.zeros_like(hi)
    feat = jnp.where(part == 0, hi, jnp.where(part == 1, mid, jnp.where(part == 2, lo, zero)))
    q_slab(2)

    for s, yn in enumerate(normed_pair(DA_WIDTH, g64_ref)):
        for t in range(2):
            a = 2 * s + t
            kt = yn[:, t * LANES:(t + 1) * LANES]
            fa = jnp.where(lane < HALF + N_SPLIT * a, zero, jnp.where(lane < HALF + N_SPLIT * (a + 1), feat, zero))
            fb = jnp.where(lane < N_SPLIT * a, zero, jnp.where(lane < N_SPLIT * (a + 1), feat, zero))
            kf_ref[2 * a] = jnp.where(lane < HALF, kt, fa)
            kf_ref[2 * a + 1] = jnp.where(lane >= HALF, kt, fb)
        (v_slab, q_slab)[s](2 + s)
    v_slab(3)


def _in_projection(x, nm, wqt, wvt, wtok, gqt, gtok, fb, part, pat, g64, g128, tri, kpos):
    b, s, d = x.shape
    tm = TM_IN

    def v_major(rows):
        return (jax.ShapeDtypeStruct((b, DA_HEADS, s // TB, rows, TB), BF16),
                pl.BlockSpec((None, DA_HEADS, tm // TB, rows, TB), lambda bi, i: (bi, 0, i, 0, 0)))

    va_major, va_spec = v_major(DA_V_DIM + ONES_ROWS)
    vf_major, vf_spec = v_major(2 * (FX_DIM + ONES_ROWS))
    tiles = 2 * DA_HEADS
    q_major = jax.ShapeDtypeStruct((b, tiles, s // TB, LANES, TB), BF16)
    q_spec = pl.BlockSpec((None, tiles, tm // TB, LANES, TB), lambda bi, i: (bi, 0, i, 0, 0))
    k_major = jax.ShapeDtypeStruct((b, tiles, s, LANES), BF16)
    k_spec = pl.BlockSpec((None, tiles, tm, LANES), lambda bi, i: (bi, 0, i, 0))
    consts = (nm, wqt, wvt, wtok, gqt, gtok, fb, part, pat, g64, g128, tri)
    return pl.pallas_call(
        _inproj_kernel,
        grid=(b, s // tm),
        in_specs=[pl.BlockSpec((None, tm, d), lambda bi, i: (bi, i, 0))] + [_const_spec(c.shape) for c in consts]
        + [pl.BlockSpec((DA_HEADS, 2, tm, LANES), lambda bi, i: (0, 0, i, 0))],
        out_specs=[q_spec, q_spec, k_spec, k_spec, va_spec, vf_spec,
                   pl.BlockSpec((None, tm, MEM_WIDTH), lambda bi, i: (bi, i, 0))],
        out_shape=[q_major, q_major, k_major, k_major, va_major, vf_major,
                   jax.ShapeDtypeStruct((b, s, MEM_WIDTH), BF16)],
        scratch_shapes=[pltpu.VMEM((SUBLANES, LANES), F32), pltpu.VMEM((d, tm), BF16)],
        compiler_params=_params("parallel", "arbitrary", vmem_limit=VMEM_LIMIT_IN),
        name="in_projection",
    )(x, *consts, kpos)


def _memkv_kernel(mem_ref, nm_ref, w_ref, gk_ref, g128_ref, km_ref, vm_ref):
    mh = _rms_rows(mem_ref[...], nm_ref[...]).astype(BF16)
    for s in range(2):
        sl = slice(s * MXU_DIM, (s + 1) * MXU_DIM)
        y = _dot(mh, w_ref[:, sl])
        ms = _dot((y * y).astype(BF16), g128_ref[...])
        km_ref[:, sl] = (y * lax.rsqrt(ms + EPS) * gk_ref[:, sl]).astype(BF16)
        vm_ref[:, sl] = _dot(mh, w_ref[:, MEM_WIDTH + s * MXU_DIM:MEM_WIDTH + (s + 1) * MXU_DIM]).astype(BF16)


def _memory_kv(mem, nm, w, gk, g128):
    b, n, d = mem.shape
    rows = b * n
    out = jax.ShapeDtypeStruct((rows, MEM_WIDTH), BF16)
    spec = pl.BlockSpec((TM_MEM, MEM_WIDTH), lambda i: (i, 0))
    km, vm = pl.pallas_call(
        _memkv_kernel,
        grid=(rows // TM_MEM,),
        in_specs=[pl.BlockSpec((TM_MEM, d), lambda i: (i, 0)),
                  _const_spec(nm.shape), _const_spec(w.shape), _const_spec(gk.shape), _const_spec(g128.shape)],
        out_specs=[spec, spec],
        out_shape=[out, out],
        compiler_params=_params("parallel"),
        name="memory_kv",
    )(mem.reshape(rows, d), nm, w, gk, g128)
    return km.reshape(b, n, MEM_WIDTH), vm.reshape(b, n, MEM_WIDTH)


def _attend(heads, z_ref):
    nb = heads[0].q.shape[1]
    stages = [(hd, i, j) for hd in heads for i in range(nb) for j in range(i + 1)]

    def scores(slot, hd, i, j):
        zmax = []
        for c in range(2):
            z = _dot(hd.k[c, j * TB:(j + 1) * TB, :], hd.q[c, i])
            if i == j:
                z = z + hd.corr[...]
            z_ref[slot, c] = z
            zmax.append(jnp.max(z, axis=0, keepdims=True))
        return zmax

    ahead = z_ref.shape[0] - 1
    pending = [scores(t, *stages[t]) for t in range(ahead)]
    m = [None, None]
    for t, (hd, i, j) in enumerate(stages):
        if t + ahead < len(stages):
            pending.append(scores((t + ahead) % (ahead + 1), *stages[t + ahead]))
        zmax = pending.pop(0)
        slot = t % (ahead + 1)
        for c in range(2):
            if j == 0:
                m[c] = zmax[c]
                hd.acc[c] = _dot(hd.v(j, c), jnp.exp2(z_ref[slot, c] - m[c]).astype(BF16))
            else:
                m_new = jnp.maximum(m[c], zmax[c])
                alpha = jnp.exp2(m[c] - m_new)
                m[c] = m_new
                hd.acc[c] = alpha * hd.acc[c] + _dot(hd.v(j, c), jnp.exp2(z_ref[slot, c] - m_new).astype(BF16))
        if j == i:
            hd.finish(i)


class _Head:
    def __init__(self, q, k, v, corr, acc, finish):
        self.q, self.k, self.v, self.corr, self.acc, self.finish = q, k, v, corr, acc, finish


def _attention_specs(s, v_rows, g):
    nb = s // TB
    q_spec = pl.BlockSpec((None, g, 2, nb, LANES, TB), lambda bi, h: (bi, h, 0, 0, 0, 0))
    k_spec = pl.BlockSpec((None, g, 2, s, LANES), lambda bi, h: (bi, h, 0, 0, 0))
    v_spec = pl.BlockSpec((None, g, nb, v_rows, TB), lambda bi, h: (bi, h, 0, 0, 0))
    o_spec = pl.BlockSpec((None, g, s, LANES), lambda bi, h: (bi, h, 0, 0))
    return q_spec, k_spec, v_spec, o_spec


def _attention_scratch(rows, g):
    return [pltpu.VMEM((g, 2, rows, TB), F32), pltpu.VMEM((g, TB, TB), F32), pltpu.VMEM((Z_SLOTS, 2, TB, TB), F32)]


def _diff_kernel(slope_ref, qt_ref, k_ref, vt_ref, lq1_ref, lk1_ref, lq2_ref, lk2_ref, sub_ref, o_ref,
                 acc_ref, corr_ref, z_ref, *, lam_init):
    kk = lax.broadcasted_iota(jnp.int32, (TB, TB), 0)
    qq = lax.broadcasted_iota(jnp.int32, (TB, TB), 1)
    lam = (jnp.exp(jnp.sum(lq1_ref[...] * lk1_ref[...], axis=1, keepdims=True))
           - jnp.exp(jnp.sum(lq2_ref[...] * lk2_ref[...], axis=1, keepdims=True)) + lam_init)
    gain = _lane_tile(sub_ref[...], TB // LANES) * (1.0 - lam_init)

    def head(g):
        slope = slope_ref[pl.program_id(1) * DIFF_HEADS_PER_STEP + g] * LOG2E
        after = jnp.where(kk > qq, (2.0 * slope) * (qq - kk).astype(F32), 0.0)
        corr_ref[g] = jnp.where((kk // CHUNK) <= (qq // CHUNK), after, NEG)
        acc = acc_ref.at[g]

        def finish(i):
            o0 = acc[0, :DA_V_DIM, :] * (1.0 / acc[0, DA_V_DIM:DA_V_DIM + 1, :])
            o1 = acc[1, :DA_V_DIM, :] * (1.0 / acc[1, DA_V_DIM:DA_V_DIM + 1, :])
            o = o0 - lam * o1
            ms = jnp.mean(o * o, axis=0, keepdims=True)
            o_ref[g, i * TB:(i + 1) * TB, :] = (o * lax.rsqrt(ms + EPS) * gain).T.astype(BF16)

        return _Head(qt_ref.at[g], k_ref.at[g], lambda j, c: vt_ref[g, j], corr_ref.at[g], acc, finish)

    _attend([head(g) for g in range(qt_ref.shape[0])], z_ref)


def _diff_attention(qt, k, vt, slopes, lq1, lk1, lq2, lk2, sub, lam_init):
    b, nh, _, nb, _, _ = qt.shape
    s = nb * TB
    rows = vt.shape[3]
    q_spec, k_spec, v_spec, o_spec = _attention_specs(s, rows, DIFF_HEADS_PER_STEP)
    vec_spec = _const_spec(lq1.shape)
    return pl.pallas_call(
        functools.partial(_diff_kernel, lam_init=lam_init),
        grid=(b, nh // DIFF_HEADS_PER_STEP),
        in_specs=[pl.BlockSpec(memory_space=pltpu.SMEM), q_spec, k_spec, v_spec,
                  vec_spec, vec_spec, vec_spec, vec_spec, _const_spec(sub.shape)],
        out_specs=o_spec,
        out_shape=jax.ShapeDtypeStruct((b, nh, s, LANES), BF16),
        scratch_shapes=_attention_scratch(rows, DIFF_HEADS_PER_STEP),
        compiler_params=_params("parallel", "parallel"),
        name="diff_attention",
    )(slopes, qt, k, vt, lq1, lk1, lq2, lk2, sub)


def _fox_kernel(qt_ref, k_ref, vt_ref, o_ref, acc_ref, corr_ref, z_ref):
    kk = lax.broadcasted_iota(jnp.int32, (TB, TB), 0)
    qq = lax.broadcasted_iota(jnp.int32, (TB, TB), 1)
    corr_ref[0] = jnp.where(kk <= qq, 0.0, NEG)
    rows = acc_ref.shape[2]

    def head(g):
        acc = acc_ref.at[g]

        def finish(i):
            pair = [acc[hh, :FX_DIM, :] * (1.0 / acc[hh, FX_DIM:FX_DIM + 1, :]) for hh in range(2)]
            o_ref[g, i * TB:(i + 1) * TB, :] = jnp.concatenate(pair, axis=0).T.astype(BF16)

        return _Head(qt_ref.at[g], k_ref.at[g], lambda j, hh: vt_ref[g, j, hh * rows:(hh + 1) * rows, :],
                     corr_ref.at[0], acc, finish)

    _attend([head(g) for g in range(qt_ref.shape[0])], z_ref)


def _fox_attention(qt, k, vt):
    b, npair, _, nb, _, _ = qt.shape
    s = nb * TB
    rows = vt.shape[3] // 2
    q_spec, k_spec, v_spec, o_spec = _attention_specs(s, 2 * rows, FOX_HEADS_PER_STEP)
    return pl.pallas_call(
        _fox_kernel,
        grid=(b, npair // FOX_HEADS_PER_STEP),
        in_specs=[q_spec, k_spec, v_spec],
        out_specs=o_spec,
        out_shape=jax.ShapeDtypeStruct((b, npair, s, LANES), BF16),
        scratch_shapes=_attention_scratch(rows, FOX_HEADS_PER_STEP),
        compiler_params=_params("parallel", "parallel"),
        name="fox_attention",
    )(qt, k, vt)


def _merge_kernel(x_ref, nm_ref, oa_ref, of_ref, qm_ref, km_ref, vm_ref, wg_ref, bg_ref,
                  wda_ref, wfx_ref, wmem_ref, wout_ref, o_ref, om_ref, merged_ref):
    x = x_ref[...]
    h = _rms_rows(x, nm_ref[...]).astype(BF16)

    for hd in range(MEM_HEADS):
        sl = slice(hd * MEM_DIM, (hd + 1) * MEM_DIM)
        s = _dot_nt(qm_ref[:, sl], km_ref[:, sl])
        p = jnp.exp2(s - jnp.max(s, axis=1, keepdims=True))
        inv = 1.0 / jnp.sum(p, axis=1, keepdims=True)
        om_ref[:, sl] = (_dot(p.astype(BF16), vm_ref[:, sl]) * inv).astype(BF16)

    oa = jnp.concatenate([oa_ref[hd] for hd in range(oa_ref.shape[0])], axis=1)
    of = jnp.concatenate([of_ref[hd] for hd in range(of_ref.shape[0])], axis=1)
    branches = ((oa, wda_ref), (of, wfx_ref), (om_ref[...], wmem_ref))
    for n in range(D_MODEL // MXU_DIM):
        sl = slice(n * MXU_DIM, (n + 1) * MXU_DIM)
        merged = None
        for br, (o_br, w_br) in enumerate(branches):
            gsl = slice(br * D_MODEL + n * MXU_DIM, br * D_MODEL + (n + 1) * MXU_DIM)
            gate = jax.nn.sigmoid(_dot(h, wg_ref[:, gsl]) + bg_ref[:, gsl])
            term = gate * _dot(o_br, w_br[:, sl])
            merged = term if merged is None else merged + term
        merged_ref[:, sl] = merged.astype(BF16)

    o_ref[...] = x + _dot(merged_ref[...], wout_ref[...])


def _merge(x, nm, oa, of, qm, km, vm, wg, bg, wda, wfx, wmem, wout):
    b, s, d = x.shape
    tm = TM_PROJ
    row = lambda width: pl.BlockSpec((None, tm, width), lambda bi, i: (bi, i, 0))
    heads = pl.BlockSpec((None, oa.shape[1], tm, LANES), lambda bi, i: (bi, 0, i, 0))
    mem_spec = pl.BlockSpec((None, N_MEM, MEM_WIDTH), lambda bi, i: (bi, 0, 0))
    return pl.pallas_call(
        _merge_kernel,
        grid=(b, s // tm),
        in_specs=[row(d), _const_spec(nm.shape), heads, heads, row(MEM_WIDTH), mem_spec, mem_spec,
                  _const_spec(wg.shape), _const_spec(bg.shape), _const_spec(wda.shape), _const_spec(wfx.shape),
                  _const_spec(wmem.shape), _const_spec(wout.shape)],
        out_specs=row(d),
        out_shape=jax.ShapeDtypeStruct((b, s, d), F32),
        scratch_shapes=[pltpu.VMEM((tm, MEM_WIDTH), BF16), pltpu.VMEM((tm, d), BF16)],
        compiler_params=_params("parallel", "parallel"),
        name="merge",
    )(x, nm, oa, of, qm, km, vm, wg, bg, wda, wfx, wmem, wout)


def _ffn_kernel(x_ref, nf_ref, wup_ref, cw_ref, cb_ref, wdn_ref, o_ref, carry_ref, act_ref):
    tm = x_ref.shape[0]

    @pl.when(pl.program_id(1) == 0)
    def _():
        carry_ref[...] = jnp.zeros_like(carry_ref)

    x = x_ref[...]
    half = tm // 2
    halves = [_rms_rows(x[r * half:(r + 1) * half], nf_ref[...]).astype(BF16) for r in range(2)]
    h2 = jnp.concatenate(halves, axis=0)
    row = lax.broadcasted_iota(jnp.int32, (SUBLANES, FC), 0)

    def conv(col):
        sl = slice(col, col + FC)
        if col == 0:
            u = jnp.concatenate([_dot(hh, wup_ref[:, sl]) for hh in halves], axis=0)
        else:
            u = _dot(h2, wup_ref[:, sl])
        prev = carry_ref[:, sl]
        carry_ref[:, sl] = u[tm - SUBLANES:tm, :]
        out = u * cw_ref[CONV_W - 1:CONV_W, sl] + cb_ref[:, sl]
        for shift in range(1, CONV_W):
            us = pltpu.roll(u, shift, 0)
            head = jnp.where(row < shift, pltpu.roll(prev, shift, 0), us[0:SUBLANES, :])
            us = jnp.concatenate([head, us[SUBLANES:, :]], axis=0)
            out = out + us * cw_ref[CONV_W - 1 - shift:CONV_W - shift, sl]
        return out

    for ch in range(D_FF // FC):
        a = conv(ch * FC)
        g = conv(D_FF + ch * FC)
        act_ref[:, ch * FC:(ch + 1) * FC] = (a * jax.nn.sigmoid(a) * g).astype(BF16)

    o_ref[...] = x + _dot(act_ref[...], wdn_ref[...])


def _ffn(x, nf, wup, cw, cb, wdn):
    b, s, d = x.shape
    tm = TM_FFN
    row = pl.BlockSpec((None, tm, d), lambda bi, i: (bi, i, 0))
    return pl.pallas_call(
        _ffn_kernel,
        grid=(b, s // tm),
        in_specs=[row, _const_spec(nf.shape), _const_spec(wup.shape), _const_spec(cw.shape), _const_spec(cb.shape),
                  _const_spec(wdn.shape)],
        out_specs=row,
        out_shape=jax.ShapeDtypeStruct((b, s, d), F32),
        scratch_shapes=[pltpu.VMEM((SUBLANES, 2 * D_FF), F32), pltpu.VMEM((tm, D_FF), BF16)],
        compiler_params=_params("parallel", "arbitrary"),
        name="conv_mlp",
    )(x, nf, wup, cw, cb, wdn)


def _group_mean_matrix(group):
    idx = np.arange(MXU_DIM) // group
    return jnp.asarray((idx[:, None] == idx[None, :]) / group, dtype=BF16)


def _alibi_key_table(s):
    slopes = np.asarray([2.0 ** (-8.0 * (i + 1) / DA_HEADS) * LOG2E for i in range(DA_HEADS)], np.float32)
    rest = np.arange(s, dtype=np.float32)[None, :] * slopes[:, None]
    table = np.zeros((DA_HEADS, 2, s, LANES), BF16)
    for p in range(N_SPLIT):
        piece = rest.astype(BF16)
        rest = rest - piece.astype(np.float32)
        table[:, 0, :, HALF + p] = piece
        table[:, 1, :, p] = piece
    return table


def _gate_lane_layout():
    head = np.full((LANES,), -1, np.int32)
    part = np.full((LANES,), N_SPLIT, np.int32)
    for a in range(FX_HEADS // 2):
        for p in range(N_SPLIT):
            head[HALF + N_SPLIT * a + p] = 2 * a
            head[N_SPLIT * a + p] = 2 * a + 1
            part[HALF + N_SPLIT * a + p] = p
            part[N_SPLIT * a + p] = p
    return head, part


def kernel(x, mem, norm_mix, w_in, b_gate, da_q_norm, da_k_norm, da_lambda_q1, da_lambda_k1, da_lambda_q2,
           da_lambda_k2, da_subln, fx_q_norm, fx_k_norm, fx_f_bias, mem_norm, w_mem_kv, mem_q_norm, mem_k_norm,
           w_branch_da, w_branch_fx, w_branch_mem, w_out, norm_ffn, w_up, conv_w, conv_b, w_down):
    b, s, d = x.shape
    depth = w_in.shape[0]
    off = np.cumsum(np.array(IN_SIZES))[:-1].tolist()
    slopes = jnp.asarray([2.0 ** (-8.0 * (i + 1) / DA_HEADS) for i in range(DA_HEADS)], dtype=F32)
    g64 = _group_mean_matrix(DA_QK_DIM)
    g128 = _group_mean_matrix(MEM_DIM)
    tri = jnp.asarray(np.tril(np.ones((MXU_DIM, MXU_DIM))), dtype=BF16)
    lane_head, lane_part = _gate_lane_layout()
    lane_used = jnp.asarray(lane_head >= 0)
    lane_src = jnp.asarray(np.maximum(lane_head, 0))
    part = jnp.asarray(lane_part).reshape(1, LANES)
    kpos = jnp.asarray(_alibi_key_table(s))
    pat_np = np.zeros((FX_HEADS // 2, HALF, TM_IN), np.float32)
    for a in range(FX_HEADS // 2):
        pat_np[a, N_SPLIT * a:N_SPLIT * (a + 1), :] = 1.0
    pat = jnp.asarray(pat_np, dtype=BF16)
    row = lambda v: v.reshape(1, -1).astype(F32)
    col = lambda v: jnp.broadcast_to(v.reshape(-1, 1).astype(F32), (v.size, LANES))

    for l in range(depth):
        lam_init = 0.8 - 0.6 * math.exp(-0.3 * l)
        a_q, a_k, a_v, f_q, f_k, f_v, f_g, m_q, w_g = jnp.split(w_in[l], off, axis=-1)
        wqt = jnp.concatenate([a_q, f_q], axis=1).T.astype(BF16)
        wvt = jnp.concatenate([a_v, f_v], axis=1).T.astype(BF16)
        wfg = jnp.where(lane_used[None, :], f_g[:, lane_src], 0.0)
        wtok = jnp.concatenate([a_k, f_k, m_q, wfg], axis=1).astype(BF16)
        fb = jnp.where(lane_used, fx_f_bias[l][lane_src], 0.0).reshape(1, LANES).astype(F32)
        gqt = col(jnp.concatenate([jnp.tile(da_q_norm[l], 2 * DA_HEADS) * (DA_QK_DIM ** -0.5 * LOG2E),
                                   jnp.tile(fx_q_norm[l], FX_HEADS) * (FX_DIM ** -0.5 * LOG2E)]))
        gtok = row(jnp.concatenate([jnp.tile(da_k_norm[l], 2 * DA_HEADS), jnp.tile(fx_k_norm[l], FX_HEADS),
                                    jnp.tile(mem_q_norm[l], MEM_HEADS) * (MEM_DIM ** -0.5 * LOG2E)]))
        gmk = row(jnp.tile(mem_k_norm[l], MEM_HEADS))

        qat, qft, ka, kf, vat, vft, qm = _in_projection(
            x, row(norm_mix[l]), wqt, wvt, wtok, gqt, gtok, fb, part, pat, g64, g128, tri, kpos)
        km, vm = _memory_kv(mem, row(mem_norm[l]), w_mem_kv[l].astype(BF16), gmk, g128)

        oa = _diff_attention(qat.reshape(b, DA_HEADS, 2, s // TB, LANES, TB), ka.reshape(b, DA_HEADS, 2, s, LANES),
                             vat, slopes, row(da_lambda_q1[l]), row(da_lambda_k1[l]), row(da_lambda_q2[l]),
                             row(da_lambda_k2[l]), col(da_subln[l]), lam_init)
        of = _fox_attention(qft.reshape(b, FX_HEADS // 2, 2, s // TB, LANES, TB),
                            kf.reshape(b, FX_HEADS // 2, 2, s, LANES), vft)

        x = _merge(x, row(norm_mix[l]), oa, of, qm, km, vm, w_g.astype(BF16), b_gate[l].reshape(1, -1).astype(F32),
                   w_branch_da[l].astype(BF16), w_branch_fx[l].astype(BF16), w_branch_mem[l].astype(BF16),
                   w_out[l].astype(BF16))
        x = _ffn(x, row(norm_ffn[l]), w_up[l].astype(BF16), conv_w[l].astype(F32), row(conv_b[l]),
                 w_down[l].astype(BF16))
    return x
```

```python
import functools
import math

import jax
import jax.numpy as jnp
import numpy as np
from jax import lax
from jax.experimental import pallas as pl
from jax.experimental.pallas import tpu as pltpu

D_MODEL = 1024
CHUNK = 64
N_MEM = 256
EPS = 1e-6

DA_HEADS = 4
DA_QK_DIM = 64
DA_V_DIM = 128
DA_WIDTH = 512
FX_HEADS = 8
FX_DIM = 64
FX_WIDTH = 512
MEM_HEADS = 4
MEM_DIM = 128
MEM_WIDTH = 512
N_BRANCH = 3
D_FF = 2816
CONV_W = 3

IN_SIZES = (512, 512, DA_WIDTH, FX_WIDTH, FX_WIDTH, FX_WIDTH, FX_HEADS, MEM_WIDTH, N_BRANCH * D_MODEL)

LANES = 128
SUBLANES = 8
MXU_DIM = 256
VMEM_LIMIT = 56 * 1024 * 1024
VMEM_LIMIT_IN = 62 * 1024 * 1024

LOG2E = 1.4426950408889634
NEG = -1e30

TM_PROJ = 1024
TM_IN = 1024
TM_FFN = 1024
TM_MEM = 1024
TB = 256
FC = 256
N_SPLIT = 3
HALF = LANES // 2
ONES_ROWS = 16
Z_SLOTS = 3
DIFF_HEADS_PER_STEP = 2
FOX_HEADS_PER_STEP = 4

F32 = jnp.float32
BF16 = jnp.bfloat16

_NT = (((1,), (1,)), ((), ()))


def _dot(a, b):
    return jnp.dot(a, b, preferred_element_type=F32)


def _dot_nt(a, b):
    return lax.dot_general(a, b, _NT, preferred_element_type=F32)


def _rms_rows(x, g):
    ms = jnp.mean(x * x, axis=-1, keepdims=True)
    return x * lax.rsqrt(ms + EPS) * g


def _three_way(a):
    hi = a.astype(BF16)
    r = a - hi.astype(F32)
    mid = r.astype(BF16)
    lo = (r - mid.astype(F32)).astype(BF16)
    return hi, mid, lo


def _lane_tile(x, reps):
    return jnp.concatenate([x] * reps, axis=1) if reps > 1 else x


def _const_spec(shape):
    nd = len(shape)
    return pl.BlockSpec(shape, lambda *_: (0,) * nd, pipeline_mode=pl.Buffered(1))


def _params(*sem, vmem_limit=VMEM_LIMIT):
    return pltpu.CompilerParams(dimension_semantics=sem, vmem_limit_bytes=vmem_limit)


def _inproj_kernel(x_ref, nm_ref, wqt_ref, wvt_ref, wtok_ref, gqt_ref, gtok_ref, fb_ref,
                   part_ref, pat_ref, g64_ref, g128_ref, tri_ref, kpos_ref,
                   qat_ref, qft_ref, ka_ref, kf_ref, vat_ref, vft_ref, qm_ref, carry_ref, ht_ref):
    tm = x_ref.shape[0]
    nb = tm // TB
    half = tm // 2
    halves = [_rms_rows(x_ref[r * half:(r + 1) * half, :], nm_ref[...]).astype(BF16) for r in range(2)]
    for r in range(2):
        ht_ref[:, r * half:(r + 1) * half] = halves[r].T
    ht = ht_ref[...]
    lane = lax.broadcasted_iota(jnp.int32, (tm, LANES), 1)

    def store_blocks(out_ref, idx, yt):
        for jb in range(nb):
            out_ref[idx, jb] = yt[:, jb * TB:(jb + 1) * TB]

    def q_slab(s):
        rows = slice(s * MXU_DIM, (s + 1) * MXU_DIM)
        yt = _dot(wqt_ref[rows, :], ht)
        y3 = yt.reshape(MXU_DIM // DA_QK_DIM, DA_QK_DIM, tm)
        inv = lax.rsqrt(jnp.mean(y3 * y3, axis=1, keepdims=True) + EPS)
        yn = ((y3 * inv).reshape(MXU_DIM, tm) * _lane_tile(gqt_ref[rows, :], tm // LANES)).astype(BF16)
        out_ref = (qat_ref, qft_ref)[s // 2]
        for t in range(2):
            hd = 2 * (s % 2) + t
            ones_rows = pat_ref[hd * (s // 2)]
            store_blocks(out_ref, 2 * hd, jnp.concatenate([yn[t * LANES:t * LANES + HALF], ones_rows], axis=0))
            store_blocks(out_ref, 2 * hd + 1, jnp.concatenate([ones_rows, yn[t * LANES + HALF:(t + 1) * LANES]], axis=0))

    ones = jnp.ones((ONES_ROWS, tm), BF16)

    def v_slab(s):
        rows = slice(s * MXU_DIM, (s + 1) * MXU_DIM)
        vt = _dot(wvt_ref[rows, :], ht).astype(BF16)
        for t in range(2):
            hd = 2 * (s % 2) + t
            if s < 2:
                store_blocks(vat_ref, hd, jnp.concatenate([vt[t * LANES:(t + 1) * LANES], ones], axis=0))
            else:
                store_blocks(vft_ref, hd, jnp.concatenate(
                    [vt[t * LANES:t * LANES + HALF], ones, vt[t * LANES + HALF:(t + 1) * LANES], ones], axis=0))


    y = jnp.concatenate([_dot(hh, wtok_ref[...]) for hh in halves], axis=0)

    def normed_pair(col, group_ref):
        cols = [slice(col + s * MXU_DIM, col + (s + 1) * MXU_DIM) for s in range(2)]
        ms = [_dot((y[:, sl] * y[:, sl]).astype(BF16), group_ref[...]) for sl in cols]
        return [(y[:, sl] * lax.rsqrt(q + EPS) * gtok_ref[:, sl]).astype(BF16) for sl, q in zip(cols, ms)]

    fg = y[:, 2 * DA_WIDTH + MEM_WIDTH:] + fb_ref[...]
    logf = jnp.minimum(fg, 0.0) - jnp.log(1.0 + jnp.exp(-jnp.abs(fg)))
    tri = tri_ref[...]
    pieces = _three_way(logf)
    local = [sum(_dot(tri, piece[r * MXU_DIM:(r + 1) * MXU_DIM]) for piece in pieces)
             for r in range(tm // MXU_DIM)]

    q_slab(0)
    for s, yn in enumerate(normed_pair(2 * DA_WIDTH, g128_ref)):
        qm_ref[:, s * MXU_DIM:(s + 1) * MXU_DIM] = yn
    v_slab(0)

    for s, yn in enumerate(normed_pair(0, g64_ref)):
        for t in range(2):
            hd = 2 * s + t
            kt = yn[:, t * LANES:(t + 1) * LANES]
            ka_ref[2 * hd] = jnp.where(lane < HALF, kt, kpos_ref[hd, 0])
            ka_ref[2 * hd + 1] = jnp.where(lane >= HALF, kt, kpos_ref[hd, 1])
        (q_slab, v_slab)[s](1)

    @pl.when(pl.program_id(1) == 0)
    def _():
        carry_ref[...] = jnp.zeros_like(carry_ref)

    run = carry_ref[0:1, :]
    blocks = []
    for blk in local:
        blocks.append(blk + run)
        run = blocks[-1][MXU_DIM - 1:MXU_DIM, :]
    c = jnp.concatenate(blocks, axis=0)
    carry_ref[...] = jnp.broadcast_to(run, carry_ref.shape)
    hi, mid, lo = _three_way(c * (-LOG2E))
    part = jnp.broadcast_to(part_ref[...], (tm, LANES))
    zero = jnp.zeros_like(hi)
    feat = jnp.where(part == 0, hi, jnp.where(part == 1, mid, jnp.where(part == 2, lo, zero)))
    q_slab(2)

    for s, yn in enumerate(normed_pair(DA_WIDTH, g64_ref)):
        for t in range(2):
            a = 2 * s + t
            kt = yn[:, t * LANES:(t + 1) * LANES]
            fa = jnp.where(lane < HALF + N_SPLIT * a, zero, jnp.where(lane < HALF + N_SPLIT * (a + 1), feat, zero))
            fb = jnp.where(lane < N_SPLIT * a, zero, jnp.where(lane < N_SPLIT * (a + 1), feat, zero))
            kf_ref[2 * a] = jnp.where(lane < HALF, kt, fa)
            kf_ref[2 * a + 1] = jnp.where(lane >= HALF, kt, fb)
        (v_slab, q_slab)[s](2 + s)
    v_slab(3)


def _in_projection(x, nm, wqt, wvt, wtok, gqt, gtok, fb, part, pat, g64, g128, tri, kpos):
    b, s, d = x.shape
    tm = TM_IN

    def v_major(rows):
        return (jax.ShapeDtypeStruct((b, DA_HEADS, s // TB, rows, TB), BF16),
                pl.BlockSpec((None, DA_HEADS, tm // TB, rows, TB), lambda bi, i: (bi, 0, i, 0, 0)))

    va_major, va_spec = v_major(DA_V_DIM + ONES_ROWS)
    vf_major, vf_spec = v_major(2 * (FX_DIM + ONES_ROWS))
    tiles = 2 * DA_HEADS
    q_major = jax.ShapeDtypeStruct((b, tiles, s // TB, LANES, TB), BF16)
    q_spec = pl.BlockSpec((None, tiles, tm // TB, LANES, TB), lambda bi, i: (bi, 0, i, 0, 0))
    k_major = jax.ShapeDtypeStruct((b, tiles, s, LANES), BF16)
    k_spec = pl.BlockSpec((None, tiles, tm, LANES), lambda bi, i: (bi, 0, i, 0))
    consts = (nm, wqt, wvt, wtok, gqt, gtok, fb, part, pat, g64, g128, tri)
    return pl.pallas_call(
        _inproj_kernel,
        grid=(b, s // tm),
        in_specs=[pl.BlockSpec((None, tm, d), lambda bi, i: (bi, i, 0))] + [_const_spec(c.shape) for c in consts]
        + [pl.BlockSpec((DA_HEADS, 2, tm, LANES), lambda bi, i: (0, 0, i, 0))],
        out_specs=[q_spec, q_spec, k_spec, k_spec, va_spec, vf_spec,
                   pl.BlockSpec((None, tm, MEM_WIDTH), lambda bi, i: (bi, i, 0))],
        out_shape=[q_major, q_major, k_major, k_major, va_major, vf_major,
                   jax.ShapeDtypeStruct((b, s, MEM_WIDTH), BF16)],
        scratch_shapes=[pltpu.VMEM((SUBLANES, LANES), F32), pltpu.VMEM((d, tm), BF16)],
        compiler_params=_params("parallel", "arbitrary", vmem_limit=VMEM_LIMIT_IN),
        name="in_projection",
    )(x, *consts, kpos)


def _memkv_kernel(mem_ref, nm_ref, w_ref, gk_ref, g128_ref, km_ref, vm_ref):
    mh = _rms_rows(mem_ref[...], nm_ref[...]).astype(BF16)
    for s in range(2):
        sl = slice(s * MXU_DIM, (s + 1) * MXU_DIM)
        y = _dot(mh, w_ref[:, sl])
        ms = _dot((y * y).astype(BF16), g128_ref[...])
        km_ref[:, sl] = (y * lax.rsqrt(ms + EPS) * gk_ref[:, sl]).astype(BF16)
        vm_ref[:, sl] = _dot(mh, w_ref[:, MEM_WIDTH + s * MXU_DIM:MEM_WIDTH + (s + 1) * MXU_DIM]).astype(BF16)


def _memory_kv(mem, nm, w, gk, g128):
    b, n, d = mem.shape
    rows = b * n
    out = jax.ShapeDtypeStruct((rows, MEM_WIDTH), BF16)
    spec = pl.BlockSpec((TM_MEM, MEM_WIDTH), lambda i: (i, 0))
    km, vm = pl.pallas_call(
        _memkv_kernel,
        grid=(rows // TM_MEM,),
        in_specs=[pl.BlockSpec((TM_MEM, d), lambda i: (i, 0)),
                  _const_spec(nm.shape), _const_spec(w.shape), _const_spec(gk.shape), _const_spec(g128.shape)],
        out_specs=[spec, spec],
        out_shape=[out, out],
        compiler_params=_params("parallel"),
        name="memory_kv",
    )(mem.reshape(rows, d), nm, w, gk, g128)
    return km.reshape(b, n, MEM_WIDTH), vm.reshape(b, n, MEM_WIDTH)


def _attend(heads, z_ref):
    nb = heads[0].q.shape[1]
    stages = [(hd, i, j) for hd in heads for i in range(nb) for j in range(i + 1)]

    def scores(slot, hd, i, j):
        zmax = []
        for c in range(2):
            z = _dot(hd.k[c, j * TB:(j + 1) * TB, :], hd.q[c, i])
            if i == j:
                z = z + hd.corr[...]
            z_ref[slot, c] = z
            zmax.append(jnp.max(z, axis=0, keepdims=True))
        return zmax

    ahead = z_ref.shape[0] - 1
    pending = [scores(t, *stages[t]) for t in range(ahead)]
    m = [None, None]
    for t, (hd, i, j) in enumerate(stages):
        if t + ahead < len(stages):
            pending.append(scores((t + ahead) % (ahead + 1), *stages[t + ahead]))
        zmax = pending.pop(0)
        slot = t % (ahead + 1)
        for c in range(2):
            if j == 0:
                m[c] = zmax[c]
                hd.acc[c] = _dot(hd.v(j, c), jnp.exp2(z_ref[slot, c] - m[c]).astype(BF16))
            else:
                m_new = jnp.maximum(m[c], zmax[c])
                alpha = jnp.exp2(m[c] - m_new)
                m[c] = m_new
                hd.acc[c] = alpha * hd.acc[c] + _dot(hd.v(j, c), jnp.exp2(z_ref[slot, c] - m_new).astype(BF16))
        if j == i:
            hd.finish(i)


class _Head:
    def __init__(self, q, k, v, corr, acc, finish):
        self.q, self.k, self.v, self.corr, self.acc, self.finish = q, k, v, corr, acc, finish


def _attention_specs(s, v_rows, g):
    nb = s // TB
    q_spec = pl.BlockSpec((None, g, 2, nb, LANES, TB), lambda bi, h: (bi, h, 0, 0, 0, 0))
    k_spec = pl.BlockSpec((None, g, 2, s, LANES), lambda bi, h: (bi, h, 0, 0, 0))
    v_spec = pl.BlockSpec((None, g, nb, v_rows, TB), lambda bi, h: (bi, h, 0, 0, 0))
    o_spec = pl.BlockSpec((None, g, s, LANES), lambda bi, h: (bi, h, 0, 0))
    return q_spec, k_spec, v_spec, o_spec


def _attention_scratch(rows, g):
    return [pltpu.VMEM((g, 2, rows, TB), F32), pltpu.VMEM((g, TB, TB), F32), pltpu.VMEM((Z_SLOTS, 2, TB, TB), F32)]


def _diff_kernel(slope_ref, qt_ref, k_ref, vt_ref, lq1_ref, lk1_ref, lq2_ref, lk2_ref, sub_ref, o_ref,
                 acc_ref, corr_ref, z_ref, *, lam_init):
    kk = lax.broadcasted_iota(jnp.int32, (TB, TB), 0)
    qq = lax.broadcasted_iota(jnp.int32, (TB, TB), 1)
    lam = (jnp.exp(jnp.sum(lq1_ref[...] * lk1_ref[...], axis=1, keepdims=True))
           - jnp.exp(jnp.sum(lq2_ref[...] * lk2_ref[...], axis=1, keepdims=True)) + lam_init)
    gain = _lane_tile(sub_ref[...], TB // LANES) * (1.0 - lam_init)

    def head(g):
        slope = slope_ref[pl.program_id(1) * DIFF_HEADS_PER_STEP + g] * LOG2E
        after = jnp.where(kk > qq, (2.0 * slope) * (qq - kk).astype(F32), 0.0)
        corr_ref[g] = jnp.where((kk // CHUNK) <= (qq // CHUNK), after, NEG)
        acc = acc_ref.at[g]

        def finish(i):
            o0 = acc[0, :DA_V_DIM, :] * (1.0 / acc[0, DA_V_DIM:DA_V_DIM + 1, :])
            o1 = acc[1, :DA_V_DIM, :] * (1.0 / acc[1, DA_V_DIM:DA_V_DIM + 1, :])
            o = o0 - lam * o1
            ms = jnp.mean(o * o, axis=0, keepdims=True)
            o_ref[g, i * TB:(i + 1) * TB, :] = (o * lax.rsqrt(ms + EPS) * gain).T.astype(BF16)

        return _Head(qt_ref.at[g], k_ref.at[g], lambda j, c: vt_ref[g, j], corr_ref.at[g], acc, finish)

    _attend([head(g) for g in range(qt_ref.shape[0])], z_ref)


def _diff_attention(qt, k, vt, slopes, lq1, lk1, lq2, lk2, sub, lam_init):
    b, nh, _, nb, _, _ = qt.shape
    s = nb * TB
    rows = vt.shape[3]
    q_spec, k_spec, v_spec, o_spec = _attention_specs(s, rows, DIFF_HEADS_PER_STEP)
    vec_spec = _const_spec(lq1.shape)
    return pl.pallas_call(
        functools.partial(_diff_kernel, lam_init=lam_init),
        grid=(b, nh // DIFF_HEADS_PER_STEP),
        in_specs=[pl.BlockSpec(memory_space=pltpu.SMEM), q_spec, k_spec, v_spec,
                  vec_spec, vec_spec, vec_spec, vec_spec, _const_spec(sub.shape)],
        out_specs=o_spec,
        out_shape=jax.ShapeDtypeStruct((b, nh, s, LANES), BF16),
        scratch_shapes=_attention_scratch(rows, DIFF_HEADS_PER_STEP),
        compiler_params=_params("parallel", "parallel"),
        name="diff_attention",
    )(slopes, qt, k, vt, lq1, lk1, lq2, lk2, sub)


def _fox_kernel(qt_ref, k_ref, vt_ref, o_ref, acc_ref, corr_ref, z_ref):
    kk = lax.broadcasted_iota(jnp.int32, (TB, TB), 0)
    qq = lax.broadcasted_iota(jnp.int32, (TB, TB), 1)
    corr_ref[0] = jnp.where(kk <= qq, 0.0, NEG)
    rows = acc_ref.shape[2]

    def head(g):
        acc = acc_ref.at[g]

        def finish(i):
            pair = [acc[hh, :FX_DIM, :] * (1.0 / acc[hh, FX_DIM:FX_DIM + 1, :]) for hh in range(2)]
            o_ref[g, i * TB:(i + 1) * TB, :] = jnp.concatenate(pair, axis=0).T.astype(BF16)

        return _Head(qt_ref.at[g], k_ref.at[g], lambda j, hh: vt_ref[g, j, hh * rows:(hh + 1) * rows, :],
                     corr_ref.at[0], acc, finish)

    _attend([head(g) for g in range(qt_ref.shape[0])], z_ref)


def _fox_attention(qt, k, vt):
    b, npair, _, nb, _, _ = qt.shape
    s = nb * TB
    rows = vt.shape[3] // 2
    q_spec, k_spec, v_spec, o_spec = _attention_specs(s, 2 * rows, FOX_HEADS_PER_STEP)
    return pl.pallas_call(
        _fox_kernel,
        grid=(b, npair // FOX_HEADS_PER_STEP),
        in_specs=[q_spec, k_spec, v_spec],
        out_specs=o_spec,
        out_shape=jax.ShapeDtypeStruct((b, npair, s, LANES), BF16),
        scratch_shapes=_attention_scratch(rows, FOX_HEADS_PER_STEP),
        compiler_params=_params("parallel", "parallel"),
        name="fox_attention",
    )(qt, k, vt)


def _merge_kernel(x_ref, nm_ref, oa_ref, of_ref, qm_ref, km_ref, vm_ref, wg_ref, bg_ref,
                  wda_ref, wfx_ref, wmem_ref, wout_ref, o_ref, om_ref, merged_ref):
    x = x_ref[...]
    h = _rms_rows(x, nm_ref[...]).astype(BF16)

    for hd in range(MEM_HEADS):
        sl = slice(hd * MEM_DIM, (hd + 1) * MEM_DIM)
        s = _dot_nt(qm_ref[:, sl], km_ref[:, sl])
        p = jnp.exp2(s - jnp.max(s, axis=1, keepdims=True))
        inv = 1.0 / jnp.sum(p, axis=1, keepdims=True)
        om_ref[:, sl] = (_dot(p.astype(BF16), vm_ref[:, sl]) * inv).astype(BF16)

    oa = jnp.concatenate([oa_ref[hd] for hd in range(oa_ref.shape[0])], axis=1)
    of = jnp.concatenate([of_ref[hd] for hd in range(of_ref.shape[0])], axis=1)
    branches = ((oa, wda_ref), (of, wfx_ref), (om_ref[...], wmem_ref))
    for n in range(D_MODEL // MXU_DIM):
        sl = slice(n * MXU_DIM, (n + 1) * MXU_DIM)
        merged = None
        for br, (o_br, w_br) in enumerate(branches):
            gsl = slice(br * D_MODEL + n * MXU_DIM, br * D_MODEL + (n + 1) * MXU_DIM)
            gate = jax.nn.sigmoid(_dot(h, wg_ref[:, gsl]) + bg_ref[:, gsl])
            term = gate * _dot(o_br, w_br[:, sl])
            merged = term if merged is None else merged + term
        merged_ref[:, sl] = merged.astype(BF16)

    o_ref[...] = x + _dot(merged_ref[...], wout_ref[...])


def _merge(x, nm, oa, of, qm, km, vm, wg, bg, wda, wfx, wmem, wout):
    b, s, d = x.shape
    tm = TM_PROJ
    row = lambda width: pl.BlockSpec((None, tm, width), lambda bi, i: (bi, i, 0))
    heads = pl.BlockSpec((None, oa.shape[1], tm, LANES), lambda bi, i: (bi, 0, i, 0))
    mem_spec = pl.BlockSpec((None, N_MEM, MEM_WIDTH), lambda bi, i: (bi, 0, 0))
    return pl.pallas_call(
        _merge_kernel,
        grid=(b, s // tm),
        in_specs=[row(d), _const_spec(nm.shape), heads, heads, row(MEM_WIDTH), mem_spec, mem_spec,
                  _const_spec(wg.shape), _const_spec(bg.shape), _const_spec(wda.shape), _const_spec(wfx.shape),
                  _const_spec(wmem.shape), _const_spec(wout.shape)],
        out_specs=row(d),
        out_shape=jax.ShapeDtypeStruct((b, s, d), F32),
        scratch_shapes=[pltpu.VMEM((tm, MEM_WIDTH), BF16), pltpu.VMEM((tm, d), BF16)],
        compiler_params=_params("parallel", "parallel"),
        name="merge",
    )(x, nm, oa, of, qm, km, vm, wg, bg, wda, wfx, wmem, wout)


def _ffn_kernel(x_ref, nf_ref, wup_ref, cw_ref, cb_ref, wdn_ref, o_ref, carry_ref, act_ref):
    tm = x_ref.shape[0]

    @pl.when(pl.program_id(1) == 0)
    def _():
        carry_ref[...] = jnp.zeros_like(carry_ref)

    x = x_ref[...]
    half = tm // 2
    halves = [_rms_rows(x[r * half:(r + 1) * half], nf_ref[...]).astype(BF16) for r in range(2)]
    h2 = jnp.concatenate(halves, axis=0)
    row = lax.broadcasted_iota(jnp.int32, (SUBLANES, FC), 0)

    def conv(col):
        sl = slice(col, col + FC)
        if col == 0:
            u = jnp.concatenate([_dot(hh, wup_ref[:, sl]) for hh in halves], axis=0)
        else:
            u = _dot(h2, wup_ref[:, sl])
        prev = carry_ref[:, sl]
        carry_ref[:, sl] = u[tm - SUBLANES:tm, :]
        out = u * cw_ref[CONV_W - 1:CONV_W, sl] + cb_ref[:, sl]
        for shift in range(1, CONV_W):
            us = pltpu.roll(u, shift, 0)
            head = jnp.where(row < shift, pltpu.roll(prev, shift, 0), us[0:SUBLANES, :])
            us = jnp.concatenate([head, us[SUBLANES:, :]], axis=0)
            out = out + us * cw_ref[CONV_W - 1 - shift:CONV_W - shift, sl]
        return out

    for ch in range(D_FF // FC):
        a = conv(ch * FC)
        g = conv(D_FF + ch * FC)
        act_ref[:, ch * FC:(ch + 1) * FC] = (a * jax.nn.sigmoid(a) * g).astype(BF16)

    o_ref[...] = x + _dot(act_ref[...], wdn_ref[...])


def _ffn(x, nf, wup, cw, cb, wdn):
    b, s, d = x.shape
    tm = TM_FFN
    row = pl.BlockSpec((None, tm, d), lambda bi, i: (bi, i, 0))
    return pl.pallas_call(
        _ffn_kernel,
        grid=(b, s // tm),
        in_specs=[row, _const_spec(nf.shape), _const_spec(wup.shape), _const_spec(cw.shape), _const_spec(cb.shape),
                  _const_spec(wdn.shape)],
        out_specs=row,
        out_shape=jax.ShapeDtypeStruct((b, s, d), F32),
        scratch_shapes=[pltpu.VMEM((SUBLANES, 2 * D_FF), F32), pltpu.VMEM((tm, D_FF), BF16)],
        compiler_params=_params("parallel", "arbitrary"),
        name="conv_mlp",
    )(x, nf, wup, cw, cb, wdn)


def _group_mean_matrix(group):
    idx = np.arange(MXU_DIM) // group
    return jnp.asarray((idx[:, None] == idx[None, :]) / group, dtype=BF16)


def _alibi_key_table(s):
    slopes = np.asarray([2.0 ** (-8.0 * (i + 1) / DA_HEADS) * LOG2E for i in range(DA_HEADS)], np.float32)
    rest = np.arange(s, dtype=np.float32)[None, :] * slopes[:, None]
    table = np.zeros((DA_HEADS, 2, s, LANES), BF16)
    for p in range(N_SPLIT):
        piece = rest.astype(BF16)
        rest = rest - piece.astype(np.float32)
        table[:, 0, :, HALF + p] = piece
        table[:, 1, :, p] = piece
    return table


def _gate_lane_layout():
    head = np.full((LANES,), -1, np.int32)
    part = np.full((LANES,), N_SPLIT, np.int32)
    for a in range(FX_HEADS // 2):
        for p in range(N_SPLIT):
            head[HALF + N_SPLIT * a + p] = 2 * a
            head[N_SPLIT * a + p] = 2 * a + 1
            part[HALF + N_SPLIT * a + p] = p
            part[N_SPLIT * a + p] = p
    return head, part


def kernel(x, mem, norm_mix, w_in, b_gate, da_q_norm, da_k_norm, da_lambda_q1, da_lambda_k1, da_lambda_q2,
           da_lambda_k2, da_subln, fx_q_norm, fx_k_norm, fx_f_bias, mem_norm, w_mem_kv, mem_q_norm, mem_k_norm,
           w_branch_da, w_branch_fx, w_branch_mem, w_out, norm_ffn, w_up, conv_w, conv_b, w_down):
    b, s, d = x.shape
    depth = w_in.shape[0]
    off = np.cumsum(np.array(IN_SIZES))[:-1].tolist()
    slopes = jnp.asarray([2.0 ** (-8.0 * (i + 1) / DA_HEADS) for i in range(DA_HEADS)], dtype=F32)
    g64 = _group_mean_matrix(DA_QK_DIM)
    g128 = _group_mean_matrix(MEM_DIM)
    tri = jnp.asarray(np.tril(np.ones((MXU_DIM, MXU_DIM))), dtype=BF16)
    lane_head, lane_part = _gate_lane_layout()
    lane_used = jnp.asarray(lane_head >= 0)
    lane_src = jnp.asarray(np.maximum(lane_head, 0))
    part = jnp.asarray(lane_part).reshape(1, LANES)
    kpos = jnp.asarray(_alibi_key_table(s))
    pat_np = np.zeros((FX_HEADS // 2, HALF, TM_IN), np.float32)
    for a in range(FX_HEADS // 2):
        pat_np[a, N_SPLIT * a:N_SPLIT * (a + 1), :] = 1.0
    pat = jnp.asarray(pat_np, dtype=BF16)
    row = lambda v: v.reshape(1, -1).astype(F32)
    col = lambda v: jnp.broadcast_to(v.reshape(-1, 1).astype(F32), (v.size, LANES))

    for l in range(depth):
        lam_init = 0.8 - 0.6 * math.exp(-0.3 * l)
        a_q, a_k, a_v, f_q, f_k, f_v, f_g, m_q, w_g = jnp.split(w_in[l], off, axis=-1)
        wqt = jnp.concatenate([a_q, f_q], axis=1).T.astype(BF16)
        wvt = jnp.concatenate([a_v, f_v], axis=1).T.astype(BF16)
        wfg = jnp.where(lane_used[None, :], f_g[:, lane_src], 0.0)
        wtok = jnp.concatenate([a_k, f_k, m_q, wfg], axis=1).astype(BF16)
        fb = jnp.where(lane_used, fx_f_bias[l][lane_src], 0.0).reshape(1, LANES).astype(F32)
        gqt = col(jnp.concatenate([jnp.tile(da_q_norm[l], 2 * DA_HEADS) * (DA_QK_DIM ** -0.5 * LOG2E),
                                   jnp.tile(fx_q_norm[l], FX_HEADS) * (FX_DIM ** -0.5 * LOG2E)]))
        gtok = row(jnp.concatenate([jnp.tile(da_k_norm[l], 2 * DA_HEADS), jnp.tile(fx_k_norm[l], FX_HEADS),
                                    jnp.tile(mem_q_norm[l], MEM_HEADS) * (MEM_DIM ** -0.5 * LOG2E)]))
        gmk = row(jnp.tile(mem_k_norm[l], MEM_HEADS))

        qat, qft, ka, kf, vat, vft, qm = _in_projection(
            x, row(norm_mix[l]), wqt, wvt, wtok, gqt, gtok, fb, part, pat, g64, g128, tri, kpos)
        km, vm = _memory_kv(mem, row(mem_norm[l]), w_mem_kv[l].astype(BF16), gmk, g128)

        oa = _diff_attention(qat.reshape(b, DA_HEADS, 2, s // TB, LANES, TB), ka.reshape(b, DA_HEADS, 2, s, LANES),
                             vat, slopes, row(da_lambda_q1[l]), row(da_lambda_k1[l]), row(da_lambda_q2[l]),
                             row(da_lambda_k2[l]), col(da_subln[l]), lam_init)
        of = _fox_attention(qft.reshape(b, FX_HEADS // 2, 2, s // TB, LANES, TB),
                            kf.reshape(b, FX_HEADS // 2, 2, s, LANES), vft)

        x = _merge(x, row(norm_mix[l]), oa, of, qm, km, vm, w_g.astype(BF16), b_gate[l].reshape(1, -1).astype(F32),
                   w_branch_da[l].astype(BF16), w_branch_fx[l].astype(BF16), w_branch_mem[l].astype(BF16),
                   w_out[l].astype(BF16))
        x = _ffn(x, row(norm_ffn[l]), w_up[l].astype(BF16), conv_w[l].astype(F32), row(conv_b[l]),
                 w_down[l].astype(BF16))
    return x
```

```python
import functools
import math

import jax
import jax.numpy as jnp
import numpy as np
from jax import lax
from jax.experimental import pallas as pl
from jax.experimental.pallas import tpu as pltpu

D_MODEL = 1024
CHUNK = 64
N_MEM = 256
EPS = 1e-6

DA_HEADS = 4
DA_QK_DIM = 64
DA_V_DIM = 128
DA_WIDTH = 512
FX_HEADS = 8
FX_DIM = 64
FX_WIDTH = 512
MEM_HEADS = 4
MEM_DIM = 128
MEM_WIDTH = 512
N_BRANCH = 3
D_FF = 2816
CONV_W = 3

IN_SIZES = (512, 512, DA_WIDTH, FX_WIDTH, FX_WIDTH, FX_WIDTH, FX_HEADS, MEM_WIDTH, N_BRANCH * D_MODEL)

LANES = 128
SUBLANES = 8
MXU_DIM = 256
VMEM_LIMIT = 56 * 1024 * 1024
VMEM_LIMIT_IN = 62 * 1024 * 1024

LOG2E = 1.4426950408889634
NEG = -1e30

TM_PROJ = 1024
TM_IN = 1024
TM_FFN = 1024
TM_MEM = 1024
TB = 256
FC = 256
N_SPLIT = 3
HALF = LANES // 2
ONES_ROWS = 16
Z_SLOTS = 3
DIFF_HEADS_PER_STEP = 2
FOX_HEADS_PER_STEP = 4

F32 = jnp.float32
BF16 = jnp.bfloat16

_NT = (((1,), (1,)), ((), ()))


def _dot(a, b):
    return jnp.dot(a, b, preferred_element_type=F32)


def _dot_nt(a, b):
    return lax.dot_general(a, b, _NT, preferred_element_type=F32)


def _rms_rows(x, g):
    ms = jnp.mean(x * x, axis=-1, keepdims=True)
    return x * lax.rsqrt(ms + EPS) * g


def _three_way(a):
    hi = a.astype(BF16)
    r = a - hi.astype(F32)
    mid = r.astype(BF16)
    lo = (r - mid.astype(F32)).astype(BF16)
    return hi, mid, lo


def _lane_tile(x, reps):
    return jnp.concatenate([x] * reps, axis=1) if reps > 1 else x


def _const_spec(shape):
    nd = len(shape)
    return pl.BlockSpec(shape, lambda *_: (0,) * nd, pipeline_mode=pl.Buffered(1))


def _params(*sem, vmem_limit=VMEM_LIMIT):
    return pltpu.CompilerParams(dimension_semantics=sem, vmem_limit_bytes=vmem_limit)


def _inproj_kernel(x_ref, nm_ref, wqt_ref, wvt_ref, wtok_ref, gqt_ref, gtok_ref, fb_ref,
                   part_ref, g64_ref, g128_ref, tri_ref, kpos_ref,
                   qat_ref, qft_ref, ka_ref, kf_ref, vat_ref, vft_ref, qm_ref, carry_ref, ht_ref):
    tm = x_ref.shape[0]
    nb = tm // TB
    h = _rms_rows(x_ref[...], nm_ref[...]).astype(BF16)
    ht_ref[...] = h.T
    ht = ht_ref[...]
    lane = lax.broadcasted_iota(jnp.int32, (tm, LANES), 1)

    def store_blocks(out_ref, idx, yt):
        for jb in range(nb):
            out_ref[idx, jb] = yt[:, jb * TB:(jb + 1) * TB]

    def q_slab(s):
        rows = slice(s * MXU_DIM, (s + 1) * MXU_DIM)
        yt = _dot(wqt_ref[rows, :], ht)
        y3 = yt.reshape(MXU_DIM // DA_QK_DIM, DA_QK_DIM, tm)
        inv = lax.rsqrt(jnp.mean(y3 * y3, axis=1, keepdims=True) + EPS)
        yn = ((y3 * inv).reshape(MXU_DIM, tm) * _lane_tile(gqt_ref[rows, :], tm // LANES)).astype(BF16)
        out_ref = (qat_ref, qft_ref)[s // 2]
        for t in range(2):
            store_blocks(out_ref, 2 * (s % 2) + t, yn[t * LANES:(t + 1) * LANES])

    ones = jnp.ones((ONES_ROWS, tm), BF16)

    def v_slab(s):
        rows = slice(s * MXU_DIM, (s + 1) * MXU_DIM)
        vt = _dot(wvt_ref[rows, :], ht).astype(BF16)
        for t in range(2):
            hd = 2 * (s % 2) + t
            if s < 2:
                store_blocks(vat_ref, hd, jnp.concatenate([vt[t * LANES:(t + 1) * LANES], ones], axis=0))
            else:
                store_blocks(vft_ref, hd, jnp.concatenate(
                    [vt[t * LANES:t * LANES + HALF], ones, vt[t * LANES + HALF:(t + 1) * LANES], ones], axis=0))


    y = _dot(h, wtok_ref[...])

    def normed_pair(col, group_ref):
        cols = [slice(col + s * MXU_DIM, col + (s + 1) * MXU_DIM) for s in range(2)]
        ms = [_dot((y[:, sl] * y[:, sl]).astype(BF16), group_ref[...]) for sl in cols]
        return [(y[:, sl] * lax.rsqrt(q + EPS) * gtok_ref[:, sl]).astype(BF16) for sl, q in zip(cols, ms)]

    fg = y[:, 2 * DA_WIDTH + MEM_WIDTH:] + fb_ref[...]
    logf = jnp.minimum(fg, 0.0) - jnp.log(1.0 + jnp.exp(-jnp.abs(fg)))
    tri = tri_ref[...]
    pieces = _three_way(logf)
    local = [sum(_dot(tri, piece[r * MXU_DIM:(r + 1) * MXU_DIM]) for piece in pieces)
             for r in range(tm // MXU_DIM)]

    q_slab(0)
    for s, yn in enumerate(normed_pair(2 * DA_WIDTH, g128_ref)):
        qm_ref[:, s * MXU_DIM:(s + 1) * MXU_DIM] = yn
    v_slab(0)

    for s, yn in enumerate(normed_pair(0, g64_ref)):
        for t in range(2):
            hd = 2 * s + t
            kt = yn[:, t * LANES:(t + 1) * LANES]
            ka_ref[2 * hd] = jnp.where(lane < HALF, kt, kpos_ref[hd, 0])
            ka_ref[2 * hd + 1] = jnp.where(lane >= HALF, kt, kpos_ref[hd, 1])
        (q_slab, v_slab)[s](1)

    @pl.when(pl.program_id(1) == 0)
    def _():
        carry_ref[...] = jnp.zeros_like(carry_ref)

    run = carry_ref[0:1, :]
    blocks = []
    for blk in local:
        blocks.append(blk + run)
        run = blocks[-1][MXU_DIM - 1:MXU_DIM, :]
    c = jnp.concatenate(blocks, axis=0)
    carry_ref[...] = jnp.broadcast_to(run, carry_ref.shape)
    hi, mid, lo = _three_way(c * (-LOG2E))
    part = jnp.broadcast_to(part_ref[...], (tm, LANES))
    zero = jnp.zeros_like(hi)
    feat = jnp.where(part == 0, hi, jnp.where(part == 1, mid, jnp.where(part == 2, lo, zero)))
    q_slab(2)

    for s, yn in enumerate(normed_pair(DA_WIDTH, g64_ref)):
        for t in range(2):
            a = 2 * s + t
            kt = yn[:, t * LANES:(t + 1) * LANES]
            fa = jnp.where(lane < HALF + N_SPLIT * a, zero, jnp.where(lane < HALF + N_SPLIT * (a + 1), feat, zero))
            fb = jnp.where(lane < N_SPLIT * a, zero, jnp.where(lane < N_SPLIT * (a + 1), feat, zero))
            kf_ref[2 * a] = jnp.where(lane < HALF, kt, fa)
            kf_ref[2 * a + 1] = jnp.where(lane >= HALF, kt, fb)
        (v_slab, q_slab)[s](2 + s)
    v_slab(3)


def _in_projection(x, nm, wqt, wvt, wtok, gqt, gtok, fb, part, g64, g128, tri, kpos):
    b, s, d = x.shape
    tm = TM_IN

    def v_major(rows):
        return (jax.ShapeDtypeStruct((b, DA_HEADS, s // TB, rows, TB), BF16),
                pl.BlockSpec((None, DA_HEADS, tm // TB, rows, TB), lambda bi, i: (bi, 0, i, 0, 0)))

    va_major, va_spec = v_major(DA_V_DIM + ONES_ROWS)
    vf_major, vf_spec = v_major(2 * (FX_DIM + ONES_ROWS))
    tiles = 2 * DA_HEADS
    q_major, q_spec = v_major(LANES)
    k_major = jax.ShapeDtypeStruct((b, tiles, s, LANES), BF16)
    k_spec = pl.BlockSpec((None, tiles, tm, LANES), lambda bi, i: (bi, 0, i, 0))
    consts = (nm, wqt, wvt, wtok, gqt, gtok, fb, part, g64, g128, tri)
    return pl.pallas_call(
        _inproj_kernel,
        grid=(b, s // tm),
        in_specs=[pl.BlockSpec((None, tm, d), lambda bi, i: (bi, i, 0))] + [_const_spec(c.shape) for c in consts]
        + [pl.BlockSpec((DA_HEADS, 2, tm, LANES), lambda bi, i: (0, 0, i, 0))],
        out_specs=[q_spec, q_spec, k_spec, k_spec, va_spec, vf_spec,
                   pl.BlockSpec((None, tm, MEM_WIDTH), lambda bi, i: (bi, i, 0))],
        out_shape=[q_major, q_major, k_major, k_major, va_major, vf_major,
                   jax.ShapeDtypeStruct((b, s, MEM_WIDTH), BF16)],
        scratch_shapes=[pltpu.VMEM((SUBLANES, LANES), F32), pltpu.VMEM((d, tm), BF16)],
        compiler_params=_params("parallel", "arbitrary", vmem_limit=VMEM_LIMIT_IN),
        name="in_projection",
    )(x, *consts, kpos)


def _memkv_kernel(mem_ref, nm_ref, w_ref, gk_ref, g128_ref, km_ref, vm_ref):
    mh = _rms_rows(mem_ref[...], nm_ref[...]).astype(BF16)
    for s in range(2):
        sl = slice(s * MXU_DIM, (s + 1) * MXU_DIM)
        y = _dot(mh, w_ref[:, sl])
        ms = _dot((y * y).astype(BF16), g128_ref[...])
        km_ref[:, sl] = (y * lax.rsqrt(ms + EPS) * gk_ref[:, sl]).astype(BF16)
        vm_ref[:, sl] = _dot(mh, w_ref[:, MEM_WIDTH + s * MXU_DIM:MEM_WIDTH + (s + 1) * MXU_DIM]).astype(BF16)


def _memory_kv(mem, nm, w, gk, g128):
    b, n, d = mem.shape
    rows = b * n
    out = jax.ShapeDtypeStruct((rows, MEM_WIDTH), BF16)
    spec = pl.BlockSpec((TM_MEM, MEM_WIDTH), lambda i: (i, 0))
    km, vm = pl.pallas_call(
        _memkv_kernel,
        grid=(rows // TM_MEM,),
        in_specs=[pl.BlockSpec((TM_MEM, d), lambda i: (i, 0)),
                  _const_spec(nm.shape), _const_spec(w.shape), _const_spec(gk.shape), _const_spec(g128.shape)],
        out_specs=[spec, spec],
        out_shape=[out, out],
        compiler_params=_params("parallel"),
        name="memory_kv",
    )(mem.reshape(rows, d), nm, w, gk, g128)
    return km.reshape(b, n, MEM_WIDTH), vm.reshape(b, n, MEM_WIDTH)


def _attend(heads, z_ref):
    nb = heads[0].nb
    stages = [(hd, i, j) for hd in heads for i in range(nb) for j in range(i + 1)]

    def scores(slot, hd, i, j):
        zmax = []
        for c in range(2):
            z = _dot(hd.k[c, j * TB:(j + 1) * TB, :], hd.q(c, i))
            if i == j:
                z = z + hd.corr[...]
            z_ref[slot, c] = z
            zmax.append(jnp.max(z, axis=0, keepdims=True))
        return zmax

    ahead = z_ref.shape[0] - 1
    pending = [scores(t, *stages[t]) for t in range(ahead)]
    m = [None, None]
    for t, (hd, i, j) in enumerate(stages):
        if t + ahead < len(stages):
            pending.append(scores((t + ahead) % (ahead + 1), *stages[t + ahead]))
        zmax = pending.pop(0)
        slot = t % (ahead + 1)
        for c in range(2):
            if j == 0:
                m[c] = zmax[c]
                hd.acc[c] = _dot(hd.v(j, c), jnp.exp2(z_ref[slot, c] - m[c]).astype(BF16))
            else:
                m_new = jnp.maximum(m[c], zmax[c])
                alpha = jnp.exp2(m[c] - m_new)
                m[c] = m_new
                hd.acc[c] = alpha * hd.acc[c] + _dot(hd.v(j, c), jnp.exp2(z_ref[slot, c] - m_new).astype(BF16))
        if j == i:
            hd.finish(i)


class _Head:
    def __init__(self, q_ref, ones_rows, k, v, corr, acc, finish):
        self.k, self.v, self.corr, self.acc, self.finish = k, v, corr, acc, finish
        self.q_ref, self.ones_rows, self.nb = q_ref, ones_rows, q_ref.shape[0]

    def q(self, c, i):
        if c == 0:
            return jnp.concatenate([self.q_ref[i, :HALF, :], self.ones_rows], axis=0)
        return jnp.concatenate([self.ones_rows, self.q_ref[i, HALF:, :]], axis=0)


def _attention_specs(s, v_rows, g):
    nb = s // TB
    q_spec = pl.BlockSpec((None, g, nb, LANES, TB), lambda bi, h: (bi, h, 0, 0, 0))
    k_spec = pl.BlockSpec((None, g, 2, s, LANES), lambda bi, h: (bi, h, 0, 0, 0))
    v_spec = pl.BlockSpec((None, g, nb, v_rows, TB), lambda bi, h: (bi, h, 0, 0, 0))
    o_spec = pl.BlockSpec((None, g, s, LANES), lambda bi, h: (bi, h, 0, 0))
    return q_spec, k_spec, v_spec, o_spec


def _attention_scratch(rows, g):
    return [pltpu.VMEM((g, 2, rows, TB), F32), pltpu.VMEM((g, TB, TB), F32), pltpu.VMEM((Z_SLOTS, 2, TB, TB), F32)]


def _diff_kernel(slope_ref, qt_ref, k_ref, vt_ref, pat_ref, lq1_ref, lk1_ref, lq2_ref, lk2_ref, sub_ref, o_ref,
                 acc_ref, corr_ref, z_ref, *, lam_init):
    kk = lax.broadcasted_iota(jnp.int32, (TB, TB), 0)
    qq = lax.broadcasted_iota(jnp.int32, (TB, TB), 1)
    lam = (jnp.exp(jnp.sum(lq1_ref[...] * lk1_ref[...], axis=1, keepdims=True))
           - jnp.exp(jnp.sum(lq2_ref[...] * lk2_ref[...], axis=1, keepdims=True)) + lam_init)
    gain = _lane_tile(sub_ref[...], TB // LANES) * (1.0 - lam_init)

    def head(g):
        slope = slope_ref[pl.program_id(1) * DIFF_HEADS_PER_STEP + g] * LOG2E
        after = jnp.where(kk > qq, (2.0 * slope) * (qq - kk).astype(F32), 0.0)
        corr_ref[g] = jnp.where((kk // CHUNK) <= (qq // CHUNK), after, NEG)
        acc = acc_ref.at[g]

        def finish(i):
            o0 = acc[0, :DA_V_DIM, :] * (1.0 / acc[0, DA_V_DIM:DA_V_DIM + 1, :])
            o1 = acc[1, :DA_V_DIM, :] * (1.0 / acc[1, DA_V_DIM:DA_V_DIM + 1, :])
            o = o0 - lam * o1
            ms = jnp.mean(o * o, axis=0, keepdims=True)
            o_ref[g, i * TB:(i + 1) * TB, :] = (o * lax.rsqrt(ms + EPS) * gain).T.astype(BF16)

        return _Head(qt_ref.at[g], pat_ref[0], k_ref.at[g], lambda j, c: vt_ref[g, j], corr_ref.at[g], acc, finish)

    _attend([head(g) for g in range(qt_ref.shape[0])], z_ref)


def _diff_attention(qt, k, vt, pat, slopes, lq1, lk1, lq2, lk2, sub, lam_init):
    b, nh, nb, _, _ = qt.shape
    s = nb * TB
    rows = vt.shape[3]
    q_spec, k_spec, v_spec, o_spec = _attention_specs(s, rows, DIFF_HEADS_PER_STEP)
    vec_spec = _const_spec(lq1.shape)
    return pl.pallas_call(
        functools.partial(_diff_kernel, lam_init=lam_init),
        grid=(b, nh // DIFF_HEADS_PER_STEP),
        in_specs=[pl.BlockSpec(memory_space=pltpu.SMEM), q_spec, k_spec, v_spec, _const_spec(pat.shape),
                  vec_spec, vec_spec, vec_spec, vec_spec, _const_spec(sub.shape)],
        out_specs=o_spec,
        out_shape=jax.ShapeDtypeStruct((b, nh, s, LANES), BF16),
        scratch_shapes=_attention_scratch(rows, DIFF_HEADS_PER_STEP),
        compiler_params=_params("parallel", "parallel"),
        name="diff_attention",
    )(slopes, qt, k, vt, pat, lq1, lk1, lq2, lk2, sub)


def _fox_kernel(qt_ref, k_ref, vt_ref, pat_ref, o_ref, acc_ref, corr_ref, z_ref):
    kk = lax.broadcasted_iota(jnp.int32, (TB, TB), 0)
    qq = lax.broadcasted_iota(jnp.int32, (TB, TB), 1)
    corr_ref[0] = jnp.where(kk <= qq, 0.0, NEG)
    rows = acc_ref.shape[2]

    def head(g):
        acc = acc_ref.at[g]

        def finish(i):
            pair = [acc[hh, :FX_DIM, :] * (1.0 / acc[hh, FX_DIM:FX_DIM + 1, :]) for hh in range(2)]
            o_ref[g, i * TB:(i + 1) * TB, :] = jnp.concatenate(pair, axis=0).T.astype(BF16)

        ones_rows = pat_ref[pl.program_id(1) * FOX_HEADS_PER_STEP + g]
        return _Head(qt_ref.at[g], ones_rows, k_ref.at[g], lambda j, hh: vt_ref[g, j, hh * rows:(hh + 1) * rows, :],
                     corr_ref.at[0], acc, finish)

    _attend([head(g) for g in range(qt_ref.shape[0])], z_ref)


def _fox_attention(qt, k, vt, pat):
    b, npair, nb, _, _ = qt.shape
    s = nb * TB
    rows = vt.shape[3] // 2
    q_spec, k_spec, v_spec, o_spec = _attention_specs(s, 2 * rows, FOX_HEADS_PER_STEP)
    return pl.pallas_call(
        _fox_kernel,
        grid=(b, npair // FOX_HEADS_PER_STEP),
        in_specs=[q_spec, k_spec, v_spec, _const_spec(pat.shape)],
        out_specs=o_spec,
        out_shape=jax.ShapeDtypeStruct((b, npair, s, LANES), BF16),
        scratch_shapes=_attention_scratch(rows, FOX_HEADS_PER_STEP),
        compiler_params=_params("parallel", "parallel"),
        name="fox_attention",
    )(qt, k, vt, pat)


def _merge_kernel(x_ref, nm_ref, oa_ref, of_ref, qm_ref, km_ref, vm_ref, wg_ref, bg_ref,
                  wda_ref, wfx_ref, wmem_ref, wout_ref, o_ref, om_ref, merged_ref):
    x = x_ref[...]
    h = _rms_rows(x, nm_ref[...]).astype(BF16)

    for hd in range(MEM_HEADS):
        sl = slice(hd * MEM_DIM, (hd + 1) * MEM_DIM)
        s = _dot_nt(qm_ref[:, sl], km_ref[:, sl])
        p = jnp.exp2(s - jnp.max(s, axis=1, keepdims=True))
        inv = 1.0 / jnp.sum(p, axis=1, keepdims=True)
        om_ref[:, sl] = (_dot(p.astype(BF16), vm_ref[:, sl]) * inv).astype(BF16)

    oa = jnp.concatenate([oa_ref[hd] for hd in range(oa_ref.shape[0])], axis=1)
    of = jnp.concatenate([of_ref[hd] for hd in range(of_ref.shape[0])], axis=1)
    branches = ((oa, wda_ref), (of, wfx_ref), (om_ref[...], wmem_ref))
    for n in range(D_MODEL // MXU_DIM):
        sl = slice(n * MXU_DIM, (n + 1) * MXU_DIM)
        merged = None
        for br, (o_br, w_br) in enumerate(branches):
            gsl = slice(br * D_MODEL + n * MXU_DIM, br * D_MODEL + (n + 1) * MXU_DIM)
            gate = jax.nn.sigmoid(_dot(h, wg_ref[:, gsl]) + bg_ref[:, gsl])
            term = gate * _dot(o_br, w_br[:, sl])
            merged = term if merged is None else merged + term
        merged_ref[:, sl] = merged.astype(BF16)

    o_ref[...] = x + _dot(merged_ref[...], wout_ref[...])


def _merge(x, nm, oa, of, qm, km, vm, wg, bg, wda, wfx, wmem, wout):
    b, s, d = x.shape
    tm = TM_PROJ
    row = lambda width: pl.BlockSpec((None, tm, width), lambda bi, i: (bi, i, 0))
    heads = pl.BlockSpec((None, oa.shape[1], tm, LANES), lambda bi, i: (bi, 0, i, 0))
    mem_spec = pl.BlockSpec((None, N_MEM, MEM_WIDTH), lambda bi, i: (bi, 0, 0))
    return pl.pallas_call(
        _merge_kernel,
        grid=(b, s // tm),
        in_specs=[row(d), _const_spec(nm.shape), heads, heads, row(MEM_WIDTH), mem_spec, mem_spec,
                  _const_spec(wg.shape), _const_spec(bg.shape), _const_spec(wda.shape), _const_spec(wfx.shape),
                  _const_spec(wmem.shape), _const_spec(wout.shape)],
        out_specs=row(d),
        out_shape=jax.ShapeDtypeStruct((b, s, d), F32),
        scratch_shapes=[pltpu.VMEM((tm, MEM_WIDTH), BF16), pltpu.VMEM((tm, d), BF16)],
        compiler_params=_params("parallel", "parallel"),
        name="merge",
    )(x, nm, oa, of, qm, km, vm, wg, bg, wda, wfx, wmem, wout)


def _ffn_kernel(x_ref, nf_ref, wup_ref, cw_ref, cb_ref, wdn_ref, o_ref, carry_ref, act_ref):
    tm = x_ref.shape[0]

    @pl.when(pl.program_id(1) == 0)
    def _():
        carry_ref[...] = jnp.zeros_like(carry_ref)

    x = x_ref[...]
    half = tm // 2
    halves = [_rms_rows(x[r * half:(r + 1) * half], nf_ref[...]).astype(BF16) for r in range(2)]
    h2 = jnp.concatenate(halves, axis=0)
    row = lax.broadcasted_iota(jnp.int32, (SUBLANES, FC), 0)

    def conv(col):
        sl = slice(col, col + FC)
        if col == 0:
            u = jnp.concatenate([_dot(hh, wup_ref[:, sl]) for hh in halves], axis=0)
        else:
            u = _dot(h2, wup_ref[:, sl])
        prev = carry_ref[:, sl]
        carry_ref[:, sl] = u[tm - SUBLANES:tm, :]
        out = u * cw_ref[CONV_W - 1:CONV_W, sl] + cb_ref[:, sl]
        for shift in range(1, CONV_W):
            us = pltpu.roll(u, shift, 0)
            head = jnp.where(row < shift, pltpu.roll(prev, shift, 0), us[0:SUBLANES, :])
            us = jnp.concatenate([head, us[SUBLANES:, :]], axis=0)
            out = out + us * cw_ref[CONV_W - 1 - shift:CONV_W - shift, sl]
        return out

    for ch in range(D_FF // FC):
        a = conv(ch * FC)
        g = conv(D_FF + ch * FC)
        act_ref[:, ch * FC:(ch + 1) * FC] = (a * jax.nn.sigmoid(a) * g).astype(BF16)

    o_ref[...] = x + _dot(act_ref[...], wdn_ref[...])


def _ffn(x, nf, wup, cw, cb, wdn):
    b, s, d = x.shape
    tm = TM_FFN
    row = pl.BlockSpec((None, tm, d), lambda bi, i: (bi, i, 0))
    return pl.pallas_call(
        _ffn_kernel,
        grid=(b, s // tm),
        in_specs=[row, _const_spec(nf.shape), _const_spec(wup.shape), _const_spec(cw.shape), _const_spec(cb.shape),
                  _const_spec(wdn.shape)],
        out_specs=row,
        out_shape=jax.ShapeDtypeStruct((b, s, d), F32),
        scratch_shapes=[pltpu.VMEM((SUBLANES, 2 * D_FF), F32), pltpu.VMEM((tm, D_FF), BF16)],
        compiler_params=_params("parallel", "arbitrary"),
        name="conv_mlp",
    )(x, nf, wup, cw, cb, wdn)


def _group_mean_matrix(group):
    idx = np.arange(MXU_DIM) // group
    return jnp.asarray((idx[:, None] == idx[None, :]) / group, dtype=BF16)


def _alibi_key_table(s):
    slopes = np.asarray([2.0 ** (-8.0 * (i + 1) / DA_HEADS) * LOG2E for i in range(DA_HEADS)], np.float32)
    rest = np.arange(s, dtype=np.float32)[None, :] * slopes[:, None]
    table = np.zeros((DA_HEADS, 2, s, LANES), BF16)
    for p in range(N_SPLIT):
        piece = rest.astype(BF16)
        rest = rest - piece.astype(np.float32)
        table[:, 0, :, HALF + p] = piece
        table[:, 1, :, p] = piece
    return table


def _gate_lane_layout():
    head = np.full((LANES,), -1, np.int32)
    part = np.full((LANES,), N_SPLIT, np.int32)
    for a in range(FX_HEADS // 2):
        for p in range(N_SPLIT):
            head[HALF + N_SPLIT * a + p] = 2 * a
            head[N_SPLIT * a + p] = 2 * a + 1
            part[HALF + N_SPLIT * a + p] = p
            part[N_SPLIT * a + p] = p
    return head, part


def kernel(x, mem, norm_mix, w_in, b_gate, da_q_norm, da_k_norm, da_lambda_q1, da_lambda_k1, da_lambda_q2,
           da_lambda_k2, da_subln, fx_q_norm, fx_k_norm, fx_f_bias, mem_norm, w_mem_kv, mem_q_norm, mem_k_norm,
           w_branch_da, w_branch_fx, w_branch_mem, w_out, norm_ffn, w_up, conv_w, conv_b, w_down):
    b, s, d = x.shape
    depth = w_in.shape[0]
    off = np.cumsum(np.array(IN_SIZES))[:-1].tolist()
    slopes = jnp.asarray([2.0 ** (-8.0 * (i + 1) / DA_HEADS) for i in range(DA_HEADS)], dtype=F32)
    g64 = _group_mean_matrix(DA_QK_DIM)
    g128 = _group_mean_matrix(MEM_DIM)
    tri = jnp.asarray(np.tril(np.ones((MXU_DIM, MXU_DIM))), dtype=BF16)
    lane_head, lane_part = _gate_lane_layout()
    lane_used = jnp.asarray(lane_head >= 0)
    lane_src = jnp.asarray(np.maximum(lane_head, 0))
    part = jnp.asarray(lane_part).reshape(1, LANES)
    kpos = jnp.asarray(_alibi_key_table(s))
    pat_np = np.zeros((FX_HEADS // 2, HALF, TB), np.float32)
    for a in range(FX_HEADS // 2):
        pat_np[a, N_SPLIT * a:N_SPLIT * (a + 1), :] = 1.0
    pat = jnp.asarray(pat_np, dtype=BF16)
    row = lambda v: v.reshape(1, -1).astype(F32)
    col = lambda v: jnp.broadcast_to(v.reshape(-1, 1).astype(F32), (v.size, LANES))

    for l in range(depth):
        lam_init = 0.8 - 0.6 * math.exp(-0.3 * l)
        a_q, a_k, a_v, f_q, f_k, f_v, f_g, m_q, w_g = jnp.split(w_in[l], off, axis=-1)
        wqt = jnp.concatenate([a_q, f_q], axis=1).T.astype(BF16)
        wvt = jnp.concatenate([a_v, f_v], axis=1).T.astype(BF16)
        wfg = jnp.where(lane_used[None, :], f_g[:, lane_src], 0.0)
        wtok = jnp.concatenate([a_k, f_k, m_q, wfg], axis=1).astype(BF16)
        fb = jnp.where(lane_used, fx_f_bias[l][lane_src], 0.0).reshape(1, LANES).astype(F32)
        gqt = col(jnp.concatenate([jnp.tile(da_q_norm[l], 2 * DA_HEADS) * (DA_QK_DIM ** -0.5 * LOG2E),
                                   jnp.tile(fx_q_norm[l], FX_HEADS) * (FX_DIM ** -0.5 * LOG2E)]))
        gtok = row(jnp.concatenate([jnp.tile(da_k_norm[l], 2 * DA_HEADS), jnp.tile(fx_k_norm[l], FX_HEADS),
                                    jnp.tile(mem_q_norm[l], MEM_HEADS) * (MEM_DIM ** -0.5 * LOG2E)]))
        gmk = row(jnp.tile(mem_k_norm[l], MEM_HEADS))

        qat, qft, ka, kf, vat, vft, qm = _in_projection(
            x, row(norm_mix[l]), wqt, wvt, wtok, gqt, gtok, fb, part, g64, g128, tri, kpos)
        km, vm = _memory_kv(mem, row(mem_norm[l]), w_mem_kv[l].astype(BF16), gmk, g128)

        oa = _diff_attention(qat, ka.reshape(b, DA_HEADS, 2, s, LANES), vat, pat, slopes,
                             row(da_lambda_q1[l]), row(da_lambda_k1[l]), row(da_lambda_q2[l]),
                             row(da_lambda_k2[l]), col(da_subln[l]), lam_init)
        of = _fox_attention(qft, kf.reshape(b, FX_HEADS // 2, 2, s, LANES), vft, pat)

        x = _merge(x, row(norm_mix[l]), oa, of, qm, km, vm, w_g.astype(BF16), b_gate[l].reshape(1, -1).astype(F32),
                   w_branch_da[l].astype(BF16), w_branch_fx[l].astype(BF16), w_branch_mem[l].astype(BF16),
                   w_out[l].astype(BF16))
        x = _ffn(x, row(norm_ffn[l]), w_up[l].astype(BF16), conv_w[l].astype(F32), row(conv_b[l]),
                 w_down[l].astype(BF16))
    return x
```
